```python
import math
import jax, jax.numpy as jnp
from jax import lax
import numpy as np

D_MODEL = 2048
BATCH = 8
SEQ = 8192
DEPTH = 4

PLE_DIM = 256
D_A = D_MODEL // 2
A_GROUPS = 8
CONV_A_WIDTH = 31
D_B = D_MODEL // 2
RG_HEADS = 8
RG_HEAD_DIM = D_B // RG_HEADS
CONV_B_WIDTH = 4
RG_C = 8.0
SB_HEADS = 16
SB_HEAD_DIM = D_MODEL // SB_HEADS
BLOCK_Q = 128
D_FF = 4 * D_MODEL
N_EVEN = (DEPTH + 1) // 2
N_ODD = DEPTH // 2
EPS = 1e-6
D_IN_REC = 2 * D_A + 2 * D_B

kernel_name = "hybrid_conformer_rglru_stickbreaking_trunk"


def rms_norm(x, g):
    xf = x.astype(jnp.float32)
    y = xf * lax.rsqrt(jnp.mean(xf * xf, axis=-1, keepdims=True) + EPS)
    return (y * g.astype(jnp.float32)).astype(x.dtype)


def layer_norm(x, g, b):
    xf = x.astype(jnp.float32)
    mu = jnp.mean(xf, axis=-1, keepdims=True)
    var = jnp.mean(jnp.square(xf - mu), axis=-1, keepdims=True)
    y = (xf - mu) * lax.rsqrt(var + EPS)
    return (y * g.astype(jnp.float32) + b.astype(jnp.float32)).astype(x.dtype)


def causal_depthwise_conv(x, w, b):
    k_width = w.shape[0]
    out = lax.conv_general_dilated(
        x, w[:, None, :].astype(x.dtype), window_strides=(1,), padding=[(k_width - 1, 0)],
        dimension_numbers=("NWC", "WIO", "NWC"), feature_group_count=x.shape[-1])
    return out + b


def rg_lru(x, w_a, b_a, w_x, b_x, lam):
    bsz, seq, ch = x.shape
    xh = x.reshape(bsz, seq, RG_HEADS, RG_HEAD_DIM)
    r = jax.nn.sigmoid(jnp.einsum("bshi,hij->bshj", xh, w_a).reshape(bsz, seq, ch) + b_a)
    i = jax.nn.sigmoid(jnp.einsum("bshi,hij->bshj", xh, w_x).reshape(bsz, seq, ch) + b_x)
    log_a = -RG_C * r.astype(jnp.float32) * jax.nn.softplus(-lam.astype(jnp.float32))
    a = jnp.exp(log_a)
    mult = jnp.sqrt(-jnp.expm1(2.0 * log_a))
    u = mult * (i * x).astype(jnp.float32)

    def combine(c1, c2):
        a1, u1 = c1
        a2, u2 = c2
        return a1 * a2, a2 * u1 + u2

    _, h = lax.associative_scan(combine, (a, u), axis=1)
    return h.astype(x.dtype)


def conv_recurrent_mixer(hn, w_in, conv_a_w, conv_a_b, ln_a_g, ln_a_b,
                         conv_b_w, conv_b_b, w_rg_a, b_rg_a, w_rg_x, b_rg_x, rg_lambda, w_out):
    u = hn @ w_in
    a_val, a_gate, xr, gr = jnp.split(u, [D_A, 2 * D_A, 2 * D_A + D_B], axis=-1)
    ya = a_val * jax.nn.sigmoid(a_gate)
    ya = causal_depthwise_conv(ya, conv_a_w, conv_a_b)
    ya = jax.nn.silu(layer_norm(ya, ln_a_g, ln_a_b))
    xr = causal_depthwise_conv(xr, conv_b_w, conv_b_b)
    yb = rg_lru(xr, w_rg_a, b_rg_a, w_rg_x, b_rg_x, rg_lambda) * jax.nn.gelu(gr)
    return jnp.concatenate([ya, yb], axis=-1) @ w_out


def stick_breaking_attention(q, k, v):
    bsz, nh, seq, dh = q.shape
    n_blocks = seq // BLOCK_Q
    scale = 1.0 / math.sqrt(dh)
    q_blocks = q.reshape(bsz, nh, n_blocks, BLOCK_Q, dh).transpose(2, 0, 1, 3, 4)
    key_pos = jnp.arange(seq)

    def one_block(args):
        q_blk, blk_idx = args
        z = jnp.einsum("bhqd,bhkd->bhqk", q_blk, k,
                       preferred_element_type=jnp.float32) * scale
        q_pos = blk_idx * BLOCK_Q + jnp.arange(BLOCK_Q)
        mask = key_pos[None, :] < q_pos[:, None]
        log_one_minus = jnp.where(mask, -jax.nn.softplus(z), 0.0)
        key_axis = log_one_minus.ndim - 1
        tail = lax.cumsum(log_one_minus, axis=key_axis, reverse=True) - log_one_minus
        log_w = jax.nn.log_sigmoid(z) + tail
        w = jnp.where(mask, jnp.exp(log_w), 0.0)
        return jnp.einsum("bhqk,bhkd->bhqd", w.astype(v.dtype), v)

    out = lax.map(one_block, (q_blocks, jnp.arange(n_blocks)))
    return out.transpose(1, 2, 0, 3, 4).reshape(bsz, nh, seq, dh)


def stick_breaking_mixer(hn, w_qkv, w_o):
    bsz, seq, _ = hn.shape
    qkv = hn @ w_qkv
    q, k, v = jnp.split(qkv, 3, axis=-1)
    to_heads = lambda t: t.reshape(bsz, seq, SB_HEADS, SB_HEAD_DIM).transpose(0, 2, 1, 3)
    o = stick_breaking_attention(to_heads(q), to_heads(k), to_heads(v))
    o = o.transpose(0, 2, 1, 3).reshape(bsz, seq, D_MODEL)
    return o @ w_o


def squared_relu_mlp(hn, w_up, w_down):
    return jnp.square(jax.nn.relu(hn @ w_up)) @ w_down


def _fwd_setup_inputs(seed: int = 0) -> dict:
    key = jax.random.key(seed)
    ks = jax.random.split(key, 32)
    f32 = jnp.float32
    nrm = lambda k, shape, fan_in: jax.random.normal(k, shape, f32) * (fan_in ** -0.5)
    gain = lambda k, shape: 1.0 + 0.05 * jax.random.normal(k, shape, f32)
    small = lambda k, shape: 0.02 * jax.random.normal(k, shape, f32)
    a_c = jax.random.uniform(ks[16], (N_EVEN, D_B), f32, 0.9, 0.999)
    a0 = a_c ** (1.0 / RG_C)
    rg_lambda = jnp.log(a0) - jnp.log1p(-a0)
    return {
        "x": jax.random.normal(ks[0], (BATCH, SEQ, D_MODEL), f32),
        "p": jax.random.normal(ks[1], (DEPTH, BATCH, SEQ, PLE_DIM), f32),
        "norm_mix_g": gain(ks[2], (DEPTH, D_MODEL)),
        "norm_mlp_g": gain(ks[3], (DEPTH, D_MODEL)),
        "norm_ple_g": gain(ks[4], (DEPTH, D_MODEL)),
        "norm_f_g": gain(ks[5], (D_MODEL,)),
        "w_in_rec": nrm(ks[6], (N_EVEN, D_MODEL, D_IN_REC), D_MODEL),
        "conv_a_w": nrm(ks[7], (N_EVEN, CONV_A_WIDTH, D_A), CONV_A_WIDTH),
        "conv_a_b": small(ks[8], (N_EVEN, D_A)),
        "ln_a_g": gain(ks[9], (N_EVEN, D_A)),
        "ln_a_b": small(ks[10], (N_EVEN, D_A)),
        "conv_b_w": nrm(ks[11], (N_EVEN, CONV_B_WIDTH, D_B), CONV_B_WIDTH),
        "conv_b_b": small(ks[12], (N_EVEN, D_B)),
        "w_rg_a": nrm(ks[13], (N_EVEN, RG_HEADS, RG_HEAD_DIM, RG_HEAD_DIM), RG_HEAD_DIM),
        "b_rg_a": small(ks[14], (N_EVEN, D_B)),
        "w_rg_x": nrm(ks[15], (N_EVEN, RG_HEADS, RG_HEAD_DIM, RG_HEAD_DIM), RG_HEAD_DIM),
        "b_rg_x": small(ks[17], (N_EVEN, D_B)),
        "rg_lambda": rg_lambda,
        "w_out_rec": nrm(ks[18], (N_EVEN, D_A + D_B, D_MODEL), D_A + D_B),
        "w_qkv": nrm(ks[19], (N_ODD, D_MODEL, 3 * D_MODEL), D_MODEL),
        "w_o_attn": nrm(ks[20], (N_ODD, D_MODEL, D_MODEL), D_MODEL),
        "w_mlp_up": nrm(ks[21], (DEPTH, D_MODEL, D_FF), D_MODEL),
        "w_mlp_down": nrm(ks[22], (DEPTH, D_FF, D_MODEL), D_FF),
        "w_ple_proj": nrm(ks[23], (DEPTH, PLE_DIM, D_MODEL), PLE_DIM),
        "w_ple_gate": nrm(ks[24], (DEPTH, D_MODEL, D_MODEL), D_MODEL),
    }


def _fwd_reference(x, p, norm_mix_g, norm_mlp_g, norm_ple_g, norm_f_g, w_in_rec, conv_a_w, conv_a_b,
              ln_a_g, ln_a_b, conv_b_w, conv_b_b, w_rg_a, b_rg_a, w_rg_x, b_rg_x, rg_lambda,
              w_out_rec, w_qkv, w_o_attn, w_mlp_up, w_mlp_down, w_ple_proj, w_ple_gate):
    h = x
    for i in range(DEPTH):
        j = i // 2
        hn = rms_norm(h, norm_mix_g[i])
        if i % 2 == 0:
            mix = conv_recurrent_mixer(hn, w_in_rec[j], conv_a_w[j], conv_a_b[j], ln_a_g[j], ln_a_b[j],
                                       conv_b_w[j], conv_b_b[j], w_rg_a[j], b_rg_a[j], w_rg_x[j],
                                       b_rg_x[j], rg_lambda[j], w_out_rec[j])
        else:
            mix = stick_breaking_mixer(hn, w_qkv[j], w_o_attn[j])
        h = h + mix
        h = h + squared_relu_mlp(rms_norm(h, norm_mlp_g[i]), w_mlp_up[i], w_mlp_down[i])
        gate = jax.nn.sigmoid(rms_norm(h, norm_ple_g[i]) @ w_ple_gate[i])
        h = h + (p[i] @ w_ple_proj[i]) * gate
    return rms_norm(h, norm_f_g)


import jax as _jax
import jax.numpy as _jnp

TWIN_FORMAT = 'train_step'
FWD_PARAMS = ['x', 'p', 'norm_mix_g', 'norm_mlp_g', 'norm_ple_g', 'norm_f_g', 'w_in_rec', 'conv_a_w', 'conv_a_b', 'ln_a_g', 'ln_a_b', 'conv_b_w', 'conv_b_b', 'w_rg_a', 'b_rg_a', 'w_rg_x', 'b_rg_x', 'rg_lambda', 'w_out_rec', 'w_qkv', 'w_o_attn', 'w_mlp_up', 'w_mlp_down', 'w_ple_proj', 'w_ple_gate']
TWIN_WEIGHTS = ['norm_mix_g', 'norm_mlp_g', 'norm_ple_g', 'norm_f_g', 'w_in_rec', 'conv_a_w', 'conv_a_b', 'ln_a_g', 'ln_a_b', 'conv_b_w', 'conv_b_b', 'w_rg_a', 'b_rg_a', 'w_rg_x', 'b_rg_x', 'rg_lambda', 'w_out_rec', 'w_qkv', 'w_o_attn', 'w_mlp_up', 'w_mlp_down', 'w_ple_proj', 'w_ple_gate']
TWIN_DIFF_INPUT = 'x'
TWIN_INPUTS = ['x', 'p', 'norm_mix_g', 'norm_mlp_g', 'norm_ple_g', 'norm_f_g', 'w_in_rec', 'conv_a_w', 'conv_a_b', 'ln_a_g', 'ln_a_b', 'conv_b_w', 'conv_b_b', 'w_rg_a', 'b_rg_a', 'w_rg_x', 'b_rg_x', 'rg_lambda', 'w_out_rec', 'w_qkv', 'w_o_attn', 'w_mlp_up', 'w_mlp_down', 'w_ple_proj', 'w_ple_gate', 'loss_target', 'm_norm_mix_g', 'm_norm_mlp_g', 'm_norm_ple_g', 'm_norm_f_g', 'm_w_in_rec', 'm_conv_a_w', 'm_conv_a_b', 'm_ln_a_g', 'm_ln_a_b', 'm_conv_b_w', 'm_conv_b_b', 'm_w_rg_a', 'm_b_rg_a', 'm_w_rg_x', 'm_b_rg_x', 'm_rg_lambda', 'm_w_out_rec', 'm_w_qkv', 'm_w_o_attn', 'm_w_mlp_up', 'm_w_mlp_down', 'm_w_ple_proj', 'm_w_ple_gate', 'v_norm_mix_g', 'v_norm_mlp_g', 'v_norm_ple_g', 'v_norm_f_g', 'v_w_in_rec', 'v_conv_a_w', 'v_conv_a_b', 'v_ln_a_g', 'v_ln_a_b', 'v_conv_b_w', 'v_conv_b_b', 'v_w_rg_a', 'v_b_rg_a', 'v_w_rg_x', 'v_b_rg_x', 'v_rg_lambda', 'v_w_out_rec', 'v_w_qkv', 'v_w_o_attn', 'v_w_mlp_up', 'v_w_mlp_down', 'v_w_ple_proj', 'v_w_ple_gate']
TWIN_OUTPUTS = ['loss', 'grad_x', 'grad_norm_mix_g', 'grad_norm_mlp_g', 'grad_norm_ple_g', 'grad_norm_f_g', 'grad_w_in_rec', 'grad_conv_a_w', 'grad_conv_a_b', 'grad_ln_a_g', 'grad_ln_a_b', 'grad_conv_b_w', 'grad_conv_b_b', 'grad_w_rg_a', 'grad_b_rg_a', 'grad_w_rg_x', 'grad_b_rg_x', 'grad_rg_lambda', 'grad_w_out_rec', 'grad_w_qkv', 'grad_w_o_attn', 'grad_w_mlp_up', 'grad_w_mlp_down', 'grad_w_ple_proj', 'grad_w_ple_gate', 'delta_norm_mix_g', 'delta_norm_mlp_g', 'delta_norm_ple_g', 'delta_norm_f_g', 'delta_w_in_rec', 'delta_conv_a_w', 'delta_conv_a_b', 'delta_ln_a_g', 'delta_ln_a_b', 'delta_conv_b_w', 'delta_conv_b_b', 'delta_w_rg_a', 'delta_b_rg_a', 'delta_w_rg_x', 'delta_b_rg_x', 'delta_rg_lambda', 'delta_w_out_rec', 'delta_w_qkv', 'delta_w_o_attn', 'delta_w_mlp_up', 'delta_w_mlp_down', 'delta_w_ple_proj', 'delta_w_ple_gate', 'new_m_norm_mix_g', 'new_m_norm_mlp_g', 'new_m_norm_ple_g', 'new_m_norm_f_g', 'new_m_w_in_rec', 'new_m_conv_a_w', 'new_m_conv_a_b', 'new_m_ln_a_g', 'new_m_ln_a_b', 'new_m_conv_b_w', 'new_m_conv_b_b', 'new_m_w_rg_a', 'new_m_b_rg_a', 'new_m_w_rg_x', 'new_m_b_rg_x', 'new_m_rg_lambda', 'new_m_w_out_rec', 'new_m_w_qkv', 'new_m_w_o_attn', 'new_m_w_mlp_up', 'new_m_w_mlp_down', 'new_m_w_ple_proj', 'new_m_w_ple_gate', 'new_v_norm_mix_g', 'new_v_norm_mlp_g', 'new_v_norm_ple_g', 'new_v_norm_f_g', 'new_v_w_in_rec', 'new_v_conv_a_w', 'new_v_conv_a_b', 'new_v_ln_a_g', 'new_v_ln_a_b', 'new_v_conv_b_w', 'new_v_conv_b_b', 'new_v_w_rg_a', 'new_v_b_rg_a', 'new_v_w_rg_x', 'new_v_b_rg_x', 'new_v_rg_lambda', 'new_v_w_out_rec', 'new_v_w_qkv', 'new_v_w_o_attn', 'new_v_w_mlp_up', 'new_v_w_mlp_down', 'new_v_w_ple_proj', 'new_v_w_ple_gate']
TWIN_LEAF_KINDS = {'loss': 'loss', 'grad_x': 'grad_x', 'grad_norm_mix_g': 'grad_w', 'grad_norm_mlp_g': 'grad_w', 'grad_norm_ple_g': 'grad_w', 'grad_norm_f_g': 'grad_w', 'grad_w_in_rec': 'grad_w', 'grad_conv_a_w': 'grad_w', 'grad_conv_a_b': 'grad_w', 'grad_ln_a_g': 'grad_w', 'grad_ln_a_b': 'grad_w', 'grad_conv_b_w': 'grad_w', 'grad_conv_b_b': 'grad_w', 'grad_w_rg_a': 'grad_w', 'grad_b_rg_a': 'grad_w', 'grad_w_rg_x': 'grad_w', 'grad_b_rg_x': 'grad_w', 'grad_rg_lambda': 'grad_w', 'grad_w_out_rec': 'grad_w', 'grad_w_qkv': 'grad_w', 'grad_w_o_attn': 'grad_w', 'grad_w_mlp_up': 'grad_w', 'grad_w_mlp_down': 'grad_w', 'grad_w_ple_proj': 'grad_w', 'grad_w_ple_gate': 'grad_w', 'delta_norm_mix_g': 'delta_w', 'delta_norm_mlp_g': 'delta_w', 'delta_norm_ple_g': 'delta_w', 'delta_norm_f_g': 'delta_w', 'delta_w_in_rec': 'delta_w', 'delta_conv_a_w': 'delta_w', 'delta_conv_a_b': 'delta_w', 'delta_ln_a_g': 'delta_w', 'delta_ln_a_b': 'delta_w', 'delta_conv_b_w': 'delta_w', 'delta_conv_b_b': 'delta_w', 'delta_w_rg_a': 'delta_w', 'delta_b_rg_a': 'delta_w', 'delta_w_rg_x': 'delta_w', 'delta_b_rg_x': 'delta_w', 'delta_rg_lambda': 'delta_w', 'delta_w_out_rec': 'delta_w', 'delta_w_qkv': 'delta_w', 'delta_w_o_attn': 'delta_w', 'delta_w_mlp_up': 'delta_w', 'delta_w_mlp_down': 'delta_w', 'delta_w_ple_proj': 'delta_w', 'delta_w_ple_gate': 'delta_w', 'new_m_norm_mix_g': 'new_m', 'new_m_norm_mlp_g': 'new_m', 'new_m_norm_ple_g': 'new_m', 'new_m_norm_f_g': 'new_m', 'new_m_w_in_rec': 'new_m', 'new_m_conv_a_w': 'new_m', 'new_m_conv_a_b': 'new_m', 'new_m_ln_a_g': 'new_m', 'new_m_ln_a_b': 'new_m', 'new_m_conv_b_w': 'new_m', 'new_m_conv_b_b': 'new_m', 'new_m_w_rg_a': 'new_m', 'new_m_b_rg_a': 'new_m', 'new_m_w_rg_x': 'new_m', 'new_m_b_rg_x': 'new_m', 'new_m_rg_lambda': 'new_m', 'new_m_w_out_rec': 'new_m', 'new_m_w_qkv': 'new_m', 'new_m_w_o_attn': 'new_m', 'new_m_w_mlp_up': 'new_m', 'new_m_w_mlp_down': 'new_m', 'new_m_w_ple_proj': 'new_m', 'new_m_w_ple_gate': 'new_m', 'new_v_norm_mix_g': 'new_v', 'new_v_norm_mlp_g': 'new_v', 'new_v_norm_ple_g': 'new_v', 'new_v_norm_f_g': 'new_v', 'new_v_w_in_rec': 'new_v', 'new_v_conv_a_w': 'new_v', 'new_v_conv_a_b': 'new_v', 'new_v_ln_a_g': 'new_v', 'new_v_ln_a_b': 'new_v', 'new_v_conv_b_w': 'new_v', 'new_v_conv_b_b': 'new_v', 'new_v_w_rg_a': 'new_v', 'new_v_b_rg_a': 'new_v', 'new_v_w_rg_x': 'new_v', 'new_v_b_rg_x': 'new_v', 'new_v_rg_lambda': 'new_v', 'new_v_w_out_rec': 'new_v', 'new_v_w_qkv': 'new_v', 'new_v_w_o_attn': 'new_v', 'new_v_w_mlp_up': 'new_v', 'new_v_w_mlp_down': 'new_v', 'new_v_w_ple_proj': 'new_v', 'new_v_w_ple_gate': 'new_v'}


def _forward(args):
    return _fwd_reference(*[args[k] for k in FWD_PARAMS])


def _output_shape():
    def fwd():
        inp = _fwd_setup_inputs(0)
        return _fwd_reference(*[inp[k] for k in FWD_PARAMS])
    out = _jax.eval_shape(fwd)
    return out.shape, out.dtype

N_MICROBATCH = 1
ADAM_LR = 0.001
ADAM_B1 = 0.9
ADAM_B2 = 0.999
ADAM_EPS = 1e-08
ADAM_WD = 0.01
ADAM_STEP = 10
PER_EXAMPLE_BATCH_AXIS = {'x': 0, 'p': 1, 'loss_target': 0}
SHARED_INPUTS = []
_WEIGHT_DTYPES = {'norm_mix_g': _jnp.float32, 'norm_mlp_g': _jnp.float32, 'norm_ple_g': _jnp.float32, 'norm_f_g': _jnp.float32, 'w_in_rec': _jnp.float32, 'conv_a_w': _jnp.float32, 'conv_a_b': _jnp.float32, 'ln_a_g': _jnp.float32, 'ln_a_b': _jnp.float32, 'conv_b_w': _jnp.float32, 'conv_b_b': _jnp.float32, 'w_rg_a': _jnp.float32, 'b_rg_a': _jnp.float32, 'w_rg_x': _jnp.float32, 'b_rg_x': _jnp.float32, 'rg_lambda': _jnp.float32, 'w_out_rec': _jnp.float32, 'w_qkv': _jnp.float32, 'w_o_attn': _jnp.float32, 'w_mlp_up': _jnp.float32, 'w_mlp_down': _jnp.float32, 'w_ple_proj': _jnp.float32, 'w_ple_gate': _jnp.float32}
MOMENT_SCALE = {'norm_mix_g': 1.592160e-01, 'norm_mlp_g': 1.191069e-01, 'norm_ple_g': 1.426572e-02, 'norm_f_g': 3.300361e+01, 'w_in_rec': 1.149774e-01, 'conv_a_w': 7.242209e-02, 'conv_a_b': 4.870308e-01, 'ln_a_g': 1.889195e-01, 'ln_a_b': 2.979719e-01, 'conv_b_w': 2.045232e-01, 'conv_b_b': 1.349169e+00, 'w_rg_a': 3.228617e-02, 'b_rg_a': 4.453037e-02, 'w_rg_x': 6.482291e-02, 'b_rg_x': 8.615152e-02, 'rg_lambda': 1.077742e-01, 'w_out_rec': 1.612782e-01, 'w_qkv': 8.085945e-02, 'w_o_attn': 1.479404e-01, 'w_mlp_up': 5.852609e-02, 'w_mlp_down': 2.413319e-01, 'w_ple_proj': 3.089464e-02, 'w_ple_gate': 1.356730e-02}


def _to_microbatches(a, axis):
    t = _jnp.moveaxis(a, axis, 0)
    t = t.reshape((N_MICROBATCH, t.shape[0] // N_MICROBATCH) + t.shape[1:])
    return _jnp.moveaxis(t, 1, axis + 1)


def setup_inputs(seed: int = 0) -> dict:
    inp = _fwd_setup_inputs(seed)
    key = _jax.random.fold_in(_jax.random.key(seed), 7919)
    shape, _ = _output_shape()
    out = dict(inp)
    out["loss_target"] = _jax.random.normal(_jax.random.fold_in(key, 0), shape, _jnp.float32)
    for i, name in enumerate(TWIN_WEIGHTS):
        w = inp[name].astype(_jnp.float32)
        if MOMENT_SCALE is None:
            s = _jnp.sqrt(_jnp.mean(_jnp.square(w)) + 1e-30)
        else:
            s = MOMENT_SCALE[name]
        km, kv = _jax.random.split(_jax.random.fold_in(key, i + 1))
        out[name] = w
        out["m_" + name] = s * _jax.random.normal(km, w.shape, _jnp.float32)
        out["v_" + name] = (s * s) * _jax.random.uniform(kv, w.shape, _jnp.float32, 0.5, 1.5)
    if N_MICROBATCH > 1:
        for name, axis in PER_EXAMPLE_BATCH_AXIS.items():
            out[name] = _to_microbatches(out[name], axis)
    return {'x': out['x'], 'p': out['p'], 'norm_mix_g': out['norm_mix_g'], 'norm_mlp_g': out['norm_mlp_g'], 'norm_ple_g': out['norm_ple_g'], 'norm_f_g': out['norm_f_g'], 'w_in_rec': out['w_in_rec'], 'conv_a_w': out['conv_a_w'], 'conv_a_b': out['conv_a_b'], 'ln_a_g': out['ln_a_g'], 'ln_a_b': out['ln_a_b'], 'conv_b_w': out['conv_b_w'], 'conv_b_b': out['conv_b_b'], 'w_rg_a': out['w_rg_a'], 'b_rg_a': out['b_rg_a'], 'w_rg_x': out['w_rg_x'], 'b_rg_x': out['b_rg_x'], 'rg_lambda': out['rg_lambda'], 'w_out_rec': out['w_out_rec'], 'w_qkv': out['w_qkv'], 'w_o_attn': out['w_o_attn'], 'w_mlp_up': out['w_mlp_up'], 'w_mlp_down': out['w_mlp_down'], 'w_ple_proj': out['w_ple_proj'], 'w_ple_gate': out['w_ple_gate'], 'loss_target': out['loss_target'], 'm_norm_mix_g': out['m_norm_mix_g'], 'm_norm_mlp_g': out['m_norm_mlp_g'], 'm_norm_ple_g': out['m_norm_ple_g'], 'm_norm_f_g': out['m_norm_f_g'], 'm_w_in_rec': out['m_w_in_rec'], 'm_conv_a_w': out['m_conv_a_w'], 'm_conv_a_b': out['m_conv_a_b'], 'm_ln_a_g': out['m_ln_a_g'], 'm_ln_a_b': out['m_ln_a_b'], 'm_conv_b_w': out['m_conv_b_w'], 'm_conv_b_b': out['m_conv_b_b'], 'm_w_rg_a': out['m_w_rg_a'], 'm_b_rg_a': out['m_b_rg_a'], 'm_w_rg_x': out['m_w_rg_x'], 'm_b_rg_x': out['m_b_rg_x'], 'm_rg_lambda': out['m_rg_lambda'], 'm_w_out_rec': out['m_w_out_rec'], 'm_w_qkv': out['m_w_qkv'], 'm_w_o_attn': out['m_w_o_attn'], 'm_w_mlp_up': out['m_w_mlp_up'], 'm_w_mlp_down': out['m_w_mlp_down'], 'm_w_ple_proj': out['m_w_ple_proj'], 'm_w_ple_gate': out['m_w_ple_gate'], 'v_norm_mix_g': out['v_norm_mix_g'], 'v_norm_mlp_g': out['v_norm_mlp_g'], 'v_norm_ple_g': out['v_norm_ple_g'], 'v_norm_f_g': out['v_norm_f_g'], 'v_w_in_rec': out['v_w_in_rec'], 'v_conv_a_w': out['v_conv_a_w'], 'v_conv_a_b': out['v_conv_a_b'], 'v_ln_a_g': out['v_ln_a_g'], 'v_ln_a_b': out['v_ln_a_b'], 'v_conv_b_w': out['v_conv_b_w'], 'v_conv_b_b': out['v_conv_b_b'], 'v_w_rg_a': out['v_w_rg_a'], 'v_b_rg_a': out['v_b_rg_a'], 'v_w_rg_x': out['v_w_rg_x'], 'v_b_rg_x': out['v_b_rg_x'], 'v_rg_lambda': out['v_rg_lambda'], 'v_w_out_rec': out['v_w_out_rec'], 'v_w_qkv': out['v_w_qkv'], 'v_w_o_attn': out['v_w_o_attn'], 'v_w_mlp_up': out['v_w_mlp_up'], 'v_w_mlp_down': out['v_w_mlp_down'], 'v_w_ple_proj': out['v_w_ple_proj'], 'v_w_ple_gate': out['v_w_ple_gate']}


def _loss(weights, diff, rest, loss_target):
    with _jax.named_scope("forward"):
        args = {**rest, TWIN_DIFF_INPUT: diff, **{k: w.astype(_WEIGHT_DTYPES[k]) for k, w in weights.items()}}
        y = _forward(args)
    with _jax.named_scope("loss_head"):
        err = _jnp.square(y.astype(_jnp.float32) - loss_target)
        return 0.5 * _jnp.sum(_jnp.mean(err, axis=-1)) if err.ndim else 0.5 * err


def _adamw(w, g, m, v):
    m = ADAM_B1 * m + (1.0 - ADAM_B1) * g
    v = ADAM_B2 * v + (1.0 - ADAM_B2) * _jnp.square(g)
    m_hat = m / (1.0 - ADAM_B1 ** ADAM_STEP)
    v_hat = v / (1.0 - ADAM_B2 ** ADAM_STEP)
    delta = -ADAM_LR * (m_hat / (_jnp.sqrt(v_hat) + ADAM_EPS) + ADAM_WD * w)
    return delta, m, v


def reference(x, p, norm_mix_g, norm_mlp_g, norm_ple_g, norm_f_g, w_in_rec, conv_a_w, conv_a_b, ln_a_g, ln_a_b, conv_b_w, conv_b_b, w_rg_a, b_rg_a, w_rg_x, b_rg_x, rg_lambda, w_out_rec, w_qkv, w_o_attn, w_mlp_up, w_mlp_down, w_ple_proj, w_ple_gate, loss_target, m_norm_mix_g, m_norm_mlp_g, m_norm_ple_g, m_norm_f_g, m_w_in_rec, m_conv_a_w, m_conv_a_b, m_ln_a_g, m_ln_a_b, m_conv_b_w, m_conv_b_b, m_w_rg_a, m_b_rg_a, m_w_rg_x, m_b_rg_x, m_rg_lambda, m_w_out_rec, m_w_qkv, m_w_o_attn, m_w_mlp_up, m_w_mlp_down, m_w_ple_proj, m_w_ple_gate, v_norm_mix_g, v_norm_mlp_g, v_norm_ple_g, v_norm_f_g, v_w_in_rec, v_conv_a_w, v_conv_a_b, v_ln_a_g, v_ln_a_b, v_conv_b_w, v_conv_b_b, v_w_rg_a, v_b_rg_a, v_w_rg_x, v_b_rg_x, v_rg_lambda, v_w_out_rec, v_w_qkv, v_w_o_attn, v_w_mlp_up, v_w_mlp_down, v_w_ple_proj, v_w_ple_gate):
    given = dict(x=x, p=p, norm_mix_g=norm_mix_g, norm_mlp_g=norm_mlp_g, norm_ple_g=norm_ple_g, norm_f_g=norm_f_g, w_in_rec=w_in_rec, conv_a_w=conv_a_w, conv_a_b=conv_a_b, ln_a_g=ln_a_g, ln_a_b=ln_a_b, conv_b_w=conv_b_w, conv_b_b=conv_b_b, w_rg_a=w_rg_a, b_rg_a=b_rg_a, w_rg_x=w_rg_x, b_rg_x=b_rg_x, rg_lambda=rg_lambda, w_out_rec=w_out_rec, w_qkv=w_qkv, w_o_attn=w_o_attn, w_mlp_up=w_mlp_up, w_mlp_down=w_mlp_down, w_ple_proj=w_ple_proj, w_ple_gate=w_ple_gate, loss_target=loss_target, m_norm_mix_g=m_norm_mix_g, m_norm_mlp_g=m_norm_mlp_g, m_norm_ple_g=m_norm_ple_g, m_norm_f_g=m_norm_f_g, m_w_in_rec=m_w_in_rec, m_conv_a_w=m_conv_a_w, m_conv_a_b=m_conv_a_b, m_ln_a_g=m_ln_a_g, m_ln_a_b=m_ln_a_b, m_conv_b_w=m_conv_b_w, m_conv_b_b=m_conv_b_b, m_w_rg_a=m_w_rg_a, m_b_rg_a=m_b_rg_a, m_w_rg_x=m_w_rg_x, m_b_rg_x=m_b_rg_x, m_rg_lambda=m_rg_lambda, m_w_out_rec=m_w_out_rec, m_w_qkv=m_w_qkv, m_w_o_attn=m_w_o_attn, m_w_mlp_up=m_w_mlp_up, m_w_mlp_down=m_w_mlp_down, m_w_ple_proj=m_w_ple_proj, m_w_ple_gate=m_w_ple_gate, v_norm_mix_g=v_norm_mix_g, v_norm_mlp_g=v_norm_mlp_g, v_norm_ple_g=v_norm_ple_g, v_norm_f_g=v_norm_f_g, v_w_in_rec=v_w_in_rec, v_conv_a_w=v_conv_a_w, v_conv_a_b=v_conv_a_b, v_ln_a_g=v_ln_a_g, v_ln_a_b=v_ln_a_b, v_conv_b_w=v_conv_b_w, v_conv_b_b=v_conv_b_b, v_w_rg_a=v_w_rg_a, v_b_rg_a=v_b_rg_a, v_w_rg_x=v_w_rg_x, v_b_rg_x=v_b_rg_x, v_rg_lambda=v_rg_lambda, v_w_out_rec=v_w_out_rec, v_w_qkv=v_w_qkv, v_w_o_attn=v_w_o_attn, v_w_mlp_up=v_w_mlp_up, v_w_mlp_down=v_w_mlp_down, v_w_ple_proj=v_w_ple_proj, v_w_ple_gate=v_w_ple_gate)
    weights = {n: given[n] for n in TWIN_WEIGHTS}
    shared = {n: given[n] for n in SHARED_INPUTS}
    per_example = {n: given[n] for n in ['x', 'p']}
    grad_fn = _jax.value_and_grad(_loss, argnums=(0, 1))

    def one_microbatch(ex, loss_target):
        ex = dict(ex)
        diff = ex.pop(TWIN_DIFF_INPUT)
        return grad_fn(weights, diff, {**shared, **ex}, loss_target)

    if N_MICROBATCH == 1:
        loss, (grad_w, grad_x) = one_microbatch(per_example, given["loss_target"])
    else:
        def body(carry, xs):
            loss_sum, grad_sum = carry
            l_k, (gw_k, gx_k) = one_microbatch(xs[0], xs[1])
            with _jax.named_scope("update"):
                return (loss_sum + l_k, _jax.tree.map(_jnp.add, grad_sum, gw_k)), gx_k

        init = (_jnp.zeros((), _jnp.float32), _jax.tree.map(_jnp.zeros_like, weights))
        (loss, grad_w), grad_x = _jax.lax.scan(body, init, (per_example, given["loss_target"]))
    with _jax.named_scope("update"):
        delta_w, new_m, new_v = {}, {}, {}
        for n in TWIN_WEIGHTS:
            delta_w[n], new_m[n], new_v[n] = _adamw(weights[n], grad_w[n], given["m_" + n], given["v_" + n])
    return (loss, grad_x, *[grad_w[n] for n in TWIN_WEIGHTS], *[delta_w[n] for n in TWIN_WEIGHTS],
            *[new_m[n] for n in TWIN_WEIGHTS], *[new_v[n] for n in TWIN_WEIGHTS])
```

```python
import functools
import math

import jax
import jax.numpy as jnp
from jax import lax
from jax.experimental import pallas as pl
from jax.experimental.pallas import tpu as pltpu

F32, BF16 = jnp.float32, jnp.bfloat16
EPS = 1e-6
N_DEV = 8
SB_HEADS = 16
RG_C = 8.0
HALO_A = 32
HALO_B = 8
LANES = 128
VMEM_LIMIT = 48 * 1024 * 1024
ADAM_LR, ADAM_B1, ADAM_B2, ADAM_EPS, ADAM_WD, ADAM_STEP = 0.001, 0.9, 0.999, 1e-08, 0.01, 10
MESH = pl.DeviceIdType.MESH
SDS = jax.ShapeDtypeStruct
ANY = pl.BlockSpec(memory_space=pl.ANY)


def _cp(*sem):
    return pltpu.CompilerParams(dimension_semantics=sem, vmem_limit_bytes=VMEM_LIMIT)


def _tile(dim, pref, align=LANES):
    if dim <= pref:
        return dim
    t = (pref // align) * align
    while t >= align:
        if dim % t == 0:
            return t
        t -= align
    return dim


def _softplus(z):
    return jnp.maximum(z, 0.0) + jnp.log(1.0 + jnp.exp(-jnp.abs(z)))


def _expm1(x):
    t = x * (1.0 + x * (0.5 + x * (1.0 / 6.0 + x * (1.0 / 24.0 + x * (1.0 / 120.0)))))
    return jnp.where(jnp.abs(x) < 0.1, t, jnp.exp(x) - 1.0)


_GELU_C = math.sqrt(2.0 / math.pi)


def _gelu(x):
    return 0.5 * x * (1.0 + jnp.tanh(_GELU_C * (x + 0.044715 * x * x * x)))


def _gelu_grad(x):
    th = jnp.tanh(_GELU_C * (x + 0.044715 * x * x * x))
    return 0.5 * (1.0 + th) + 0.5 * x * (1.0 - th * th) * _GELU_C * (1.0 + 3.0 * 0.044715 * x * x)


def _dot(a, b):
    return jnp.dot(a, b, preferred_element_type=F32)


def _dot_nt(a, b):
    return lax.dot_general(a, b, (((1,), (1,)), ((), ())), preferred_element_type=F32)


def _dot_tn(a, b):
    return lax.dot_general(a, b, (((0,), (0,)), ((), ())), preferred_element_type=F32)


def _dot_split(x, tri):
    hi = x.astype(BF16)
    lo = (x - hi.astype(F32)).astype(BF16)
    return _dot(hi, tri) + _dot(lo, tri)


def _wspec(shape4, col_sharded, layer, br, bc, rc_of_grid):
    _, _, rows, cols = shape4
    if col_sharded:
        per = cols // bc

        def imap(*g):
            rb, cb = rc_of_grid(*g)
            return (cb // per, layer, rb, cb % per)
    else:
        per = rows // br

        def imap(*g):
            rb, cb = rc_of_grid(*g)
            return (rb // per, layer, rb % per, cb)

    return pl.BlockSpec((None, None, br, bc), imap)


def mm_w(a, wg, layer, *, col_sharded, transpose_w, out_dtypes, name, extras=(), epilogue=None,
         tm=1024, tn=1024, tk=1024):
    m, k_dim = a.shape
    _, _, rows, cols = wg.shape
    rows_full, cols_full = (rows, cols * N_DEV) if col_sharded else (rows * N_DEV, cols)
    n = rows_full if transpose_w else cols_full
    assert k_dim == (cols_full if transpose_w else rows_full), (a.shape, wg.shape)
    tm = _tile(m, tm, 8)
    if transpose_w:
        tn, tk = _tile(rows, tn), _tile(cols, tk)
        w_spec = _wspec(wg.shape, col_sharded, layer, tn, tk, lambda i, j, k: (j, k))
    else:
        tk, tn = _tile(rows, tk), _tile(cols, tn)
        w_spec = _wspec(wg.shape, col_sharded, layer, tk, tn, lambda i, j, k: (k, j))
    nk = k_dim // tk
    n_extra, n_out = len(extras), len(out_dtypes)
    if epilogue is None:
        epilogue = lambda acc: (acc,)

    def body(a_ref, w_ref, *rest):
        extra_refs, out_refs, acc = rest[:n_extra], rest[n_extra:n_extra + n_out], rest[-1]
        k = pl.program_id(2)

        @pl.when(k == 0)
        def _():
            acc[...] = jnp.zeros_like(acc)

        if transpose_w:
            acc[...] += _dot_nt(a_ref[...], w_ref[...])
        else:
            acc[...] += _dot(a_ref[...], w_ref[...])

        @pl.when(k == nk - 1)
        def _():
            outs = epilogue(acc[...], *[r[...] for r in extra_refs])
            for o_ref, o in zip(out_refs, outs):
                o_ref[...] = o.astype(o_ref.dtype)

    tile_spec = pl.BlockSpec((tm, tn), lambda i, j, k: (i, j))
    outs = pl.pallas_call(
        body, name=name, grid=(m // tm, n // tn, nk),
        in_specs=[pl.BlockSpec((tm, tk), lambda i, j, k: (i, k)), w_spec] + [tile_spec] * n_extra,
        out_specs=[tile_spec] * n_out,
        out_shape=[SDS((m, n), dt) for dt in out_dtypes],
        scratch_shapes=[pltpu.VMEM((tm, tn), F32)],
        compiler_params=_cp("parallel", "parallel", "arbitrary"),
    )(a, wg, *extras)
    return outs


def mm_dw(a, b, buf, layer, *, shape4, col_sharded, name, tm=512, tn=1024, tk=512):
    tokens, m = a.shape
    _, n = b.shape
    _, _, rows, cols = shape4
    tm = _tile(m if col_sharded else rows, tm)
    tn = _tile(cols if col_sharded else n, tn)
    tk = _tile(tokens, tk)
    nk = tokens // tk

    def body(a_ref, b_ref, *rest):
        o_ref, acc = rest[-2], rest[-1]
        k = pl.program_id(2)

        @pl.when(k == 0)
        def _():
            acc[...] = jnp.zeros_like(acc)

        acc[...] += _dot_tn(a_ref[...], b_ref[...])

        @pl.when(k == nk - 1)
        def _():
            o_ref[...] = acc[...]

    ins = [a, b] + ([] if buf is None else [buf])
    return pl.pallas_call(
        body, name=name, grid=(m // tm, n // tn, nk),
        in_specs=[pl.BlockSpec((tk, tm), lambda i, j, k: (k, i)),
                  pl.BlockSpec((tk, tn), lambda i, j, k: (k, j))] + ([] if buf is None else [ANY]),
        out_specs=_wspec(shape4, col_sharded, layer, tm, tn, lambda i, j, k: (i, j)),
        out_shape=SDS(shape4, F32),
        scratch_shapes=[pltpu.VMEM((tm, tn), F32)],
        input_output_aliases={} if buf is None else {2: 0},
        compiler_params=_cp("parallel", "parallel", "arbitrary"),
    )(*ins)


def rms_fwd(h, g, name):
    t, d = h.shape
    tt = _tile(t, 512, 8)

    def body(h_ref, g_ref, o_ref):
        x = h_ref[...]
        r = lax.rsqrt(jnp.mean(x * x, axis=-1, keepdims=True) + EPS)
        o_ref[...] = (x * r * g_ref[...]).astype(BF16)

    row = pl.BlockSpec((tt, d), lambda i: (i, 0))
    return pl.pallas_call(
        body, name=name, grid=(t // tt,),
        in_specs=[row, pl.BlockSpec((1, d), lambda i: (0, 0))], out_specs=row,
        out_shape=SDS((t, d), BF16), compiler_params=_cp("parallel"),
    )(h, g.reshape(1, d))


def _rms_bwd_math(x, g, dhn):
    r = lax.rsqrt(jnp.mean(x * x, axis=-1, keepdims=True) + EPS)
    xn = x * r
    dxn = dhn * g
    dx = r * (dxn - xn * jnp.mean(dxn * xn, axis=-1, keepdims=True))
    return dx, jnp.sum(dhn * xn, axis=0, keepdims=True)


def rms_bwd(h, g, dhn, dres, name):
    t, d = h.shape
    tt = _tile(t, 256, 8)

    def body(h_ref, g_ref, dhn_ref, dres_ref, dh_ref, dhb_ref, dg_ref):
        @pl.when(pl.program_id(0) == 0)
        def _():
            dg_ref[...] = jnp.zeros_like(dg_ref)

        dx, dg = _rms_bwd_math(h_ref[...], g_ref[...], dhn_ref[...].astype(F32))
        dh = dres_ref[...] + dx
        dh_ref[...] = dh
        dhb_ref[...] = dh.astype(BF16)
        dg_ref[...] += dg

    row = pl.BlockSpec((tt, d), lambda i: (i, 0))
    vec = pl.BlockSpec((1, d), lambda i: (0, 0))
    return pl.pallas_call(
        body, name=name, grid=(t // tt,),
        in_specs=[row, vec, row, row], out_specs=[row, row, vec],
        out_shape=[SDS((t, d), F32), SDS((t, d), BF16), SDS((1, d), F32)],
        compiler_params=_cp("arbitrary"),
    )(h, g.reshape(1, d), dhn, dres)


def loss_head(h, g, target, name):
    t, d = h.shape
    tt = _tile(t, 256, 8)

    def body(h_ref, g_ref, t_ref, dh_ref, dg_ref, loss_ref):
        @pl.when(pl.program_id(0) == 0)
        def _():
            dg_ref[...] = jnp.zeros_like(dg_ref)
            loss_ref[...] = jnp.zeros_like(loss_ref)

        x, gain = h_ref[...], g_ref[...]
        r = lax.rsqrt(jnp.mean(x * x, axis=-1, keepdims=True) + EPS)
        err = x * r * gain - t_ref[...]
        loss_ref[...] += 0.5 * jnp.sum(jnp.mean(err * err, axis=-1, keepdims=True))
        dx, dg = _rms_bwd_math(x, gain, err * (1.0 / d))
        dh_ref[...] = dx
        dg_ref[...] += dg

    row = pl.BlockSpec((tt, d), lambda i: (i, 0))
    vec = pl.BlockSpec((1, d), lambda i: (0, 0))
    return pl.pallas_call(
        body, name=name, grid=(t // tt,),
        in_specs=[row, vec, row],
        out_specs=[row, vec, pl.BlockSpec((8, LANES), lambda i: (0, 0))],
        out_shape=[SDS((t, d), F32), SDS((1, d), F32), SDS((8, LANES), F32)],
        compiler_params=_cp("arbitrary"),
    )(h, g.reshape(1, d), target)


def ple_bwd(dh, gz, pp, name):
    t, d = dh.shape
    tt = _tile(t, 256, 8)

    def body(dh_ref, gz_ref, pp_ref, dpp_ref, dgz_ref):
        g = dh_ref[...]
        gate = jax.nn.sigmoid(gz_ref[...].astype(F32))
        dpp_ref[...] = (g * gate).astype(BF16)
        dgz_ref[...] = (g * pp_ref[...].astype(F32) * gate * (1.0 - gate)).astype(BF16)

    row = pl.BlockSpec((tt, d), lambda i: (i, 0))
    return pl.pallas_call(
        body, name=name, grid=(t // tt,), in_specs=[row, row, row], out_specs=[row, row],
        out_shape=[SDS((t, d), BF16), SDS((t, d), BF16)], compiler_params=_cp("parallel"),
    )(dh, gz, pp)


def _layer_norm_parts(y1):
    mu = jnp.mean(y1, axis=-1, keepdims=True)
    dlt = y1 - mu
    rstd = lax.rsqrt(jnp.mean(dlt * dlt, axis=-1, keepdims=True) + EPS)
    return dlt * rstd, rstd


def conf_fwd(u, cw, cb, lg, lb, name):
    t = u.shape[0]
    ka, c = cw.shape
    tt = _tile(t, 256, HALO_A)
    hb = tt // HALO_A
    off = HALO_A - (ka - 1)

    def body(av_ref, ag_ref, pv_ref, pg_ref, cw_ref, cb_ref, lg_ref, lb_ref, ya_ref, ya1_ref, buf):
        live = (pl.program_id(0) > 0).astype(F32)
        buf[0:HALO_A, :] = pv_ref[...] * jax.nn.sigmoid(pg_ref[...]) * live
        buf[HALO_A:, :] = av_ref[...] * jax.nn.sigmoid(ag_ref[...])
        for c0 in range(0, c, LANES):
            cs = pl.ds(c0, LANES)
            acc = jnp.broadcast_to(cb_ref[:, cs], (tt, LANES))
            for k in range(ka):
                acc = acc + cw_ref[k:k + 1, cs] * buf[pl.ds(off + k, tt), cs]
            ya1_ref[:, cs] = acc
        yn, _ = _layer_norm_parts(ya1_ref[...])
        y2 = yn * lg_ref[...] + lb_ref[...]
        ya_ref[...] = (y2 * jax.nn.sigmoid(y2)).astype(BF16)

    cur = lambda col: pl.BlockSpec((tt, c), lambda i: (i, col))
    prev = lambda col: pl.BlockSpec((HALO_A, c), lambda i: (jnp.maximum(i * hb - 1, 0), col))
    vec = pl.BlockSpec((1, c), lambda i: (0, 0))
    return pl.pallas_call(
        body, name=name, grid=(t // tt,),
        in_specs=[cur(0), cur(1), prev(0), prev(1), pl.BlockSpec((ka, c), lambda i: (0, 0)), vec, vec, vec],
        out_specs=[pl.BlockSpec((tt, c), lambda i: (i, 0))] * 2,
        out_shape=[SDS((t, c), BF16), SDS((t, c), F32)],
        scratch_shapes=[pltpu.VMEM((tt + HALO_A, c), F32)],
        compiler_params=_cp("parallel"),
    )(u, u, u, u, cw, cb, lg, lb)


def conf_bwd(u, ya1, dcat, cw, lg, lb, name):
    t = u.shape[0]
    ka, c = cw.shape
    tt = _tile(t, 256, HALO_A)
    nt, hb = t // tt, tt // HALO_A
    off = HALO_A - (ka - 1)

    def body(av_ref, ag_ref, pv_ref, pg_ref, y1_ref, dya_ref, cw_ref, lg_ref, lb_ref,
             du_ref, pgrad_ref, ybuf, dbuf, carry):
        i = pl.program_id(0)

        @pl.when(i == 0)
        def _():
            carry[...] = jnp.zeros_like(carry)
            pgrad_ref[...] = jnp.zeros_like(pgrad_ref)

        live = (i < nt - 1).astype(F32)
        av = av_ref[...]
        sg = jax.nn.sigmoid(ag_ref[...])
        ybuf[0:HALO_A, :] = pv_ref[...] * jax.nn.sigmoid(pg_ref[...]) * live
        ybuf[HALO_A:, :] = av * sg
        yn, rstd = _layer_norm_parts(y1_ref[...])
        gain = lg_ref[...]
        y2 = yn * gain + lb_ref[...]
        s2 = jax.nn.sigmoid(y2)
        dy2 = dya_ref[...].astype(F32) * (s2 * (1.0 + y2 * (1.0 - s2)))
        pgrad_ref[ka + 1:ka + 2, :] += jnp.sum(dy2 * yn, axis=0, keepdims=True)
        pgrad_ref[ka + 2:ka + 3, :] += jnp.sum(dy2, axis=0, keepdims=True)
        dyn = dy2 * gain
        dy1 = rstd * (dyn - jnp.mean(dyn, axis=-1, keepdims=True)
                      - yn * jnp.mean(dyn * yn, axis=-1, keepdims=True))
        pgrad_ref[ka:ka + 1, :] += jnp.sum(dy1, axis=0, keepdims=True)
        dbuf[0:tt, :] = dy1
        dbuf[tt:, :] = carry[...]
        carry[...] = dbuf[0:HALO_A, :]
        for c0 in range(0, c, LANES):
            cs = pl.ds(c0, LANES)
            d_cur = dbuf[0:tt, cs]
            acc = jnp.zeros((tt, LANES), F32)
            for k in range(ka):
                acc = acc + cw_ref[k:k + 1, cs] * dbuf[pl.ds(ka - 1 - k, tt), cs]
                pgrad_ref[k:k + 1, cs] += jnp.sum(d_cur * ybuf[pl.ds(off + k, tt), cs], axis=0, keepdims=True)
            sgc, avc = sg[:, c0:c0 + LANES], av[:, c0:c0 + LANES]
            du_ref[:, cs] = (acc * sgc).astype(BF16)
            du_ref[:, pl.ds(c + c0, LANES)] = (acc * avc * sgc * (1.0 - sgc)).astype(BF16)

    cur = lambda col: pl.BlockSpec((tt, c), lambda i: (nt - 1 - i, col))
    prev = lambda col: pl.BlockSpec((HALO_A, c), lambda i: (jnp.maximum((nt - 1 - i) * hb - 1, 0), col))
    vec = pl.BlockSpec((1, c), lambda i: (0, 0))
    return pl.pallas_call(
        body, name=name, grid=(nt,),
        in_specs=[cur(0), cur(1), prev(0), prev(1), cur(0), cur(0),
                  pl.BlockSpec((ka, c), lambda i: (0, 0)), vec, vec],
        out_specs=[pl.BlockSpec((tt, 2 * c), lambda i: (nt - 1 - i, 0)),
                   pl.BlockSpec((ka + 3, c), lambda i: (0, 0))],
        out_shape=[SDS((t, 2 * c), BF16), SDS((ka + 3, c), F32)],
        scratch_shapes=[pltpu.VMEM((tt + HALO_A, c), F32), pltpu.VMEM((tt + HALO_A, c), F32),
                        pltpu.VMEM((HALO_A, c), F32)],
        compiler_params=_cp("arbitrary"),
    )(u, u, u, u, ya1, dcat, cw, lg, lb)


def _rg_gates(xc, wa_ref, ba, wx_ref, bx, sp, nh, hd):
    parts = []
    for h in range(nh):
        hs = slice(h * hd, (h + 1) * hd)
        xh = xc[:, hs].astype(BF16)
        r = jax.nn.sigmoid(_dot(xh, wa_ref[h]) + ba[:, hs])
        ig = jax.nn.sigmoid(_dot(xh, wx_ref[h]) + bx[:, hs])
        log_a = -RG_C * r * sp[:, hs]
        parts.append((r, ig, jnp.exp(log_a), jnp.sqrt(-_expm1(2.0 * log_a))))
    return parts


def _conv_b(xbuf, bw_ref, bb, tt, kb):
    off = HALO_B - (kb - 1)
    xc = bb
    for k in range(kb):
        xc = xc + bw_ref[k:k + 1, :] * xbuf[pl.ds(off + k, tt), :]
    return xc


def rglru_fwd(u, bw, bb, wa, ba, wx, bx, lam, name):
    t = u.shape[0]
    kb, c = bw.shape
    nh, hd, _ = wa.shape
    tt = _tile(t, 256, HALO_B)
    hb = tt // HALO_B

    def body(xr_ref, gr_ref, px_ref, bw_ref, bb_ref, wa_ref, ba_ref, wx_ref, bx_ref, lam_ref,
             yb_ref, hs_ref, xbuf, a_s, u_s, hc):
        i = pl.program_id(0)

        @pl.when(i == 0)
        def _():
            hc[...] = jnp.zeros_like(hc)

        xbuf[0:HALO_B, :] = px_ref[...] * (i > 0).astype(F32)
        xbuf[HALO_B:, :] = xr_ref[...]
        xc = _conv_b(xbuf, bw_ref, bb_ref[...], tt, kb)
        sp = _softplus(-lam_ref[...])
        gates = _rg_gates(xc, wa_ref, ba_ref[...], wx_ref, bx_ref[...], sp, nh, hd)
        for h, (_, ig, a, mult) in enumerate(gates):
            hs = slice(h * hd, (h + 1) * hd)
            a_s[:, hs] = a
            u_s[:, hs] = mult * ig * xc[:, hs]

        def step(g, hcur):
            base = pl.multiple_of(g * 8, 8)
            for j in range(8):
                hcur = a_s[pl.ds(base + j, 1), :] * hcur + u_s[pl.ds(base + j, 1), :]
                hs_ref[pl.ds(base + j, 1), :] = hcur
            return hcur

        hc[...] = lax.fori_loop(0, tt // 8, step, hc[...])
        yb_ref[...] = (hs_ref[...] * _gelu(gr_ref[...])).astype(BF16)

    cur = lambda col: pl.BlockSpec((tt, c), lambda i: (i, col))
    vec = pl.BlockSpec((1, c), lambda i: (0, 0))
    wsp = pl.BlockSpec((nh, hd, hd), lambda i: (0, 0, 0))
    return pl.pallas_call(
        body, name=name, grid=(t // tt,),
        in_specs=[cur(2), cur(3), pl.BlockSpec((HALO_B, c), lambda i: (jnp.maximum(i * hb - 1, 0), 2)),
                  pl.BlockSpec((kb, c), lambda i: (0, 0)), vec, wsp, vec, wsp, vec, vec],
        out_specs=[pl.BlockSpec((tt, c), lambda i: (i, 0))] * 2,
        out_shape=[SDS((t, c), BF16), SDS((t, c), F32)],
        scratch_shapes=[pltpu.VMEM((tt + HALO_B, c), F32), pltpu.VMEM((tt, c), F32),
                        pltpu.VMEM((tt, c), F32), pltpu.VMEM((1, c), F32)],
        compiler_params=_cp("arbitrary"),
    )(u, u, u, bw, bb, wa, ba, wx, bx, lam)


def rglru_bwd(u, hs_all, dcat, bw, bb, wa, ba, wx, bx, lam, name):
    t = u.shape[0]
    kb, c = bw.shape
    nh, hd, _ = wa.shape
    tt = _tile(t, 256, HALO_B)
    nt, hb = t // tt, tt // HALO_B
    off = HALO_B - (kb - 1)

    def body(xr_ref, gr_ref, px_ref, hs_ref, ph_ref, dyb_ref, bw_ref, bb_ref, wa_ref, ba_ref, wx_ref, bx_ref,
             lam_ref, du_ref, pgrad_ref, dwa_ref, dwx_ref, xbuf, hbuf, a_s, g_s, dxbuf, cg, cdx):
        i = pl.program_id(0)

        @pl.when(i == 0)
        def _():
            cg[...] = jnp.zeros_like(cg)
            cdx[...] = jnp.zeros_like(cdx)
            pgrad_ref[...] = jnp.zeros_like(pgrad_ref)
            dwa_ref[...] = jnp.zeros_like(dwa_ref)
            dwx_ref[...] = jnp.zeros_like(dwx_ref)

        live = (i < nt - 1).astype(F32)
        xbuf[0:HALO_B, :] = px_ref[...] * live
        xbuf[HALO_B:, :] = xr_ref[...]
        hbuf[0:HALO_B, :] = ph_ref[...] * live
        hbuf[HALO_B:, :] = hs_ref[...]
        xc = _conv_b(xbuf, bw_ref, bb_ref[...], tt, kb)
        lam_v = lam_ref[...]
        sp = _softplus(-lam_v)
        gates = _rg_gates(xc, wa_ref, ba_ref[...], wx_ref, bx_ref[...], sp, nh, hd)
        gr = gr_ref[...]
        dyb = dyb_ref[...].astype(F32)
        g_s[...] = dyb * _gelu(gr)
        for h, (_, _, a, _) in enumerate(gates):
            a_s[:, h * hd:(h + 1) * hd] = a

        def step(g, carry):
            base = pl.multiple_of((tt // 8 - 1 - g) * 8, 8)
            for j in range(7, -1, -1):
                gt = g_s[pl.ds(base + j, 1), :] + carry
                g_s[pl.ds(base + j, 1), :] = gt
                carry = a_s[pl.ds(base + j, 1), :] * gt
            return carry

        cg[...] = lax.fori_loop(0, tt // 8, step, cg[...])
        hprev = hbuf[pl.ds(HALO_B - 1, tt), :]
        gfull = g_s[...]
        for h, (r, ig, a, mult) in enumerate(gates):
            hs = slice(h * hd, (h + 1) * hd)
            g, xch = gfull[:, hs], xc[:, hs]
            d_la = g * hprev[:, hs] * a - g * ig * xch * (a * a) / mult
            d_ig = g * mult * xch
            d_ra = d_la * (-RG_C * sp[:, hs]) * r * (1.0 - r)
            d_ia = d_ig * ig * (1.0 - ig)
            dsp = jnp.sum(d_la * (-RG_C) * r, axis=0, keepdims=True)
            pgrad_ref[kb + 1:kb + 2, hs] += jnp.sum(d_ra, axis=0, keepdims=True)
            pgrad_ref[kb + 2:kb + 3, hs] += jnp.sum(d_ia, axis=0, keepdims=True)
            pgrad_ref[kb + 3:kb + 4, hs] += dsp * (-jax.nn.sigmoid(-lam_v[:, hs]))
            xh, d_ra_b, d_ia_b = xch.astype(BF16), d_ra.astype(BF16), d_ia.astype(BF16)
            dwa_ref[h] += _dot_tn(xh, d_ra_b)
            dwx_ref[h] += _dot_tn(xh, d_ia_b)
            dxbuf[0:tt, hs] = g * mult * ig + _dot_nt(d_ra_b, wa_ref[h]) + _dot_nt(d_ia_b, wx_ref[h])
        dxbuf[tt:, :] = cdx[...]
        cdx[...] = dxbuf[0:HALO_B, :]
        d_xc = dxbuf[0:tt, :]
        pgrad_ref[kb:kb + 1, :] += jnp.sum(d_xc, axis=0, keepdims=True)
        d_xr = jnp.zeros((tt, c), F32)
        for k in range(kb):
            d_xr = d_xr + bw_ref[k:k + 1, :] * dxbuf[pl.ds(kb - 1 - k, tt), :]
            pgrad_ref[k:k + 1, :] += jnp.sum(d_xc * xbuf[pl.ds(off + k, tt), :], axis=0, keepdims=True)
        du_ref[:, 0:c] = d_xr.astype(BF16)
        du_ref[:, c:2 * c] = (dyb * hs_ref[...] * _gelu_grad(gr)).astype(BF16)

    cur = lambda col: pl.BlockSpec((tt, c), lambda i: (nt - 1 - i, col))
    prev = lambda col: pl.BlockSpec((HALO_B, c), lambda i: (jnp.maximum((nt - 1 - i) * hb - 1, 0), col))
    vec = pl.BlockSpec((1, c), lambda i: (0, 0))
    wsp = pl.BlockSpec((nh, hd, hd), lambda i: (0, 0, 0))
    return pl.pallas_call(
        body, name=name, grid=(nt,),
        in_specs=[cur(2), cur(3), prev(2), cur(0), prev(0), cur(1),
                  pl.BlockSpec((kb, c), lambda i: (0, 0)), vec, wsp, vec, wsp, vec, vec],
        out_specs=[pl.BlockSpec((tt, 2 * c), lambda i: (nt - 1 - i, 0)),
                   pl.BlockSpec((kb + 4, c), lambda i: (0, 0)), wsp, wsp],
        out_shape=[SDS((t, 2 * c), BF16), SDS((kb + 4, c), F32), SDS((nh, hd, hd), F32), SDS((nh, hd, hd), F32)],
        scratch_shapes=[pltpu.VMEM((tt + HALO_B, c), F32), pltpu.VMEM((tt + HALO_B, c), F32),
                        pltpu.VMEM((tt, c), F32), pltpu.VMEM((tt, c), F32), pltpu.VMEM((tt + HALO_B, c), F32),
                        pltpu.VMEM((1, c), F32), pltpu.VMEM((HALO_B, c), F32)],
        compiler_params=_cp("arbitrary"),
    )(u, u, u, hs_all, hs_all, dcat, bw, bb, wa, ba, wx, bx, lam)


def attn_fwd(qkv, nh, name):
    t = qkv.shape[0]
    dh = qkv.shape[1] // (3 * nh)
    bq = _tile(t, 256)
    scale = 1.0 / math.sqrt(dh)

    def body(q_ref, k_ref, v_ref, o_ref, s_ref):
        qi = pl.program_id(1)
        q = q_ref[...]
        row = lax.broadcasted_iota(jnp.int32, (bq, bq), 0)
        col = lax.broadcasted_iota(jnp.int32, (bq, bq), 1)
        tri = (row >= col).astype(BF16)

        def block(kb, c, acc, mask):
            ks = pl.ds(pl.multiple_of(kb * bq, bq), bq)
            z = _dot_nt(q, k_ref[ks, :]) * scale
            sp = _softplus(z)
            if mask is not None:
                sp = jnp.where(mask, sp, 0.0)
            cum = _dot_split(sp, tri)
            lw = z - c - cum
            if mask is not None:
                lw = jnp.where(mask, lw, -1e30)
            acc = acc + _dot(jnp.exp(lw).astype(BF16), v_ref[ks, :])
            return c + cum[:, 0:1], acc

        c, acc = block(qi, jnp.zeros((bq, 1), F32), jnp.zeros((bq, dh), F32), col < row)
        c, acc = lax.fori_loop(0, qi, lambda it, cr: block(qi - 1 - it, cr[0], cr[1], None), (c, acc))
        o_ref[...] = acc.astype(BF16)
        s_ref[...] = jnp.broadcast_to(c, (bq, LANES))

    whole = lambda base: pl.BlockSpec((t, dh), lambda h, qi: (0, base + h))
    return pl.pallas_call(
        body, name=name, grid=(nh, t // bq),
        in_specs=[pl.BlockSpec((bq, dh), lambda h, qi: (qi, h)), whole(nh), whole(2 * nh)],
        out_specs=[pl.BlockSpec((bq, dh), lambda h, qi: (qi, h)),
                   pl.BlockSpec((None, bq, LANES), lambda h, qi: (h, qi, 0))],
        out_shape=[SDS((t, nh * dh), BF16), SDS((nh, t, LANES), F32)],
        compiler_params=_cp("parallel", "parallel"),
    )(qkv, qkv, qkv)


def attn_bwd(qkv, do, s_tot, nh, name):
    t = qkv.shape[0]
    dh = qkv.shape[1] // (3 * nh)
    bq = _tile(t, 256)
    scale = 1.0 / math.sqrt(dh)

    def body(q_ref, k_ref, v_ref, do_ref, s_ref, dq_ref, dk_ref, dv_ref):
        qi = pl.program_id(1)

        @pl.when(qi == 0)
        def _():
            dk_ref[...] = jnp.zeros_like(dk_ref)
            dv_ref[...] = jnp.zeros_like(dv_ref)

        q, do_b = q_ref[...], do_ref[...]
        s_row = s_ref[:, 0:1]
        row = lax.broadcasted_iota(jnp.int32, (bq, bq), 0)
        col = lax.broadcasted_iota(jnp.int32, (bq, bq), 1)
        tri_excl = (row < col).astype(BF16)
        tri_incl = (row <= col).astype(BF16)

        def block(kb, psp, pg, dq, mask):
            ks = pl.ds(pl.multiple_of(kb * bq, bq), bq)
            k_b, v_b = k_ref[ks, :], v_ref[ks, :]
            z = _dot_nt(q, k_b) * scale
            sp_all = _softplus(z)
            sp = sp_all if mask is None else jnp.where(mask, sp_all, 0.0)
            lw = z - (s_row - psp - _dot_split(sp, tri_excl))
            if mask is not None:
                lw = jnp.where(mask, lw, -1e30)
            a = jnp.exp(lw)
            g = _dot_nt(do_b, v_b) * a
            dz = g - (pg + _dot_split(g, tri_incl)) * jnp.exp(z - sp_all)
            if mask is not None:
                dz = jnp.where(mask, dz, 0.0)
            dzs = (dz * scale).astype(BF16)
            dk_ref[ks, :] += _dot_tn(dzs, q)
            dv_ref[ks, :] += _dot_tn(a.astype(BF16), do_b)
            return (psp + jnp.sum(sp, axis=1, keepdims=True), pg + jnp.sum(g, axis=1, keepdims=True),
                    dq + _dot(dzs, k_b))

        init = (jnp.zeros((bq, 1), F32), jnp.zeros((bq, 1), F32), jnp.zeros((bq, dh), F32))
        psp, pg, dq = lax.fori_loop(0, qi, lambda kb, cr: block(kb, cr[0], cr[1], cr[2], None), init)
        _, _, dq = block(qi, psp, pg, dq, col < row)
        dq_ref[...] = dq.astype(BF16)

    whole = lambda base: pl.BlockSpec((t, dh), lambda h, qi: (0, base + h))
    qblk = pl.BlockSpec((bq, dh), lambda h, qi: (qi, h))
    acc = pl.BlockSpec((t, dh), lambda h, qi: (0, h))
    return pl.pallas_call(
        body, name=name, grid=(nh, t // bq),
        in_specs=[qblk, whole(nh), whole(2 * nh), qblk,
                  pl.BlockSpec((None, bq, LANES), lambda h, qi: (h, qi, 0))],
        out_specs=[qblk, acc, acc],
        out_shape=[SDS((t, nh * dh), BF16), SDS((t, nh * dh), F32), SDS((t, nh * dh), F32)],
        compiler_params=_cp("parallel", "arbitrary"),
    )(qkv, qkv, qkv, do, s_tot)


def all_gather(x, name):
    def body(x_ref, out_ref, send_sems, recv_sems, local_sem):
        mx, my, mc = lax.axis_index("x"), lax.axis_index("y"), lax.axis_index("c")
        me, sibling = (mx, my, mc), (mx, my, 1 - mc)
        chips = [(1 - mx, my), (mx, 1 - my), (1 - mx, 1 - my)]

        def slot(px, py, pc):
            return out_ref.at[4 * px + 2 * py + pc]

        def copy(k, block, to, src=None):
            return pltpu.make_async_remote_copy(
                src_ref=slot(*block) if src is None else src, dst_ref=slot(*block),
                send_sem=send_sems.at[k], recv_sem=recv_sems.at[k], device_id=to, device_id_type=MESH)

        mine = pltpu.make_async_copy(x_ref, slot(*me), local_sem)
        mine.start()
        first = [copy(0, me, sibling, src=x_ref)]
        first += [copy(1 + j, me, (*chip, mc), src=x_ref) for j, chip in enumerate(chips)]
        for cp in first:
            cp.start()
        passed = [copy(4 + j, (*chip, mc), sibling) for j, chip in enumerate(chips)]
        for j, chip in enumerate(chips):
            copy(1 + j, (*chip, mc), me).wait_recv()
            passed[j].start()
        copy(0, sibling, me).wait_recv()
        for j, chip in enumerate(chips):
            copy(4 + j, (*chip, 1 - mc), me).wait_recv()
        for cp in first + passed:
            cp.wait_send()
        mine.wait()

    return pl.pallas_call(
        body, name=name, out_shape=SDS((N_DEV,) + x.shape, x.dtype), in_specs=[ANY], out_specs=ANY,
        scratch_shapes=[pltpu.SemaphoreType.DMA((7,)), pltpu.SemaphoreType.DMA((7,)), pltpu.SemaphoreType.DMA],
    )(x)


def exchange_slices(g, name):
    def body(g_ref, out_ref, send_sems, recv_sems, local_sem):
        mx, my, mc = lax.axis_index("x"), lax.axis_index("y"), lax.axis_index("c")
        me = 4 * mx + 2 * my + mc
        mine = pltpu.make_async_copy(g_ref.at[me], out_ref.at[me], local_sem)
        mine.start()
        copies = []
        for k in range(1, N_DEV):
            px = 1 - mx if k & 4 else mx
            py = 1 - my if k & 2 else my
            pc = 1 - mc if k & 1 else mc
            cp = pltpu.make_async_remote_copy(
                src_ref=g_ref.at[4 * px + 2 * py + pc], dst_ref=out_ref.at[me],
                send_sem=send_sems.at[k - 1], recv_sem=recv_sems.at[k - 1],
                device_id=(px, py, pc), device_id_type=MESH)
            cp.start()
            copies.append(cp)
        for cp in copies:
            cp.wait()
        mine.wait()

    return pl.pallas_call(
        body, name=name, out_shape=SDS(g.shape, g.dtype), in_specs=[ANY], out_specs=ANY,
        scratch_shapes=[pltpu.SemaphoreType.DMA((7,)), pltpu.SemaphoreType.DMA((7,)), pltpu.SemaphoreType.DMA],
    )(g)


def reduce_adamw(gs, w, m, v, name):
    s, r, c = gs.shape
    tr = _tile(r, max(8, (128 * 1024) // c), 8)

    def body(gs_ref, w_ref, m_ref, v_ref, g_out, d_out, m_out, v_out):
        g = gs_ref[0]
        for j in range(1, s):
            g = g + gs_ref[j]
        m_new = ADAM_B1 * m_ref[...] + (1.0 - ADAM_B1) * g
        v_new = ADAM_B2 * v_ref[...] + (1.0 - ADAM_B2) * (g * g)
        m_hat = m_new / (1.0 - ADAM_B1 ** ADAM_STEP)
        v_hat = v_new / (1.0 - ADAM_B2 ** ADAM_STEP)
        g_out[...] = g
        d_out[...] = -ADAM_LR * (m_hat / (jnp.sqrt(v_hat) + ADAM_EPS) + ADAM_WD * w_ref[...])
        m_out[...] = m_new
        v_out[...] = v_new

    row = pl.BlockSpec((tr, c), lambda i: (i, 0))
    return pl.pallas_call(
        body, name=name, grid=(r // tr,),
        in_specs=[pl.BlockSpec((s, tr, c), lambda i: (0, i, 0)), row, row, row], out_specs=[row] * 4,
        out_shape=[SDS((r, c), F32)] * 4, compiler_params=_cp("parallel"),
    )(gs, w, m, v)


BIG = {
    "w_in_rec": True, "w_out_rec": False, "w_qkv": True, "w_o_attn": False,
    "w_mlp_up": True, "w_mlp_down": False, "w_ple_proj": True, "w_ple_gate": False,
}
REPLICATED = ["norm_mix_g", "norm_mlp_g", "norm_ple_g", "norm_f_g", "conv_a_b", "ln_a_g", "ln_a_b", "conv_b_b",
              "w_rg_a", "b_rg_a", "w_rg_x", "b_rg_x", "rg_lambda"]
CONV_W = ["conv_a_w", "conv_b_w"]
WEIGHTS = ["norm_mix_g", "norm_mlp_g", "norm_ple_g", "norm_f_g", "w_in_rec", "conv_a_w", "conv_a_b", "ln_a_g",
           "ln_a_b", "conv_b_w", "conv_b_b", "w_rg_a", "b_rg_a", "w_rg_x", "b_rg_x", "rg_lambda", "w_out_rec",
           "w_qkv", "w_o_attn", "w_mlp_up", "w_mlp_down", "w_ple_proj", "w_ple_gate"]


def _pack(arrays):
    flat = jnp.concatenate([a.reshape(-1) for a in arrays])
    pad = (-flat.shape[0]) % (8 * LANES)
    return jnp.pad(flat, (0, pad)).reshape(-1, LANES)


def _unpack(packed, shapes, lead=()):
    flat = packed.reshape(lead + (-1,))
    out, pos = [], 0
    for shp in shapes:
        size = math.prod(shp)
        out.append(flat[..., pos:pos + size].reshape(lead + tuple(shp)))
        pos += size
    return out


def _step(x, p, loss_target, w, mom, var):
    dev = 4 * lax.axis_index("x") + 2 * lax.axis_index("y") + lax.axis_index("c")
    depth = w["norm_mix_g"].shape[0]
    h = x[0]
    nh = SB_HEADS

    wg = {n: all_gather(w[n].astype(BF16), name=f"ag_{n}") for n in BIG}
    conv_shapes = [w[n].shape for n in CONV_W]
    conv_all = all_gather(_pack([w[n] for n in CONV_W]), name="ag_conv_w")
    conv_full = [jnp.moveaxis(a, 0, -2).reshape(a.shape[1:-1] + (-1,))
                 for a in _unpack(conv_all, conv_shapes, lead=(N_DEV,))]
    conv_a_w, conv_b_w = conv_full
    vec = lambda a: a.reshape(1, -1)

    saved = []
    for i in range(depth):
        j = i // 2
        s = {"h0": h}
        s["hn1"] = rms_fwd(h, w["norm_mix_g"][i], name="rms_fwd")
        if i % 2 == 0:
            (s["u"],) = mm_w(s["hn1"], wg["w_in_rec"], j, col_sharded=True, transpose_w=False,
                             out_dtypes=[F32], name="mm_in_rec")
            ya, s["ya1"] = conf_fwd(s["u"], conv_a_w[j], vec(w["conv_a_b"][j]), vec(w["ln_a_g"][j]),
                                    vec(w["ln_a_b"][j]), name="conf_fwd")
            yb, s["hs"] = rglru_fwd(s["u"], conv_b_w[j], vec(w["conv_b_b"][j]), w["w_rg_a"][j].astype(BF16),
                                    vec(w["b_rg_a"][j]), w["w_rg_x"][j].astype(BF16), vec(w["b_rg_x"][j]),
                                    vec(w["rg_lambda"][j]), name="rglru_fwd")
            s["mix_in"] = jnp.concatenate([ya, yb], axis=1)
            w_out, out_name = wg["w_out_rec"], "w_out_rec"
        else:
            (s["qkv"],) = mm_w(s["hn1"], wg["w_qkv"], j, col_sharded=True, transpose_w=False,
                               out_dtypes=[BF16], name="mm_qkv")
            s["mix_in"], s["s_tot"] = attn_fwd(s["qkv"], nh, name="attn_fwd")
            w_out, out_name = wg["w_o_attn"], "w_o_attn"
        s["w_out"], s["out_name"] = w_out, out_name
        (h,) = mm_w(s["mix_in"], w_out, j, col_sharded=False, transpose_w=False, out_dtypes=[F32],
                    extras=[h], epilogue=lambda acc, res: (res + acc,), name="mm_mix_out")
        s["h1"] = h
        s["hn2"] = rms_fwd(h, w["norm_mlp_g"][i], name="rms_fwd")
        relu = lambda acc: jnp.maximum(acc, 0.0)
        s["up"], s["act"] = mm_w(s["hn2"], wg["w_mlp_up"], i, col_sharded=True, transpose_w=False,
                                 out_dtypes=[BF16, BF16], name="mm_mlp_up",
                                 epilogue=lambda acc: (acc, relu(acc) * relu(acc)))
        (h,) = mm_w(s["act"], wg["w_mlp_down"], i, col_sharded=False, transpose_w=False, out_dtypes=[F32],
                    extras=[h], epilogue=lambda acc, res: (res + acc,), name="mm_mlp_down")
        s["h2"] = h
        s["hn3"] = rms_fwd(h, w["norm_ple_g"][i], name="rms_fwd")
        s["p"] = p[i, 0].astype(BF16)
        (s["pp"],) = mm_w(s["p"], wg["w_ple_proj"], i, col_sharded=True, transpose_w=False, out_dtypes=[F32],
                          name="mm_ple_proj")
        h, s["gz"] = mm_w(s["hn3"], wg["w_ple_gate"], i, col_sharded=False, transpose_w=False,
                          out_dtypes=[F32, BF16], extras=[h, s["pp"]], name="mm_ple_gate",
                          epilogue=lambda acc, res, pp: (res + pp * jax.nn.sigmoid(acc), acc))
        saved.append(s)

    dh, dg_f, loss_part = loss_head(h, w["norm_f_g"], loss_target[0], name="loss_head")
    loss = lax.psum(loss_part[0, 0], ("x", "y", "c"))

    acc = {n: None for n in BIG}
    shape4 = {n: wg[n].shape for n in BIG}
    part = {n: [None] * w[n].shape[0] for n in REPLICATED + CONV_W if n != "norm_f_g"}
    part["norm_f_g"] = dg_f[0]

    def dw(name, a, b, layer):
        acc[name] = mm_dw(a, b, acc[name], layer, shape4=shape4[name], col_sharded=BIG[name], name=f"dw_{name}")

    for i in reversed(range(depth)):
        j = i // 2
        s = saved[i]
        d_pp, d_gz = ple_bwd(dh, s["gz"], s["pp"], name="ple_bwd")
        dw("w_ple_proj", s["p"], d_pp, i)
        dw("w_ple_gate", s["hn3"], d_gz, i)
        (d_hn3,) = mm_w(d_gz, wg["w_ple_gate"], i, col_sharded=False, transpose_w=True, out_dtypes=[BF16],
                        name="mmT_ple_gate")
        dh, dh_b, dg = rms_bwd(s["h2"], w["norm_ple_g"][i], d_hn3, dh, name="rms_bwd")
        part["norm_ple_g"][i] = dg[0]
        (d_up,) = mm_w(dh_b, wg["w_mlp_down"], i, col_sharded=False, transpose_w=True, out_dtypes=[BF16],
                       extras=[s["up"]], name="mmT_mlp_down",
                       epilogue=lambda acc_, up: (acc_ * (2.0 * jnp.maximum(up.astype(F32), 0.0)),))
        dw("w_mlp_down", s["act"], dh_b, i)
        dw("w_mlp_up", s["hn2"], d_up, i)
        (d_hn2,) = mm_w(d_up, wg["w_mlp_up"], i, col_sharded=True, transpose_w=True, out_dtypes=[BF16],
                        name="mmT_mlp_up")
        dh, dh_b, dg = rms_bwd(s["h1"], w["norm_mlp_g"][i], d_hn2, dh, name="rms_bwd")
        part["norm_mlp_g"][i] = dg[0]
        dw(s["out_name"], s["mix_in"], dh_b, j)
        if i % 2 == 0:
            (d_cat,) = mm_w(dh_b, s["w_out"], j, col_sharded=False, transpose_w=True, out_dtypes=[BF16],
                            name="mmT_mix_out")
            du_a, pg_a = conf_bwd(s["u"], s["ya1"], d_cat, conv_a_w[j], vec(w["ln_a_g"][j]), vec(w["ln_a_b"][j]),
                                  name="conf_bwd")
            du_b, pg_b, d_wa, d_wx = rglru_bwd(
                s["u"], s["hs"], d_cat, conv_b_w[j], vec(w["conv_b_b"][j]), w["w_rg_a"][j].astype(BF16),
                vec(w["b_rg_a"][j]), w["w_rg_x"][j].astype(BF16), vec(w["b_rg_x"][j]), vec(w["rg_lambda"][j]),
                name="rglru_bwd")
            ka, kb = conv_a_w.shape[1], conv_b_w.shape[1]
            part["conv_a_w"][j], part["conv_a_b"][j] = pg_a[:ka], pg_a[ka]
            part["ln_a_g"][j], part["ln_a_b"][j] = pg_a[ka + 1], pg_a[ka + 2]
            part["conv_b_w"][j], part["conv_b_b"][j] = pg_b[:kb], pg_b[kb]
            part["b_rg_a"][j], part["b_rg_x"][j], part["rg_lambda"][j] = pg_b[kb + 1], pg_b[kb + 2], pg_b[kb + 3]
            part["w_rg_a"][j], part["w_rg_x"][j] = d_wa, d_wx
            d_mix = jnp.concatenate([du_a, du_b], axis=1)
            w_in, in_name = wg["w_in_rec"], "w_in_rec"
        else:
            (d_o,) = mm_w(dh_b, s["w_out"], j, col_sharded=False, transpose_w=True, out_dtypes=[BF16],
                          name="mmT_mix_out")
            dq, dk, dv = attn_bwd(s["qkv"], d_o, s["s_tot"], nh, name="attn_bwd")
            d_mix = jnp.concatenate([dq, dk.astype(BF16), dv.astype(BF16)], axis=1)
            w_in, in_name = wg["w_qkv"], "w_qkv"
        dw(in_name, s["hn1"], d_mix, j)
        (d_hn1,) = mm_w(d_mix, w_in, j, col_sharded=True, transpose_w=True, out_dtypes=[BF16], name="mmT_mix_in")
        dh, _, dg = rms_bwd(s["h0"], w["norm_mix_g"][i], d_hn1, dh, name="rms_bwd")
        part["norm_mix_g"][i] = dg[0]

    grads, deltas, new_m, new_v = {}, {}, {}, {}

    def finish(name, outs, shape):
        for d, o in zip((grads, deltas, new_m, new_v), outs):
            d[name] = o.reshape(shape)

    for n in BIG:
        recv = exchange_slices(acc[n], name=f"xch_{n}")
        _, nl, r, c = recv.shape
        flat = lambda a: a.reshape(nl * r, c)
        finish(n, reduce_adamw(recv.reshape(N_DEV, nl * r, c), flat(w[n]), flat(mom[n]), flat(var[n]),
                               name=f"adamw_{n}"), w[n].shape)

    full = {n: (part[n] if n == "norm_f_g" else jnp.stack(part[n])) for n in REPLICATED + CONV_W}
    rep = _pack([full[n] for n in REPLICATED])
    conv = _pack([full[n] for n in CONV_W])
    small = all_gather(jnp.concatenate([rep, conv]), name="ag_small_grads")
    rep_shapes = [w[n].shape for n in REPLICATED]
    outs = reduce_adamw(small[:, :rep.shape[0]], _pack([w[n] for n in REPLICATED]),
                        _pack([mom[n] for n in REPLICATED]), _pack([var[n] for n in REPLICATED]), name="adamw_small")
    for d, o in zip((grads, deltas, new_m, new_v), outs):
        for n, a in zip(REPLICATED, _unpack(o, rep_shapes)):
            d[n] = a
    conv_parts = _unpack(small[:, rep.shape[0]:], [full[n].shape for n in CONV_W], lead=(N_DEV,))
    width = w["conv_a_w"].shape[-1]
    mine = [lax.dynamic_slice_in_dim(a, dev * width, width, axis=a.ndim - 1) for a in conv_parts]
    packed = jnp.stack([_pack([a[d] for a in mine]) for d in range(N_DEV)])
    outs = reduce_adamw(packed, _pack([w[n] for n in CONV_W]), _pack([mom[n] for n in CONV_W]),
                        _pack([var[n] for n in CONV_W]), name="adamw_conv_w")
    for d, o in zip((grads, deltas, new_m, new_v), outs):
        for n, a in zip(CONV_W, _unpack(o, conv_shapes)):
            d[n] = a
    return loss, dh[None], grads, deltas, new_m, new_v


def kernel(x, p, norm_mix_g, norm_mlp_g, norm_ple_g, norm_f_g, w_in_rec, conv_a_w, conv_a_b, ln_a_g, ln_a_b, conv_b_w, conv_b_b, w_rg_a, b_rg_a, w_rg_x, b_rg_x, rg_lambda, w_out_rec, w_qkv, w_o_attn, w_mlp_up, w_mlp_down, w_ple_proj, w_ple_gate, loss_target, m_norm_mix_g, m_norm_mlp_g, m_norm_ple_g, m_norm_f_g, m_w_in_rec, m_conv_a_w, m_conv_a_b, m_ln_a_g, m_ln_a_b, m_conv_b_w, m_conv_b_b, m_w_rg_a, m_b_rg_a, m_w_rg_x, m_b_rg_x, m_rg_lambda, m_w_out_rec, m_w_qkv, m_w_o_attn, m_w_mlp_up, m_w_mlp_down, m_w_ple_proj, m_w_ple_gate, v_norm_mix_g, v_norm_mlp_g, v_norm_ple_g, v_norm_f_g, v_w_in_rec, v_conv_a_w, v_conv_a_b, v_ln_a_g, v_ln_a_b, v_conv_b_w, v_conv_b_b, v_w_rg_a, v_b_rg_a, v_w_rg_x, v_b_rg_x, v_rg_lambda, v_w_out_rec, v_w_qkv, v_w_o_attn, v_w_mlp_up, v_w_mlp_down, v_w_ple_proj, v_w_ple_gate):
    given = dict(locals())
    w = {n: given[n] for n in WEIGHTS}
    mom = {n: given["m_" + n] for n in WEIGHTS}
    var = {n: given["v_" + n] for n in WEIGHTS}
    loss, grad_x, grads, deltas, new_m, new_v = _step(x, p, loss_target, w, mom, var)
    return (loss, grad_x, *[grads[n] for n in WEIGHTS], *[deltas[n] for n in WEIGHTS],
            *[new_m[n] for n in WEIGHTS], *[new_v[n] for n in WEIGHTS])
```

```python
import functools
import math

import jax
import jax.numpy as jnp
from jax import lax
from jax.experimental import pallas as pl
from jax.experimental.pallas import tpu as pltpu

F32, BF16 = jnp.float32, jnp.bfloat16
EPS = 1e-6
N_DEV = 8
SB_HEADS = 16
RG_C = 8.0
HALO_A = 32
HALO_B = 8
LANES = 128
VMEM_LIMIT = 48 * 1024 * 1024
ADAM_LR, ADAM_B1, ADAM_B2, ADAM_EPS, ADAM_WD, ADAM_STEP = 0.001, 0.9, 0.999, 1e-08, 0.01, 10
MESH = pl.DeviceIdType.MESH
SDS = jax.ShapeDtypeStruct
ANY = pl.BlockSpec(memory_space=pl.ANY)


def _cp(*sem):
    return pltpu.CompilerParams(dimension_semantics=sem, vmem_limit_bytes=VMEM_LIMIT)


def _tile(dim, pref, align=LANES):
    if dim <= pref:
        return dim
    t = (pref // align) * align
    while t >= align:
        if dim % t == 0:
            return t
        t -= align
    return dim


def _softplus(z):
    return jnp.maximum(z, 0.0) + jnp.log(1.0 + jnp.exp(-jnp.abs(z)))


def _expm1(x):
    t = x * (1.0 + x * (0.5 + x * (1.0 / 6.0 + x * (1.0 / 24.0 + x * (1.0 / 120.0)))))
    return jnp.where(jnp.abs(x) < 0.1, t, jnp.exp(x) - 1.0)


_GELU_C = math.sqrt(2.0 / math.pi)


def _gelu(x):
    return 0.5 * x * (1.0 + jnp.tanh(_GELU_C * (x + 0.044715 * x * x * x)))


def _gelu_grad(x):
    th = jnp.tanh(_GELU_C * (x + 0.044715 * x * x * x))
    return 0.5 * (1.0 + th) + 0.5 * x * (1.0 - th * th) * _GELU_C * (1.0 + 3.0 * 0.044715 * x * x)


def _dot(a, b):
    return jnp.dot(a, b, preferred_element_type=F32)


def _dot_nt(a, b):
    return lax.dot_general(a, b, (((1,), (1,)), ((), ())), preferred_element_type=F32)


def _dot_tn(a, b):
    return lax.dot_general(a, b, (((0,), (0,)), ((), ())), preferred_element_type=F32)


def mm_w(a, w, layer, *, transpose_w, out_dtypes, name, extras=(), epilogue=None, tm=512, tn=1024, tk=2048):
    m, k_dim = a.shape
    _, rows, cols = w.shape
    n = rows if transpose_w else cols
    assert k_dim == (cols if transpose_w else rows), (a.shape, w.shape)
    tm, tn, tk = _tile(m, tm, 8), _tile(n, tn), _tile(k_dim, tk)
    nk = k_dim // tk
    n_extra, n_out = len(extras), len(out_dtypes)
    if epilogue is None:
        epilogue = lambda acc: (acc,)
    dot = _dot_nt if transpose_w else _dot

    def body(a_ref, w_ref, *rest):
        extra_refs, out_refs = rest[:n_extra], rest[n_extra:n_extra + n_out]

        def finish(acc):
            outs = epilogue(acc, *[r[...] for r in extra_refs])
            for o_ref, o in zip(out_refs, outs):
                o_ref[...] = o.astype(o_ref.dtype)

        if nk == 1:
            finish(dot(a_ref[...], w_ref[...]))
            return
        acc = rest[-1]
        k = pl.program_id(2)

        @pl.when(k == 0)
        def _():
            acc[...] = jnp.zeros_like(acc)

        acc[...] += dot(a_ref[...], w_ref[...])

        @pl.when(k == nk - 1)
        def _():
            finish(acc[...])

    if transpose_w:
        w_spec = pl.BlockSpec((None, tn, tk), lambda i, j, k: (layer, j, k))
    else:
        w_spec = pl.BlockSpec((None, tk, tn), lambda i, j, k: (layer, k, j))
    tile_spec = pl.BlockSpec((tm, tn), lambda i, j, k: (i, j))
    return pl.pallas_call(
        body, name=name, grid=(m // tm, n // tn, nk),
        in_specs=[pl.BlockSpec((tm, tk), lambda i, j, k: (i, k)), w_spec] + [tile_spec] * n_extra,
        out_specs=[tile_spec] * n_out,
        out_shape=[SDS((m, n), dt) for dt in out_dtypes],
        scratch_shapes=[pltpu.VMEM((tm, tn), F32)] if nk > 1 else [],
        compiler_params=_cp("parallel", "parallel", "arbitrary"),
    )(a, w, *extras)


def mm_dw(a, b, buf, layer, n_layers, name, tm=512, tn=1024, tk=2048):
    tokens, m = a.shape
    _, n = b.shape
    tm, tn, tk = _tile(m, tm), _tile(n, tn), _tile(tokens, tk)
    nk = tokens // tk

    def body(a_ref, b_ref, *rest):
        o_ref, acc = rest[-2], rest[-1]
        k = pl.program_id(2)

        @pl.when(k == 0)
        def _():
            acc[...] = jnp.zeros_like(acc)

        acc[...] += _dot_tn(a_ref[...], b_ref[...])

        @pl.when(k == nk - 1)
        def _():
            o_ref[...] = acc[...].astype(BF16)

    ins = [a, b] + ([] if buf is None else [buf])
    return pl.pallas_call(
        body, name=name, grid=(m // tm, n // tn, nk),
        in_specs=[pl.BlockSpec((tk, tm), lambda i, j, k: (k, i)),
                  pl.BlockSpec((tk, tn), lambda i, j, k: (k, j))] + ([] if buf is None else [ANY]),
        out_specs=pl.BlockSpec((None, tm, tn), lambda i, j, k: (layer, i, j)),
        out_shape=SDS((n_layers, m, n), BF16),
        scratch_shapes=[pltpu.VMEM((tm, tn), F32)],
        input_output_aliases={} if buf is None else {2: 0},
        compiler_params=_cp("parallel", "parallel", "arbitrary"),
    )(*ins)


def rms_fwd(h, g, name):
    t, d = h.shape
    tt = _tile(t, 512, 8)

    def body(h_ref, g_ref, o_ref):
        x = h_ref[...]
        r = lax.rsqrt(jnp.mean(x * x, axis=-1, keepdims=True) + EPS)
        o_ref[...] = (x * r * g_ref[...]).astype(BF16)

    row = pl.BlockSpec((tt, d), lambda i: (i, 0))
    return pl.pallas_call(
        body, name=name, grid=(t // tt,),
        in_specs=[row, pl.BlockSpec((1, d), lambda i: (0, 0))], out_specs=row,
        out_shape=SDS((t, d), BF16), compiler_params=_cp("parallel"),
    )(h, g.reshape(1, d))


def _rms_bwd_math(x, g, dhn):
    r = lax.rsqrt(jnp.mean(x * x, axis=-1, keepdims=True) + EPS)
    xn = x * r
    dxn = dhn * g
    dx = r * (dxn - xn * jnp.mean(dxn * xn, axis=-1, keepdims=True))
    return dx, jnp.sum(dhn * xn, axis=0, keepdims=True)


def rms_bwd(h, g, dhn, dres, name):
    t, d = h.shape
    tt = _tile(t, 256, 8)

    def body(h_ref, g_ref, dhn_ref, dres_ref, dh_ref, dhb_ref, dg_ref):
        @pl.when(pl.program_id(0) == 0)
        def _():
            dg_ref[...] = jnp.zeros_like(dg_ref)

        dx, dg = _rms_bwd_math(h_ref[...], g_ref[...], dhn_ref[...].astype(F32))
        dh = dres_ref[...] + dx
        dh_ref[...] = dh
        dhb_ref[...] = dh.astype(BF16)
        dg_ref[...] += dg

    row = pl.BlockSpec((tt, d), lambda i: (i, 0))
    vec = pl.BlockSpec((1, d), lambda i: (0, 0))
    return pl.pallas_call(
        body, name=name, grid=(t // tt,),
        in_specs=[row, vec, row, row], out_specs=[row, row, vec],
        out_shape=[SDS((t, d), F32), SDS((t, d), BF16), SDS((1, d), F32)],
        compiler_params=_cp("arbitrary"),
    )(h, g.reshape(1, d), dhn, dres)


def loss_head(h, g, target, name):
    t, d = h.shape
    tt = _tile(t, 256, 8)

    def body(h_ref, g_ref, t_ref, dh_ref, dg_ref, loss_ref):
        @pl.when(pl.program_id(0) == 0)
        def _():
            dg_ref[...] = jnp.zeros_like(dg_ref)
            loss_ref[...] = jnp.zeros_like(loss_ref)

        x, gain = h_ref[...], g_ref[...]
        r = lax.rsqrt(jnp.mean(x * x, axis=-1, keepdims=True) + EPS)
        err = x * r * gain - t_ref[...]
        loss_ref[...] += 0.5 * jnp.sum(jnp.mean(err * err, axis=-1, keepdims=True))
        dx, dg = _rms_bwd_math(x, gain, err * (1.0 / d))
        dh_ref[...] = dx
        dg_ref[...] += dg

    row = pl.BlockSpec((tt, d), lambda i: (i, 0))
    vec = pl.BlockSpec((1, d), lambda i: (0, 0))
    return pl.pallas_call(
        body, name=name, grid=(t // tt,),
        in_specs=[row, vec, row],
        out_specs=[row, vec, pl.BlockSpec((8, LANES), lambda i: (0, 0))],
        out_shape=[SDS((t, d), F32), SDS((1, d), F32), SDS((8, LANES), F32)],
        compiler_params=_cp("arbitrary"),
    )(h, g.reshape(1, d), target)


def ple_bwd(dh, gz, pp, name):
    t, d = dh.shape
    tt = _tile(t, 256, 8)

    def body(dh_ref, gz_ref, pp_ref, dpp_ref, dgz_ref):
        g = dh_ref[...]
        gate = jax.nn.sigmoid(gz_ref[...].astype(F32))
        dpp_ref[...] = (g * gate).astype(BF16)
        dgz_ref[...] = (g * pp_ref[...].astype(F32) * gate * (1.0 - gate)).astype(BF16)

    row = pl.BlockSpec((tt, d), lambda i: (i, 0))
    return pl.pallas_call(
        body, name=name, grid=(t // tt,), in_specs=[row, row, row], out_specs=[row, row],
        out_shape=[SDS((t, d), BF16), SDS((t, d), BF16)], compiler_params=_cp("parallel"),
    )(dh, gz, pp)


def _layer_norm_parts(y1):
    mu = jnp.mean(y1, axis=-1, keepdims=True)
    dlt = y1 - mu
    rstd = lax.rsqrt(jnp.mean(dlt * dlt, axis=-1, keepdims=True) + EPS)
    return dlt * rstd, rstd


def conf_fwd(u, cw, cb, lg, lb, name):
    t = u.shape[0]
    ka, c = cw.shape
    tt = _tile(t, 256, HALO_A)
    hb = tt // HALO_A
    off = HALO_A - (ka - 1)

    def body(av_ref, ag_ref, pv_ref, pg_ref, cw_ref, cb_ref, lg_ref, lb_ref, ya_ref, ya1_ref, buf):
        live = (pl.program_id(0) > 0).astype(F32)
        buf[0:HALO_A, :] = pv_ref[...] * jax.nn.sigmoid(pg_ref[...]) * live
        buf[HALO_A:, :] = av_ref[...] * jax.nn.sigmoid(ag_ref[...])
        for c0 in range(0, c, LANES):
            cs = pl.ds(c0, LANES)
            acc = jnp.broadcast_to(cb_ref[:, cs], (tt, LANES))
            for k in range(ka):
                acc = acc + cw_ref[k:k + 1, cs] * buf[pl.ds(off + k, tt), cs]
            ya1_ref[:, cs] = acc
        yn, _ = _layer_norm_parts(ya1_ref[...])
        y2 = yn * lg_ref[...] + lb_ref[...]
        ya_ref[...] = (y2 * jax.nn.sigmoid(y2)).astype(BF16)

    cur = lambda col: pl.BlockSpec((tt, c), lambda i: (i, col))
    prev = lambda col: pl.BlockSpec((HALO_A, c), lambda i: (jnp.maximum(i * hb - 1, 0), col))
    vec = pl.BlockSpec((1, c), lambda i: (0, 0))
    return pl.pallas_call(
        body, name=name, grid=(t // tt,),
        in_specs=[cur(0), cur(1), prev(0), prev(1), pl.BlockSpec((ka, c), lambda i: (0, 0)), vec, vec, vec],
        out_specs=[pl.BlockSpec((tt, c), lambda i: (i, 0))] * 2,
        out_shape=[SDS((t, c), BF16), SDS((t, c), F32)],
        scratch_shapes=[pltpu.VMEM((tt + HALO_A, c), F32)],
        compiler_params=_cp("parallel"),
    )(u, u, u, u, cw, cb, lg, lb)


def conf_bwd(u, ya1, dcat, cw, lg, lb, name):
    t = u.shape[0]
    ka, c = cw.shape
    tt = _tile(t, 256, HALO_A)
    nt, hb = t // tt, tt // HALO_A
    off = HALO_A - (ka - 1)

    def body(av_ref, ag_ref, pv_ref, pg_ref, y1_ref, dya_ref, cw_ref, lg_ref, lb_ref,
             du_ref, pgrad_ref, ybuf, dbuf, carry):
        i = pl.program_id(0)

        @pl.when(i == 0)
        def _():
            carry[...] = jnp.zeros_like(carry)
            pgrad_ref[...] = jnp.zeros_like(pgrad_ref)

        live = (i < nt - 1).astype(F32)
        av = av_ref[...]
        sg = jax.nn.sigmoid(ag_ref[...])
        ybuf[0:HALO_A, :] = pv_ref[...] * jax.nn.sigmoid(pg_ref[...]) * live
        ybuf[HALO_A:, :] = av * sg
        yn, rstd = _layer_norm_parts(y1_ref[...])
        gain = lg_ref[...]
        y2 = yn * gain + lb_ref[...]
        s2 = jax.nn.sigmoid(y2)
        dy2 = dya_ref[...].astype(F32) * (s2 * (1.0 + y2 * (1.0 - s2)))
        pgrad_ref[ka + 1:ka + 2, :] += jnp.sum(dy2 * yn, axis=0, keepdims=True)
        pgrad_ref[ka + 2:ka + 3, :] += jnp.sum(dy2, axis=0, keepdims=True)
        dyn = dy2 * gain
        dy1 = rstd * (dyn - jnp.mean(dyn, axis=-1, keepdims=True)
                      - yn * jnp.mean(dyn * yn, axis=-1, keepdims=True))
        pgrad_ref[ka:ka + 1, :] += jnp.sum(dy1, axis=0, keepdims=True)
        dbuf[0:tt, :] = dy1
        dbuf[tt:, :] = carry[...]
        carry[...] = dbuf[0:HALO_A, :]
        for c0 in range(0, c, LANES):
            cs = pl.ds(c0, LANES)
            d_cur = dbuf[0:tt, cs]
            acc = jnp.zeros((tt, LANES), F32)
            for k in range(ka):
                acc = acc + cw_ref[k:k + 1, cs] * dbuf[pl.ds(ka - 1 - k, tt), cs]
                pgrad_ref[k:k + 1, cs] += jnp.sum(d_cur * ybuf[pl.ds(off + k, tt), cs], axis=0, keepdims=True)
            sgc, avc = sg[:, c0:c0 + LANES], av[:, c0:c0 + LANES]
            du_ref[:, cs] = (acc * sgc).astype(BF16)
            du_ref[:, pl.ds(c + c0, LANES)] = (acc * avc * sgc * (1.0 - sgc)).astype(BF16)

    cur = lambda col: pl.BlockSpec((tt, c), lambda i: (nt - 1 - i, col))
    prev = lambda col: pl.BlockSpec((HALO_A, c), lambda i: (jnp.maximum((nt - 1 - i) * hb - 1, 0), col))
    vec = pl.BlockSpec((1, c), lambda i: (0, 0))
    return pl.pallas_call(
        body, name=name, grid=(nt,),
        in_specs=[cur(0), cur(1), prev(0), prev(1), cur(0), cur(0),
                  pl.BlockSpec((ka, c), lambda i: (0, 0)), vec, vec],
        out_specs=[pl.BlockSpec((tt, 2 * c), lambda i: (nt - 1 - i, 0)),
                   pl.BlockSpec((ka + 3, c), lambda i: (0, 0))],
        out_shape=[SDS((t, 2 * c), BF16), SDS((ka + 3, c), F32)],
        scratch_shapes=[pltpu.VMEM((tt + HALO_A, c), F32), pltpu.VMEM((tt + HALO_A, c), F32),
                        pltpu.VMEM((HALO_A, c), F32)],
        compiler_params=_cp("arbitrary"),
    )(u, u, u, u, ya1, dcat, cw, lg, lb)


def _rg_gates(xc, wa_ref, ba, wx_ref, bx, sp, nh, hd):
    parts = []
    for h in range(nh):
        hs = slice(h * hd, (h + 1) * hd)
        xh = xc[:, hs].astype(BF16)
        r = jax.nn.sigmoid(_dot(xh, wa_ref[h]) + ba[:, hs])
        ig = jax.nn.sigmoid(_dot(xh, wx_ref[h]) + bx[:, hs])
        log_a = -RG_C * r * sp[:, hs]
        parts.append((r, ig, jnp.exp(log_a), jnp.sqrt(-_expm1(2.0 * log_a))))
    return parts


def _conv_b(xbuf, bw_ref, bb, tt, kb):
    off = HALO_B - (kb - 1)
    xc = bb
    for k in range(kb):
        xc = xc + bw_ref[k:k + 1, :] * xbuf[pl.ds(off + k, tt), :]
    return xc


def rglru_fwd(u, bw, bb, wa, ba, wx, bx, lam, name):
    t = u.shape[0]
    kb, c = bw.shape
    nh, hd, _ = wa.shape
    tt = _tile(t, 256, HALO_B)
    hb = tt // HALO_B

    def body(xr_ref, gr_ref, px_ref, bw_ref, bb_ref, wa_ref, ba_ref, wx_ref, bx_ref, lam_ref,
             yb_ref, hs_ref, xbuf, a_s, u_s, hc):
        i = pl.program_id(0)

        @pl.when(i == 0)
        def _():
            hc[...] = jnp.zeros_like(hc)

        xbuf[0:HALO_B, :] = px_ref[...] * (i > 0).astype(F32)
        xbuf[HALO_B:, :] = xr_ref[...]
        xc = _conv_b(xbuf, bw_ref, bb_ref[...], tt, kb)
        sp = _softplus(-lam_ref[...])
        gates = _rg_gates(xc, wa_ref, ba_ref[...], wx_ref, bx_ref[...], sp, nh, hd)
        for h, (_, ig, a, mult) in enumerate(gates):
            hs = slice(h * hd, (h + 1) * hd)
            a_s[:, hs] = a
            u_s[:, hs] = mult * ig * xc[:, hs]

        def step(g, hcur):
            base = pl.multiple_of(g * 8, 8)
            for j in range(8):
                hcur = a_s[pl.ds(base + j, 1), :] * hcur + u_s[pl.ds(base + j, 1), :]
                hs_ref[pl.ds(base + j, 1), :] = hcur
            return hcur

        hc[...] = lax.fori_loop(0, tt // 8, step, hc[...])
        yb_ref[...] = (hs_ref[...] * _gelu(gr_ref[...])).astype(BF16)

    cur = lambda col: pl.BlockSpec((tt, c), lambda i: (i, col))
    vec = pl.BlockSpec((1, c), lambda i: (0, 0))
    wsp = pl.BlockSpec((nh, hd, hd), lambda i: (0, 0, 0))
    return pl.pallas_call(
        body, name=name, grid=(t // tt,),
        in_specs=[cur(2), cur(3), pl.BlockSpec((HALO_B, c), lambda i: (jnp.maximum(i * hb - 1, 0), 2)),
                  pl.BlockSpec((kb, c), lambda i: (0, 0)), vec, wsp, vec, wsp, vec, vec],
        out_specs=[pl.BlockSpec((tt, c), lambda i: (i, 0))] * 2,
        out_shape=[SDS((t, c), BF16), SDS((t, c), F32)],
        scratch_shapes=[pltpu.VMEM((tt + HALO_B, c), F32), pltpu.VMEM((tt, c), F32),
                        pltpu.VMEM((tt, c), F32), pltpu.VMEM((1, c), F32)],
        compiler_params=_cp("arbitrary"),
    )(u, u, u, bw, bb, wa, ba, wx, bx, lam)


def rglru_bwd(u, hs_all, dcat, bw, bb, wa, ba, wx, bx, lam, name):
    t = u.shape[0]
    kb, c = bw.shape
    nh, hd, _ = wa.shape
    tt = _tile(t, 256, HALO_B)
    nt, hb = t // tt, tt // HALO_B
    off = HALO_B - (kb - 1)

    def body(xr_ref, gr_ref, px_ref, hs_ref, ph_ref, dyb_ref, bw_ref, bb_ref, wa_ref, ba_ref, wx_ref, bx_ref,
             lam_ref, du_ref, pgrad_ref, dwa_ref, dwx_ref, xbuf, hbuf, a_s, g_s, dxbuf, cg, cdx):
        i = pl.program_id(0)

        @pl.when(i == 0)
        def _():
            cg[...] = jnp.zeros_like(cg)
            cdx[...] = jnp.zeros_like(cdx)
            pgrad_ref[...] = jnp.zeros_like(pgrad_ref)
            dwa_ref[...] = jnp.zeros_like(dwa_ref)
            dwx_ref[...] = jnp.zeros_like(dwx_ref)

        live = (i < nt - 1).astype(F32)
        xbuf[0:HALO_B, :] = px_ref[...] * live
        xbuf[HALO_B:, :] = xr_ref[...]
        hbuf[0:HALO_B, :] = ph_ref[...] * live
        hbuf[HALO_B:, :] = hs_ref[...]
        xc = _conv_b(xbuf, bw_ref, bb_ref[...], tt, kb)
        lam_v = lam_ref[...]
        sp = _softplus(-lam_v)
        gates = _rg_gates(xc, wa_ref, ba_ref[...], wx_ref, bx_ref[...], sp, nh, hd)
        gr = gr_ref[...]
        dyb = dyb_ref[...].astype(F32)
        g_s[...] = dyb * _gelu(gr)
        for h, (_, _, a, _) in enumerate(gates):
            a_s[:, h * hd:(h + 1) * hd] = a

        def step(g, carry):
            base = pl.multiple_of((tt // 8 - 1 - g) * 8, 8)
            for j in range(7, -1, -1):
                gt = g_s[pl.ds(base + j, 1), :] + carry
                g_s[pl.ds(base + j, 1), :] = gt
                carry = a_s[pl.ds(base + j, 1), :] * gt
            return carry

        cg[...] = lax.fori_loop(0, tt // 8, step, cg[...])
        hprev = hbuf[pl.ds(HALO_B - 1, tt), :]
        gfull = g_s[...]
        for h, (r, ig, a, mult) in enumerate(gates):
            hs = slice(h * hd, (h + 1) * hd)
            g, xch = gfull[:, hs], xc[:, hs]
            d_la = g * hprev[:, hs] * a - g * ig * xch * (a * a) / mult
            d_ig = g * mult * xch
            d_ra = d_la * (-RG_C * sp[:, hs]) * r * (1.0 - r)
            d_ia = d_ig * ig * (1.0 - ig)
            dsp = jnp.sum(d_la * (-RG_C) * r, axis=0, keepdims=True)
            pgrad_ref[kb + 1:kb + 2, hs] += jnp.sum(d_ra, axis=0, keepdims=True)
            pgrad_ref[kb + 2:kb + 3, hs] += jnp.sum(d_ia, axis=0, keepdims=True)
            pgrad_ref[kb + 3:kb + 4, hs] += dsp * (-jax.nn.sigmoid(-lam_v[:, hs]))
            xh, d_ra_b, d_ia_b = xch.astype(BF16), d_ra.astype(BF16), d_ia.astype(BF16)
            dwa_ref[h] += _dot_tn(xh, d_ra_b)
            dwx_ref[h] += _dot_tn(xh, d_ia_b)
            dxbuf[0:tt, hs] = g * mult * ig + _dot_nt(d_ra_b, wa_ref[h]) + _dot_nt(d_ia_b, wx_ref[h])
        dxbuf[tt:, :] = cdx[...]
        cdx[...] = dxbuf[0:HALO_B, :]
        d_xc = dxbuf[0:tt, :]
        pgrad_ref[kb:kb + 1, :] += jnp.sum(d_xc, axis=0, keepdims=True)
        d_xr = jnp.zeros((tt, c), F32)
        for k in range(kb):
            d_xr = d_xr + bw_ref[k:k + 1, :] * dxbuf[pl.ds(kb - 1 - k, tt), :]
            pgrad_ref[k:k + 1, :] += jnp.sum(d_xc * xbuf[pl.ds(off + k, tt), :], axis=0, keepdims=True)
        du_ref[:, 0:c] = d_xr.astype(BF16)
        du_ref[:, c:2 * c] = (dyb * hs_ref[...] * _gelu_grad(gr)).astype(BF16)

    cur = lambda col: pl.BlockSpec((tt, c), lambda i: (nt - 1 - i, col))
    prev = lambda col: pl.BlockSpec((HALO_B, c), lambda i: (jnp.maximum((nt - 1 - i) * hb - 1, 0), col))
    vec = pl.BlockSpec((1, c), lambda i: (0, 0))
    wsp = pl.BlockSpec((nh, hd, hd), lambda i: (0, 0, 0))
    return pl.pallas_call(
        body, name=name, grid=(nt,),
        in_specs=[cur(2), cur(3), prev(2), cur(0), prev(0), cur(1),
                  pl.BlockSpec((kb, c), lambda i: (0, 0)), vec, wsp, vec, wsp, vec, vec],
        out_specs=[pl.BlockSpec((tt, 2 * c), lambda i: (nt - 1 - i, 0)),
                   pl.BlockSpec((kb + 4, c), lambda i: (0, 0)), wsp, wsp],
        out_shape=[SDS((t, 2 * c), BF16), SDS((kb + 4, c), F32), SDS((nh, hd, hd), F32), SDS((nh, hd, hd), F32)],
        scratch_shapes=[pltpu.VMEM((tt + HALO_B, c), F32), pltpu.VMEM((tt + HALO_B, c), F32),
                        pltpu.VMEM((tt, c), F32), pltpu.VMEM((tt, c), F32), pltpu.VMEM((tt + HALO_B, c), F32),
                        pltpu.VMEM((1, c), F32), pltpu.VMEM((HALO_B, c), F32)],
        compiler_params=_cp("arbitrary"),
    )(u, u, u, hs_all, hs_all, dcat, bw, bb, wa, ba, wx, bx, lam)


MASS_CUTOFF = 100.0


def attn_fwd(qkv, nh, name):
    t = qkv.shape[0]
    dh = qkv.shape[1] // (3 * nh)
    bq = _tile(t, 256)
    scale = 1.0 / math.sqrt(dh)

    def body(q_ref, k_ref, v_ref, o_ref, s_ref, start_ref):
        hd, qi = pl.program_id(0), pl.program_id(1)
        q = q_ref[...]
        row = lax.broadcasted_iota(jnp.int32, (bq, bq), 0)
        col = lax.broadcasted_iota(jnp.int32, (bq, bq), 1)
        tri = (row >= col).astype(BF16)

        def block(kb, c, acc, mask):
            ks = pl.ds(pl.multiple_of(kb * bq, bq), bq)
            z = _dot_nt(q, k_ref[ks, :]) * scale
            sp = _softplus(z)
            if mask is not None:
                sp = jnp.where(mask, sp, 0.0)
            lw = z - c - _dot(sp.astype(BF16), tri)
            if mask is not None:
                lw = jnp.where(mask, lw, -1e30)
            acc = acc + _dot(jnp.exp(lw).astype(BF16), v_ref[ks, :])
            return c + jnp.sum(sp, axis=1, keepdims=True), acc

        c, acc = block(qi, jnp.zeros((bq, 1), F32), jnp.zeros((bq, dh), F32), col < row)

        def step(it, st):
            first, c_old, acc_old = st
            return lax.cond(jnp.min(c_old) < MASS_CUTOFF,
                            lambda: (qi - 1 - it,) + block(qi - 1 - it, c_old, acc_old, None),
                            lambda: st)

        first, c, acc = lax.fori_loop(0, qi, step, (qi, c, acc))
        o_ref[...] = acc.astype(BF16)
        s_ref[...] = jnp.broadcast_to(c, (bq, LANES))
        start_ref[hd, qi] = first.astype(F32)

    whole = lambda base: pl.BlockSpec((t, dh), lambda h, qi: (0, base + h))
    return pl.pallas_call(
        body, name=name, grid=(nh, t // bq),
        in_specs=[pl.BlockSpec((bq, dh), lambda h, qi: (qi, h)), whole(nh), whole(2 * nh)],
        out_specs=[pl.BlockSpec((bq, dh), lambda h, qi: (qi, h)),
                   pl.BlockSpec((None, bq, LANES), lambda h, qi: (h, qi, 0)),
                   pl.BlockSpec(memory_space=pltpu.SMEM)],
        out_shape=[SDS((t, nh * dh), BF16), SDS((nh, t, LANES), F32), SDS((nh, t // bq), F32)],
        compiler_params=_cp("arbitrary", "arbitrary"),
    )(qkv, qkv, qkv)


def attn_bwd(qkv, do, s_tot, start, nh, name):
    t = qkv.shape[0]
    dh = qkv.shape[1] // (3 * nh)
    bq = _tile(t, 256)
    scale = 1.0 / math.sqrt(dh)

    def body(start_ref, q_ref, k_ref, v_ref, do_ref, s_ref, dq_ref, dk_ref, dv_ref):
        hd, qi = pl.program_id(0), pl.program_id(1)

        @pl.when(qi == 0)
        def _():
            dk_ref[...] = jnp.zeros_like(dk_ref)
            dv_ref[...] = jnp.zeros_like(dv_ref)

        q, do_b = q_ref[...], do_ref[...]
        s_row = s_ref[:, 0:1]
        row = lax.broadcasted_iota(jnp.int32, (bq, bq), 0)
        col = lax.broadcasted_iota(jnp.int32, (bq, bq), 1)
        tri_suffix = (row >= col).astype(BF16)
        tri_prefix = (row <= col).astype(BF16)

        def block(kb, psp, pg, dq, mask):
            ks = pl.ds(pl.multiple_of(kb * bq, bq), bq)
            k_b, v_b = k_ref[ks, :], v_ref[ks, :]
            z = _dot_nt(q, k_b) * scale
            sp_all = _softplus(z)
            sp = sp_all if mask is None else jnp.where(mask, sp_all, 0.0)
            psp = psp + jnp.sum(sp, axis=1, keepdims=True)
            lw = z - (s_row - psp) - _dot(sp.astype(BF16), tri_suffix)
            if mask is not None:
                lw = jnp.where(mask, lw, -1e30)
            a = jnp.exp(lw)
            g = _dot_nt(do_b, v_b) * a
            dz = g - (pg + _dot(g.astype(BF16), tri_prefix)) * jnp.exp(z - sp_all)
            if mask is not None:
                dz = jnp.where(mask, dz, 0.0)
            dzs = (dz * scale).astype(BF16)
            dk_ref[ks, :] += _dot_tn(dzs, q)
            dv_ref[ks, :] += _dot_tn(a.astype(BF16), do_b)
            return psp, pg + jnp.sum(g, axis=1, keepdims=True), dq + _dot(dzs, k_b)

        first = jnp.clip(start_ref[hd, qi].astype(jnp.int32), 0, qi)
        init = (jnp.zeros((bq, 1), F32), jnp.zeros((bq, 1), F32), jnp.zeros((bq, dh), F32))
        psp, pg, dq = lax.fori_loop(
            0, qi, lambda kb, cr: lax.cond(kb >= first, lambda: block(kb, cr[0], cr[1], cr[2], None), lambda: cr), init)
        _, _, dq = block(qi, psp, pg, dq, col < row)
        dq_ref[...] = dq.astype(BF16)

    whole = lambda base: pl.BlockSpec((t, dh), lambda h, qi: (0, base + h))
    qblk = pl.BlockSpec((bq, dh), lambda h, qi: (qi, h))
    acc = pl.BlockSpec((t, dh), lambda h, qi: (0, h))
    return pl.pallas_call(
        body, name=name, grid=(nh, t // bq),
        in_specs=[pl.BlockSpec(memory_space=pltpu.SMEM), qblk, whole(nh), whole(2 * nh), qblk,
                  pl.BlockSpec((None, bq, LANES), lambda h, qi: (h, qi, 0))],
        out_specs=[qblk, acc, acc],
        out_shape=[SDS((t, nh * dh), BF16), SDS((t, nh * dh), F32), SDS((t, nh * dh), F32)],
        compiler_params=_cp("arbitrary", "arbitrary"),
    )(start, qkv, qkv, qkv, do, s_tot)


def _shard_of(ref, axis, dev, size):
    if axis is None:
        return ref.at[dev]
    return ref.at[(slice(None),) * axis + (pl.ds(pl.multiple_of(dev * size, size), size),)]


def all_gather(x, name, axis=None):
    if axis is None:
        out_shape = (N_DEV,) + x.shape
    else:
        out_shape = x.shape[:axis] + (N_DEV * x.shape[axis],) + x.shape[axis + 1:]

    def body(x_ref, out_ref, send_sems, recv_sems, local_sem):
        mx, my, mc = lax.axis_index("x"), lax.axis_index("y"), lax.axis_index("c")
        me, sibling = (mx, my, mc), (mx, my, 1 - mc)
        chips = [(1 - mx, my), (mx, 1 - my), (1 - mx, 1 - my)]

        def slot(px, py, pc):
            return _shard_of(out_ref, axis, 4 * px + 2 * py + pc, None if axis is None else x.shape[axis])

        def copy(k, block, to, src=None):
            return pltpu.make_async_remote_copy(
                src_ref=slot(*block) if src is None else src, dst_ref=slot(*block),
                send_sem=send_sems.at[k], recv_sem=recv_sems.at[k], device_id=to, device_id_type=MESH)

        mine = pltpu.make_async_copy(x_ref, slot(*me), local_sem)
        mine.start()
        first = [copy(0, me, sibling, src=x_ref)]
        first += [copy(1 + j, me, (*chip, mc), src=x_ref) for j, chip in enumerate(chips)]
        for cp in first:
            cp.start()
        passed = [copy(4 + j, (*chip, mc), sibling) for j, chip in enumerate(chips)]
        for j, chip in enumerate(chips):
            copy(1 + j, (*chip, mc), me).wait_recv()
            passed[j].start()
        copy(0, sibling, me).wait_recv()
        for j, chip in enumerate(chips):
            copy(4 + j, (*chip, 1 - mc), me).wait_recv()
        for cp in first + passed:
            cp.wait_send()
        mine.wait()

    return pl.pallas_call(
        body, name=name, out_shape=SDS(out_shape, x.dtype), in_specs=[ANY], out_specs=ANY,
        scratch_shapes=[pltpu.SemaphoreType.DMA((7,)), pltpu.SemaphoreType.DMA((7,)), pltpu.SemaphoreType.DMA],
    )(x)


def exchange_slices(g, axis, name):
    size = g.shape[axis] // N_DEV
    shard_shape = g.shape[:axis] + (size,) + g.shape[axis + 1:]

    def body(g_ref, out_ref, send_sems, recv_sems, local_sem):
        mx, my, mc = lax.axis_index("x"), lax.axis_index("y"), lax.axis_index("c")
        me = 4 * mx + 2 * my + mc
        mine = pltpu.make_async_copy(_shard_of(g_ref, axis, me, size), out_ref.at[me], local_sem)
        mine.start()
        copies = []
        for k in range(1, N_DEV):
            px = 1 - mx if k & 4 else mx
            py = 1 - my if k & 2 else my
            pc = 1 - mc if k & 1 else mc
            cp = pltpu.make_async_remote_copy(
                src_ref=_shard_of(g_ref, axis, 4 * px + 2 * py + pc, size), dst_ref=out_ref.at[me],
                send_sem=send_sems.at[k - 1], recv_sem=recv_sems.at[k - 1],
                device_id=(px, py, pc), device_id_type=MESH)
            cp.start()
            copies.append(cp)
        for cp in copies:
            cp.wait()
        mine.wait()

    return pl.pallas_call(
        body, name=name, out_shape=SDS((N_DEV,) + shard_shape, g.dtype), in_specs=[ANY], out_specs=ANY,
        scratch_shapes=[pltpu.SemaphoreType.DMA((7,)), pltpu.SemaphoreType.DMA((7,)), pltpu.SemaphoreType.DMA],
    )(g)


def reduce_adamw(gs, w, m, v, name):
    s, r, c = gs.shape
    tr = _tile(r, max(16, (128 * 1024) // c), 16)

    def body(gs_ref, w_ref, m_ref, v_ref, g_out, d_out, m_out, v_out):
        g = gs_ref[0].astype(F32)
        for j in range(1, s):
            g = g + gs_ref[j].astype(F32)
        m_new = ADAM_B1 * m_ref[...] + (1.0 - ADAM_B1) * g
        v_new = ADAM_B2 * v_ref[...] + (1.0 - ADAM_B2) * (g * g)
        m_hat = m_new / (1.0 - ADAM_B1 ** ADAM_STEP)
        v_hat = v_new / (1.0 - ADAM_B2 ** ADAM_STEP)
        g_out[...] = g
        d_out[...] = -ADAM_LR * (m_hat / (jnp.sqrt(v_hat) + ADAM_EPS) + ADAM_WD * w_ref[...])
        m_out[...] = m_new
        v_out[...] = v_new

    row = pl.BlockSpec((tr, c), lambda i: (i, 0))
    return pl.pallas_call(
        body, name=name, grid=(r // tr,),
        in_specs=[pl.BlockSpec((s, tr, c), lambda i: (0, i, 0)), row, row, row], out_specs=[row] * 4,
        out_shape=[SDS((r, c), F32)] * 4, compiler_params=_cp("parallel"),
    )(gs, w, m, v)


BIG = {
    "w_in_rec": 2, "w_out_rec": 1, "w_qkv": 2, "w_o_attn": 1,
    "w_mlp_up": 2, "w_mlp_down": 1, "w_ple_proj": 2, "w_ple_gate": 1,
}
REPLICATED = ["norm_mix_g", "norm_mlp_g", "norm_ple_g", "norm_f_g", "conv_a_b", "ln_a_g", "ln_a_b", "conv_b_b",
              "w_rg_a", "b_rg_a", "w_rg_x", "b_rg_x", "rg_lambda"]
CONV_W = ["conv_a_w", "conv_b_w"]
WEIGHTS = ["norm_mix_g", "norm_mlp_g", "norm_ple_g", "norm_f_g", "w_in_rec", "conv_a_w", "conv_a_b", "ln_a_g",
           "ln_a_b", "conv_b_w", "conv_b_b", "w_rg_a", "b_rg_a", "w_rg_x", "b_rg_x", "rg_lambda", "w_out_rec",
           "w_qkv", "w_o_attn", "w_mlp_up", "w_mlp_down", "w_ple_proj", "w_ple_gate"]


def _pack(arrays):
    flat = jnp.concatenate([a.reshape(-1) for a in arrays])
    pad = (-flat.shape[0]) % (8 * LANES)
    return jnp.pad(flat, (0, pad)).reshape(-1, LANES)


def _unpack(packed, shapes, lead=()):
    flat = packed.reshape(lead + (-1,))
    out, pos = [], 0
    for shp in shapes:
        size = math.prod(shp)
        out.append(flat[..., pos:pos + size].reshape(lead + tuple(shp)))
        pos += size
    return out


def _step(x, p, loss_target, w, mom, var):
    dev = 4 * lax.axis_index("x") + 2 * lax.axis_index("y") + lax.axis_index("c")
    depth = w["norm_mix_g"].shape[0]
    h = x[0]
    nh = SB_HEADS

    wg = {n: all_gather(w[n].astype(BF16), name=f"ag_{n}", axis=BIG[n]) for n in BIG}
    conv_shapes = [w[n].shape for n in CONV_W]
    conv_all = all_gather(_pack([w[n] for n in CONV_W]), name="ag_conv_w")
    conv_full = [jnp.moveaxis(a, 0, -2).reshape(a.shape[1:-1] + (-1,))
                 for a in _unpack(conv_all, conv_shapes, lead=(N_DEV,))]
    conv_a_w, conv_b_w = conv_full
    vec = lambda a: a.reshape(1, -1)

    saved = []
    for i in range(depth):
        j = i // 2
        s = {"h0": h}
        s["hn1"] = rms_fwd(h, w["norm_mix_g"][i], name="rms_fwd")
        if i % 2 == 0:
            (s["u"],) = mm_w(s["hn1"], wg["w_in_rec"], j, transpose_w=False, out_dtypes=[F32], name="mm_in_rec")
            ya, s["ya1"] = conf_fwd(s["u"], conv_a_w[j], vec(w["conv_a_b"][j]), vec(w["ln_a_g"][j]),
                                    vec(w["ln_a_b"][j]), name="conf_fwd")
            yb, s["hs"] = rglru_fwd(s["u"], conv_b_w[j], vec(w["conv_b_b"][j]), w["w_rg_a"][j].astype(BF16),
                                    vec(w["b_rg_a"][j]), w["w_rg_x"][j].astype(BF16), vec(w["b_rg_x"][j]),
                                    vec(w["rg_lambda"][j]), name="rglru_fwd")
            s["mix_in"] = jnp.concatenate([ya, yb], axis=1)
            w_out, out_name = wg["w_out_rec"], "w_out_rec"
        else:
            (s["qkv"],) = mm_w(s["hn1"], wg["w_qkv"], j, transpose_w=False, out_dtypes=[BF16], name="mm_qkv")
            s["mix_in"], s["s_tot"], s["start"] = attn_fwd(s["qkv"], nh, name="attn_fwd")
            w_out, out_name = wg["w_o_attn"], "w_o_attn"
        s["w_out"], s["out_name"] = w_out, out_name
        (h,) = mm_w(s["mix_in"], w_out, j, transpose_w=False, out_dtypes=[F32],
                    extras=[h], epilogue=lambda acc, res: (res + acc,), name="mm_mix_out")
        s["h1"] = h
        s["hn2"] = rms_fwd(h, w["norm_mlp_g"][i], name="rms_fwd")
        relu = lambda acc: jnp.maximum(acc, 0.0)
        s["up"], s["act"] = mm_w(s["hn2"], wg["w_mlp_up"], i, transpose_w=False,
                                 out_dtypes=[BF16, BF16], name="mm_mlp_up", tm=1024,
                                 epilogue=lambda acc: (acc, relu(acc) * relu(acc)))
        (h,) = mm_w(s["act"], wg["w_mlp_down"], i, transpose_w=False, out_dtypes=[F32], tm=1024, tk=1024,
                    extras=[h], epilogue=lambda acc, res: (res + acc,), name="mm_mlp_down")
        s["h2"] = h
        s["hn3"] = rms_fwd(h, w["norm_ple_g"][i], name="rms_fwd")
        s["p"] = p[i, 0].astype(BF16)
        (s["pp"],) = mm_w(s["p"], wg["w_ple_proj"], i, transpose_w=False, out_dtypes=[F32], name="mm_ple_proj")
        h, s["gz"] = mm_w(s["hn3"], wg["w_ple_gate"], i, transpose_w=False,
                          out_dtypes=[F32, BF16], extras=[h, s["pp"]], name="mm_ple_gate",
                          epilogue=lambda acc, res, pp: (res + pp * jax.nn.sigmoid(acc), acc))
        saved.append(s)

    dh, dg_f, loss_part = loss_head(h, w["norm_f_g"], loss_target[0], name="loss_head")
    loss = lax.psum(loss_part[0, 0], ("x", "y", "c"))

    acc = {n: None for n in BIG}
    part = {n: [None] * w[n].shape[0] for n in REPLICATED + CONV_W if n != "norm_f_g"}
    part["norm_f_g"] = dg_f[0]

    def dw(name, a, b, layer):
        acc[name] = mm_dw(a, b, acc[name], layer, wg[name].shape[0], name=f"dw_{name}")

    for i in reversed(range(depth)):
        j = i // 2
        s = saved[i]
        d_pp, d_gz = ple_bwd(dh, s["gz"], s["pp"], name="ple_bwd")
        dw("w_ple_proj", s["p"], d_pp, i)
        dw("w_ple_gate", s["hn3"], d_gz, i)
        (d_hn3,) = mm_w(d_gz, wg["w_ple_gate"], i, transpose_w=True, out_dtypes=[BF16], name="mmT_ple_gate")
        dh, dh_b, dg = rms_bwd(s["h2"], w["norm_ple_g"][i], d_hn3, dh, name="rms_bwd")
        part["norm_ple_g"][i] = dg[0]
        (d_up,) = mm_w(dh_b, wg["w_mlp_down"], i, transpose_w=True, out_dtypes=[BF16], tm=1024,
                       extras=[s["up"]], name="mmT_mlp_down",
                       epilogue=lambda acc_, up: (acc_ * (2.0 * jnp.maximum(up.astype(F32), 0.0)),))
        dw("w_mlp_down", s["act"], dh_b, i)
        dw("w_mlp_up", s["hn2"], d_up, i)
        (d_hn2,) = mm_w(d_up, wg["w_mlp_up"], i, transpose_w=True, out_dtypes=[BF16], tm=1024, name="mmT_mlp_up")
        dh, dh_b, dg = rms_bwd(s["h1"], w["norm_mlp_g"][i], d_hn2, dh, name="rms_bwd")
        part["norm_mlp_g"][i] = dg[0]
        dw(s["out_name"], s["mix_in"], dh_b, j)
        if i % 2 == 0:
            (d_cat,) = mm_w(dh_b, s["w_out"], j, transpose_w=True, out_dtypes=[BF16], name="mmT_mix_out")
            du_a, pg_a = conf_bwd(s["u"], s["ya1"], d_cat, conv_a_w[j], vec(w["ln_a_g"][j]), vec(w["ln_a_b"][j]),
                                  name="conf_bwd")
            du_b, pg_b, d_wa, d_wx = rglru_bwd(
                s["u"], s["hs"], d_cat, conv_b_w[j], vec(w["conv_b_b"][j]), w["w_rg_a"][j].astype(BF16),
                vec(w["b_rg_a"][j]), w["w_rg_x"][j].astype(BF16), vec(w["b_rg_x"][j]), vec(w["rg_lambda"][j]),
                name="rglru_bwd")
            ka, kb = conv_a_w.shape[1], conv_b_w.shape[1]
            part["conv_a_w"][j], part["conv_a_b"][j] = pg_a[:ka], pg_a[ka]
            part["ln_a_g"][j], part["ln_a_b"][j] = pg_a[ka + 1], pg_a[ka + 2]
            part["conv_b_w"][j], part["conv_b_b"][j] = pg_b[:kb], pg_b[kb]
            part["b_rg_a"][j], part["b_rg_x"][j], part["rg_lambda"][j] = pg_b[kb + 1], pg_b[kb + 2], pg_b[kb + 3]
            part["w_rg_a"][j], part["w_rg_x"][j] = d_wa, d_wx
            d_mix = jnp.concatenate([du_a, du_b], axis=1)
            w_in, in_name = wg["w_in_rec"], "w_in_rec"
        else:
            (d_o,) = mm_w(dh_b, s["w_out"], j, transpose_w=True, out_dtypes=[BF16], name="mmT_mix_out")
            dq, dk, dv = attn_bwd(s["qkv"], d_o, s["s_tot"], s["start"], nh, name="attn_bwd")
            d_mix = jnp.concatenate([dq, dk.astype(BF16), dv.astype(BF16)], axis=1)
            w_in, in_name = wg["w_qkv"], "w_qkv"
        dw(in_name, s["hn1"], d_mix, j)
        (d_hn1,) = mm_w(d_mix, w_in, j, transpose_w=True, out_dtypes=[BF16], name="mmT_mix_in")
        dh, _, dg = rms_bwd(s["h0"], w["norm_mix_g"][i], d_hn1, dh, name="rms_bwd")
        part["norm_mix_g"][i] = dg[0]

    grads, deltas, new_m, new_v = {}, {}, {}, {}

    def finish(name, outs, shape):
        for d, o in zip((grads, deltas, new_m, new_v), outs):
            d[name] = o.reshape(shape)

    for n in BIG:
        recv = exchange_slices(acc[n], BIG[n], name=f"xch_{n}")
        _, nl, r, c = recv.shape
        flat = lambda a: a.reshape(nl * r, c)
        finish(n, reduce_adamw(recv.reshape(N_DEV, nl * r, c), flat(w[n]), flat(mom[n]), flat(var[n]),
                               name=f"adamw_{n}"), w[n].shape)

    full = {n: (part[n] if n == "norm_f_g" else jnp.stack(part[n])) for n in REPLICATED + CONV_W}
    rep = _pack([full[n] for n in REPLICATED])
    conv = _pack([full[n] for n in CONV_W])
    small = all_gather(jnp.concatenate([rep, conv]), name="ag_small_grads")
    rep_shapes = [w[n].shape for n in REPLICATED]
    outs = reduce_adamw(small[:, :rep.shape[0]], _pack([w[n] for n in REPLICATED]),
                        _pack([mom[n] for n in REPLICATED]), _pack([var[n] for n in REPLICATED]), name="adamw_small")
    for d, o in zip((grads, deltas, new_m, new_v), outs):
        for n, a in zip(REPLICATED, _unpack(o, rep_shapes)):
            d[n] = a
    conv_parts = _unpack(small[:, rep.shape[0]:], [full[n].shape for n in CONV_W], lead=(N_DEV,))
    width = w["conv_a_w"].shape[-1]
    mine = [lax.dynamic_slice_in_dim(a, dev * width, width, axis=a.ndim - 1) for a in conv_parts]
    packed = jnp.stack([_pack([a[d] for a in mine]) for d in range(N_DEV)])
    outs = reduce_adamw(packed, _pack([w[n] for n in CONV_W]), _pack([mom[n] for n in CONV_W]),
                        _pack([var[n] for n in CONV_W]), name="adamw_conv_w")
    for d, o in zip((grads, deltas, new_m, new_v), outs):
        for n, a in zip(CONV_W, _unpack(o, conv_shapes)):
            d[n] = a
    return loss, dh[None], grads, deltas, new_m, new_v


def kernel(x, p, norm_mix_g, norm_mlp_g, norm_ple_g, norm_f_g, w_in_rec, conv_a_w, conv_a_b, ln_a_g, ln_a_b, conv_b_w, conv_b_b, w_rg_a, b_rg_a, w_rg_x, b_rg_x, rg_lambda, w_out_rec, w_qkv, w_o_attn, w_mlp_up, w_mlp_down, w_ple_proj, w_ple_gate, loss_target, m_norm_mix_g, m_norm_mlp_g, m_norm_ple_g, m_norm_f_g, m_w_in_rec, m_conv_a_w, m_conv_a_b, m_ln_a_g, m_ln_a_b, m_conv_b_w, m_conv_b_b, m_w_rg_a, m_b_rg_a, m_w_rg_x, m_b_rg_x, m_rg_lambda, m_w_out_rec, m_w_qkv, m_w_o_attn, m_w_mlp_up, m_w_mlp_down, m_w_ple_proj, m_w_ple_gate, v_norm_mix_g, v_norm_mlp_g, v_norm_ple_g, v_norm_f_g, v_w_in_rec, v_conv_a_w, v_conv_a_b, v_ln_a_g, v_ln_a_b, v_conv_b_w, v_conv_b_b, v_w_rg_a, v_b_rg_a, v_w_rg_x, v_b_rg_x, v_rg_lambda, v_w_out_rec, v_w_qkv, v_w_o_attn, v_w_mlp_up, v_w_mlp_down, v_w_ple_proj, v_w_ple_gate):
    given = dict(locals())
    w = {n: given[n] for n in WEIGHTS}
    mom = {n: given["m_" + n] for n in WEIGHTS}
    var = {n: given["v_" + n] for n in WEIGHTS}
    loss, grad_x, grads, deltas, new_m, new_v = _step(x, p, loss_target, w, mom, var)
    return (loss, grad_x, *[grads[n] for n in WEIGHTS], *[deltas[n] for n in WEIGHTS],
            *[new_m[n] for n in WEIGHTS], *[new_v[n] for n in WEIGHTS])
```

```python
import functools
import math

import jax
import jax.numpy as jnp
from jax import lax
from jax.experimental import pallas as pl
from jax.experimental.pallas import tpu as pltpu

F32, BF16 = jnp.float32, jnp.bfloat16
EPS = 1e-6
N_DEV = 8
SB_HEADS = 16
RG_C = 8.0
HALO_A = 32
HALO_B = 8
LANES = 128
VMEM_LIMIT = 48 * 1024 * 1024
ADAM_LR, ADAM_B1, ADAM_B2, ADAM_EPS, ADAM_WD, ADAM_STEP = 0.001, 0.9, 0.999, 1e-08, 0.01, 10
MESH = pl.DeviceIdType.MESH
SDS = jax.ShapeDtypeStruct
ANY = pl.BlockSpec(memory_space=pl.ANY)


def _cp(*sem):
    return pltpu.CompilerParams(dimension_semantics=sem, vmem_limit_bytes=VMEM_LIMIT)


def _tile(dim, pref, align=LANES):
    if dim <= pref:
        return dim
    t = (pref // align) * align
    while t >= align:
        if dim % t == 0:
            return t
        t -= align
    return dim


def _softplus(z):
    return jnp.maximum(z, 0.0) + jnp.log(1.0 + jnp.exp(-jnp.abs(z)))


def _expm1(x):
    t = x * (1.0 + x * (0.5 + x * (1.0 / 6.0 + x * (1.0 / 24.0 + x * (1.0 / 120.0)))))
    return jnp.where(jnp.abs(x) < 0.1, t, jnp.exp(x) - 1.0)


_GELU_C = math.sqrt(2.0 / math.pi)


def _gelu(x):
    return 0.5 * x * (1.0 + jnp.tanh(_GELU_C * (x + 0.044715 * x * x * x)))


def _gelu_grad(x):
    th = jnp.tanh(_GELU_C * (x + 0.044715 * x * x * x))
    return 0.5 * (1.0 + th) + 0.5 * x * (1.0 - th * th) * _GELU_C * (1.0 + 3.0 * 0.044715 * x * x)


def _dot(a, b):
    return jnp.dot(a, b, preferred_element_type=F32)


def _dot_nt(a, b):
    return lax.dot_general(a, b, (((1,), (1,)), ((), ())), preferred_element_type=F32)


def _dot_tn(a, b):
    return lax.dot_general(a, b, (((0,), (0,)), ((), ())), preferred_element_type=F32)


def _shard_of(ref, axis, dev, size):
    if axis is None:
        return ref.at[dev]
    return ref.at[(slice(None),) * axis + (pl.ds(pl.multiple_of(dev * size, size), size),)]


def _peer(k, mx, my, mc):
    return (1 - mx if k & 4 else mx, 1 - my if k & 2 else my, 1 - mc if k & 1 else mc)


class Ride:
    def __init__(self):
        self.items = []

    def gather(self, shard, axis, layer, whole_prev):
        shape = shard.shape[:axis] + (N_DEV * shard.shape[axis],) + shard.shape[axis + 1:]
        self.items.append(("gather", shard, whole_prev, SDS(shape, shard.dtype), axis, layer, shard.shape[axis]))
        return self

    def exchange(self, grad, axis, layer, recv_prev):
        size = grad.shape[axis] // N_DEV
        shape = (N_DEV,) + grad.shape[:axis] + (size,) + grad.shape[axis + 1:]
        self.items.append(("exchange", grad, recv_prev, SDS(shape, grad.dtype), axis, layer, size))
        return self

    def operands(self):
        return [a for it in self.items for a in ([it[1]] if it[2] is None else [it[1], it[2]])]

    def out_shapes(self):
        return [it[3] for it in self.items]

    def aliases(self, first_in, first_out):
        out, pos = {}, first_in
        for t, it in enumerate(self.items):
            pos += 1
            if it[2] is not None:
                out[pos] = first_out + t
                pos += 1
        return out

    def scratch(self):
        n = len(self.items)
        return [pltpu.SemaphoreType.DMA((n, N_DEV - 1)), pltpu.SemaphoreType.DMA((n, N_DEV - 1)),
                pltpu.SemaphoreType.DMA((n,))]

    def copies(self, in_refs, out_refs, send_sems, recv_sems, local_sems):
        mx, my, mc = lax.axis_index("x"), lax.axis_index("y"), lax.axis_index("c")
        me = 4 * mx + 2 * my + mc
        out, pos = [], 0
        for t, (kind, _, prev, _, axis, layer, size) in enumerate(self.items):
            src = in_refs[pos].at[layer]
            pos += 1 if prev is None else 2
            if kind == "gather":
                dst = _shard_of(out_refs[t].at[layer], axis - 1, me, size)
                mine = lambda d: src
            else:
                dst = out_refs[t].at[me, layer]
                mine = lambda d, src=src, axis=axis, size=size: _shard_of(src, axis - 1, d, size)
            out.append(pltpu.make_async_copy(mine(me), dst, local_sems.at[t]))
            for k in range(1, N_DEV):
                px, py, pc = _peer(k, mx, my, mc)
                out.append(pltpu.make_async_remote_copy(
                    src_ref=mine(4 * px + 2 * py + pc), dst_ref=dst, send_sem=send_sems.at[t, k - 1],
                    recv_sem=recv_sems.at[t, k - 1], device_id=(px, py, pc), device_id_type=MESH))
        return out


def _ride_hooks(ride, grid, in_refs, out_refs, sems):
    ids = [pl.program_id(d) for d in range(len(grid))]
    first = functools.reduce(jnp.logical_and, [i == 0 for i in ids])
    last = functools.reduce(jnp.logical_and, [i == g - 1 for i, g in zip(ids, grid)])

    def start():
        @pl.when(first)
        def _():
            for cp in ride.copies(in_refs, out_refs, *sems):
                cp.start()

    def finish():
        @pl.when(last)
        def _():
            for cp in ride.copies(in_refs, out_refs, *sems):
                cp.wait()

    return start, finish


def mm_w(a, w, layer, *, transpose_w, out_dtypes, name, extras=(), epilogue=None, ride=None,
         tm=512, tn=1024, tk=2048):
    m, k_dim = a.shape
    _, rows, cols = w.shape
    n = rows if transpose_w else cols
    assert k_dim == (cols if transpose_w else rows), (a.shape, w.shape)
    tm, tn, tk = _tile(m, tm, 8), _tile(n, tn), _tile(k_dim, tk)
    nk = k_dim // tk
    grid = (m // tm, n // tn, nk)
    n_extra, n_out = len(extras), len(out_dtypes)
    ride_in = ride.operands() if ride else []
    n_ride_out = len(ride.items) if ride else 0
    if epilogue is None:
        epilogue = lambda acc: (acc,)
    dot = _dot_nt if transpose_w else _dot

    def body(a_ref, w_ref, *rest):
        extra_refs, rest = rest[:n_extra], rest[n_extra:]
        ride_in_refs, rest = rest[:len(ride_in)], rest[len(ride_in):]
        out_refs, rest = rest[:n_out], rest[n_out:]
        ride_out_refs, scratch = rest[:n_ride_out], rest[n_ride_out:]
        if ride:
            start, finish = _ride_hooks(ride, grid, ride_in_refs, ride_out_refs, scratch[-3:])
            start()

        def write(acc):
            outs = epilogue(acc, *[r[...] for r in extra_refs])
            for o_ref, o in zip(out_refs, outs):
                o_ref[...] = o.astype(o_ref.dtype)

        if nk == 1:
            write(dot(a_ref[...], w_ref[...]))
        else:
            acc = scratch[0]
            k = pl.program_id(2)

            @pl.when(k == 0)
            def _():
                acc[...] = jnp.zeros_like(acc)

            acc[...] += dot(a_ref[...], w_ref[...])

            @pl.when(k == nk - 1)
            def _():
                write(acc[...])

        if ride:
            finish()

    if transpose_w:
        w_spec = pl.BlockSpec((None, tn, tk), lambda i, j, k: (layer, j, k))
    else:
        w_spec = pl.BlockSpec((None, tk, tn), lambda i, j, k: (layer, k, j))
    tile_spec = pl.BlockSpec((tm, tn), lambda i, j, k: (i, j))
    return pl.pallas_call(
        body, name=name, grid=grid,
        in_specs=[pl.BlockSpec((tm, tk), lambda i, j, k: (i, k)), w_spec] + [tile_spec] * n_extra
        + [ANY] * len(ride_in),
        out_specs=[tile_spec] * n_out + [ANY] * n_ride_out,
        out_shape=[SDS((m, n), dt) for dt in out_dtypes] + (ride.out_shapes() if ride else []),
        scratch_shapes=([pltpu.VMEM((tm, tn), F32)] if nk > 1 else []) + (ride.scratch() if ride else []),
        input_output_aliases=ride.aliases(2 + n_extra, n_out) if ride else {},
        compiler_params=_cp("arbitrary", "arbitrary", "arbitrary") if ride
        else _cp("parallel", "parallel", "arbitrary"),
    )(a, w, *extras, *ride_in)


def mm_dw(a, b, buf, layer, n_layers, name, ride=None, tm=512, tn=1024, tk=2048):
    tokens, m = a.shape
    _, n = b.shape
    tm, tn, tk = _tile(m, tm), _tile(n, tn), _tile(tokens, tk)
    nk = tokens // tk
    grid = (m // tm, n // tn, nk)
    n_buf = 0 if buf is None else 1
    ride_in = ride.operands() if ride else []
    n_ride_out = len(ride.items) if ride else 0

    def body(a_ref, b_ref, *rest):
        rest = rest[n_buf:]
        ride_in_refs, rest = rest[:len(ride_in)], rest[len(ride_in):]
        o_ref, rest = rest[0], rest[1:]
        ride_out_refs, scratch = rest[:n_ride_out], rest[n_ride_out:]
        acc = scratch[0]
        if ride:
            start, finish = _ride_hooks(ride, grid, ride_in_refs, ride_out_refs, scratch[-3:])
            start()
        k = pl.program_id(2)

        @pl.when(k == 0)
        def _():
            acc[...] = jnp.zeros_like(acc)

        acc[...] += _dot_tn(a_ref[...], b_ref[...])

        @pl.when(k == nk - 1)
        def _():
            o_ref[...] = acc[...].astype(BF16)

        if ride:
            finish()

    aliases = {} if buf is None else {2: 0}
    if ride:
        aliases.update(ride.aliases(2 + n_buf, 1))
    outs = pl.pallas_call(
        body, name=name, grid=grid,
        in_specs=[pl.BlockSpec((tk, tm), lambda i, j, k: (k, i)),
                  pl.BlockSpec((tk, tn), lambda i, j, k: (k, j))] + [ANY] * (n_buf + len(ride_in)),
        out_specs=[pl.BlockSpec((None, tm, tn), lambda i, j, k: (layer, i, j))] + [ANY] * n_ride_out,
        out_shape=[SDS((n_layers, m, n), BF16)] + (ride.out_shapes() if ride else []),
        scratch_shapes=[pltpu.VMEM((tm, tn), F32)] + (ride.scratch() if ride else []),
        input_output_aliases=aliases,
        compiler_params=_cp("arbitrary", "arbitrary", "arbitrary") if ride
        else _cp("parallel", "parallel", "arbitrary"),
    )(a, b, *([] if buf is None else [buf]), *ride_in)
    return outs if ride else outs[0]


def rms_fwd(h, g, name):
    t, d = h.shape
    tt = _tile(t, 512, 8)

    def body(h_ref, g_ref, o_ref):
        x = h_ref[...]
        r = lax.rsqrt(jnp.mean(x * x, axis=-1, keepdims=True) + EPS)
        o_ref[...] = (x * r * g_ref[...]).astype(BF16)

    row = pl.BlockSpec((tt, d), lambda i: (i, 0))
    return pl.pallas_call(
        body, name=name, grid=(t // tt,),
        in_specs=[row, pl.BlockSpec((1, d), lambda i: (0, 0))], out_specs=row,
        out_shape=SDS((t, d), BF16), compiler_params=_cp("parallel"),
    )(h, g.reshape(1, d))


def _rms_bwd_math(x, g, dhn):
    r = lax.rsqrt(jnp.mean(x * x, axis=-1, keepdims=True) + EPS)
    xn = x * r
    dxn = dhn * g
    dx = r * (dxn - xn * jnp.mean(dxn * xn, axis=-1, keepdims=True))
    return dx, jnp.sum(dhn * xn, axis=0, keepdims=True)


def rms_bwd(h, g, dhn, dres, name):
    t, d = h.shape
    tt = _tile(t, 256, 8)

    def body(h_ref, g_ref, dhn_ref, dres_ref, dh_ref, dhb_ref, dg_ref):
        @pl.when(pl.program_id(0) == 0)
        def _():
            dg_ref[...] = jnp.zeros_like(dg_ref)

        dx, dg = _rms_bwd_math(h_ref[...], g_ref[...], dhn_ref[...].astype(F32))
        dh = dres_ref[...] + dx
        dh_ref[...] = dh
        dhb_ref[...] = dh.astype(BF16)
        dg_ref[...] += dg

    row = pl.BlockSpec((tt, d), lambda i: (i, 0))
    vec = pl.BlockSpec((1, d), lambda i: (0, 0))
    return pl.pallas_call(
        body, name=name, grid=(t // tt,),
        in_specs=[row, vec, row, row], out_specs=[row, row, vec],
        out_shape=[SDS((t, d), F32), SDS((t, d), BF16), SDS((1, d), F32)],
        compiler_params=_cp("arbitrary"),
    )(h, g.reshape(1, d), dhn, dres)


def loss_head(h, g, target, name):
    t, d = h.shape
    tt = _tile(t, 256, 8)

    def body(h_ref, g_ref, t_ref, dh_ref, dg_ref, loss_ref):
        @pl.when(pl.program_id(0) == 0)
        def _():
            dg_ref[...] = jnp.zeros_like(dg_ref)
            loss_ref[...] = jnp.zeros_like(loss_ref)

        x, gain = h_ref[...], g_ref[...]
        r = lax.rsqrt(jnp.mean(x * x, axis=-1, keepdims=True) + EPS)
        err = x * r * gain - t_ref[...]
        loss_ref[...] += 0.5 * jnp.sum(jnp.mean(err * err, axis=-1, keepdims=True))
        dx, dg = _rms_bwd_math(x, gain, err * (1.0 / d))
        dh_ref[...] = dx
        dg_ref[...] += dg

    row = pl.BlockSpec((tt, d), lambda i: (i, 0))
    vec = pl.BlockSpec((1, d), lambda i: (0, 0))
    return pl.pallas_call(
        body, name=name, grid=(t // tt,),
        in_specs=[row, vec, row],
        out_specs=[row, vec, pl.BlockSpec((8, LANES), lambda i: (0, 0))],
        out_shape=[SDS((t, d), F32), SDS((1, d), F32), SDS((8, LANES), F32)],
        compiler_params=_cp("arbitrary"),
    )(h, g.reshape(1, d), target)


def ple_bwd(dh, gz, pp, name):
    t, d = dh.shape
    tt = _tile(t, 256, 8)

    def body(dh_ref, gz_ref, pp_ref, dpp_ref, dgz_ref):
        g = dh_ref[...]
        gate = jax.nn.sigmoid(gz_ref[...].astype(F32))
        dpp_ref[...] = (g * gate).astype(BF16)
        dgz_ref[...] = (g * pp_ref[...].astype(F32) * gate * (1.0 - gate)).astype(BF16)

    row = pl.BlockSpec((tt, d), lambda i: (i, 0))
    return pl.pallas_call(
        body, name=name, grid=(t // tt,), in_specs=[row, row, row], out_specs=[row, row],
        out_shape=[SDS((t, d), BF16), SDS((t, d), BF16)], compiler_params=_cp("parallel"),
    )(dh, gz, pp)


def _layer_norm_parts(y1):
    mu = jnp.mean(y1, axis=-1, keepdims=True)
    dlt = y1 - mu
    rstd = lax.rsqrt(jnp.mean(dlt * dlt, axis=-1, keepdims=True) + EPS)
    return dlt * rstd, rstd


def conf_fwd(u, cw, cb, lg, lb, name):
    t = u.shape[0]
    ka, c = cw.shape
    tt = _tile(t, 256, HALO_A)
    hb = tt // HALO_A
    off = HALO_A - (ka - 1)

    def body(av_ref, ag_ref, pv_ref, pg_ref, cw_ref, cb_ref, lg_ref, lb_ref, ya_ref, ya1_ref, buf):
        live = (pl.program_id(0) > 0).astype(F32)
        buf[0:HALO_A, :] = pv_ref[...] * jax.nn.sigmoid(pg_ref[...]) * live
        buf[HALO_A:, :] = av_ref[...] * jax.nn.sigmoid(ag_ref[...])
        for c0 in range(0, c, LANES):
            cs = pl.ds(c0, LANES)
            acc = jnp.broadcast_to(cb_ref[:, cs], (tt, LANES))
            for k in range(ka):
                acc = acc + cw_ref[k:k + 1, cs] * buf[pl.ds(off + k, tt), cs]
            ya1_ref[:, cs] = acc
        yn, _ = _layer_norm_parts(ya1_ref[...])
        y2 = yn * lg_ref[...] + lb_ref[...]
        ya_ref[...] = (y2 * jax.nn.sigmoid(y2)).astype(BF16)

    cur = lambda col: pl.BlockSpec((tt, c), lambda i: (i, col))
    prev = lambda col: pl.BlockSpec((HALO_A, c), lambda i: (jnp.maximum(i * hb - 1, 0), col))
    vec = pl.BlockSpec((1, c), lambda i: (0, 0))
    return pl.pallas_call(
        body, name=name, grid=(t // tt,),
        in_specs=[cur(0), cur(1), prev(0), prev(1), pl.BlockSpec((ka, c), lambda i: (0, 0)), vec, vec, vec],
        out_specs=[pl.BlockSpec((tt, c), lambda i: (i, 0))] * 2,
        out_shape=[SDS((t, c), BF16), SDS((t, c), F32)],
        scratch_shapes=[pltpu.VMEM((tt + HALO_A, c), F32)],
        compiler_params=_cp("parallel"),
    )(u, u, u, u, cw, cb, lg, lb)


def conf_bwd(u, ya1, dcat, cw, lg, lb, name):
    t = u.shape[0]
    ka, c = cw.shape
    tt = _tile(t, 256, HALO_A)
    nt, hb = t // tt, tt // HALO_A
    off = HALO_A - (ka - 1)

    def body(av_ref, ag_ref, pv_ref, pg_ref, y1_ref, dya_ref, cw_ref, lg_ref, lb_ref,
             du_ref, pgrad_ref, ybuf, dbuf, carry):
        i = pl.program_id(0)

        @pl.when(i == 0)
        def _():
            carry[...] = jnp.zeros_like(carry)
            pgrad_ref[...] = jnp.zeros_like(pgrad_ref)

        live = (i < nt - 1).astype(F32)
        av = av_ref[...]
        sg = jax.nn.sigmoid(ag_ref[...])
        ybuf[0:HALO_A, :] = pv_ref[...] * jax.nn.sigmoid(pg_ref[...]) * live
        ybuf[HALO_A:, :] = av * sg
        yn, rstd = _layer_norm_parts(y1_ref[...])
        gain = lg_ref[...]
        y2 = yn * gain + lb_ref[...]
        s2 = jax.nn.sigmoid(y2)
        dy2 = dya_ref[...].astype(F32) * (s2 * (1.0 + y2 * (1.0 - s2)))
        pgrad_ref[ka + 1:ka + 2, :] += jnp.sum(dy2 * yn, axis=0, keepdims=True)
        pgrad_ref[ka + 2:ka + 3, :] += jnp.sum(dy2, axis=0, keepdims=True)
        dyn = dy2 * gain
        dy1 = rstd * (dyn - jnp.mean(dyn, axis=-1, keepdims=True)
                      - yn * jnp.mean(dyn * yn, axis=-1, keepdims=True))
        pgrad_ref[ka:ka + 1, :] += jnp.sum(dy1, axis=0, keepdims=True)
        dbuf[0:tt, :] = dy1
        dbuf[tt:, :] = carry[...]
        carry[...] = dbuf[0:HALO_A, :]
        for c0 in range(0, c, LANES):
            cs = pl.ds(c0, LANES)
            d_cur = dbuf[0:tt, cs]
            acc = jnp.zeros((tt, LANES), F32)
            for k in range(ka):
                acc = acc + cw_ref[k:k + 1, cs] * dbuf[pl.ds(ka - 1 - k, tt), cs]
                pgrad_ref[k:k + 1, cs] += jnp.sum(d_cur * ybuf[pl.ds(off + k, tt), cs], axis=0, keepdims=True)
            sgc, avc = sg[:, c0:c0 + LANES], av[:, c0:c0 + LANES]
            du_ref[:, cs] = (acc * sgc).astype(BF16)
            du_ref[:, pl.ds(c + c0, LANES)] = (acc * avc * sgc * (1.0 - sgc)).astype(BF16)

    cur = lambda col: pl.BlockSpec((tt, c), lambda i: (nt - 1 - i, col))
    prev = lambda col: pl.BlockSpec((HALO_A, c), lambda i: (jnp.maximum((nt - 1 - i) * hb - 1, 0), col))
    vec = pl.BlockSpec((1, c), lambda i: (0, 0))
    return pl.pallas_call(
        body, name=name, grid=(nt,),
        in_specs=[cur(0), cur(1), prev(0), prev(1), cur(0), cur(0),
                  pl.BlockSpec((ka, c), lambda i: (0, 0)), vec, vec],
        out_specs=[pl.BlockSpec((tt, 2 * c), lambda i: (nt - 1 - i, 0)),
                   pl.BlockSpec((ka + 3, c), lambda i: (0, 0))],
        out_shape=[SDS((t, 2 * c), BF16), SDS((ka + 3, c), F32)],
        scratch_shapes=[pltpu.VMEM((tt + HALO_A, c), F32), pltpu.VMEM((tt + HALO_A, c), F32),
                        pltpu.VMEM((HALO_A, c), F32)],
        compiler_params=_cp("arbitrary"),
    )(u, u, u, u, ya1, dcat, cw, lg, lb)


def _rg_gates(xc, wa_ref, ba, wx_ref, bx, sp, nh, hd):
    parts = []
    for h in range(nh):
        hs = slice(h * hd, (h + 1) * hd)
        xh = xc[:, hs].astype(BF16)
        r = jax.nn.sigmoid(_dot(xh, wa_ref[h]) + ba[:, hs])
        ig = jax.nn.sigmoid(_dot(xh, wx_ref[h]) + bx[:, hs])
        log_a = -RG_C * r * sp[:, hs]
        parts.append((r, ig, jnp.exp(log_a), jnp.sqrt(-_expm1(2.0 * log_a))))
    return parts


def _conv_b(xbuf, bw_ref, bb, tt, kb):
    off = HALO_B - (kb - 1)
    xc = bb
    for k in range(kb):
        xc = xc + bw_ref[k:k + 1, :] * xbuf[pl.ds(off + k, tt), :]
    return xc


def rglru_fwd(u, bw, bb, wa, ba, wx, bx, lam, name):
    t = u.shape[0]
    kb, c = bw.shape
    nh, hd, _ = wa.shape
    tt = _tile(t, 256, HALO_B)
    hb = tt // HALO_B

    def body(xr_ref, gr_ref, px_ref, bw_ref, bb_ref, wa_ref, ba_ref, wx_ref, bx_ref, lam_ref,
             yb_ref, hs_ref, xbuf, a_s, u_s, hc):
        i = pl.program_id(0)

        @pl.when(i == 0)
        def _():
            hc[...] = jnp.zeros_like(hc)

        xbuf[0:HALO_B, :] = px_ref[...] * (i > 0).astype(F32)
        xbuf[HALO_B:, :] = xr_ref[...]
        xc = _conv_b(xbuf, bw_ref, bb_ref[...], tt, kb)
        sp = _softplus(-lam_ref[...])
        gates = _rg_gates(xc, wa_ref, ba_ref[...], wx_ref, bx_ref[...], sp, nh, hd)
        for h, (_, ig, a, mult) in enumerate(gates):
            hs = slice(h * hd, (h + 1) * hd)
            a_s[:, hs] = a
            u_s[:, hs] = mult * ig * xc[:, hs]

        def step(g, hcur):
            base = pl.multiple_of(g * 8, 8)
            for j in range(8):
                hcur = a_s[pl.ds(base + j, 1), :] * hcur + u_s[pl.ds(base + j, 1), :]
                hs_ref[pl.ds(base + j, 1), :] = hcur
            return hcur

        hc[...] = lax.fori_loop(0, tt // 8, step, hc[...])
        yb_ref[...] = (hs_ref[...] * _gelu(gr_ref[...])).astype(BF16)

    cur = lambda col: pl.BlockSpec((tt, c), lambda i: (i, col))
    vec = pl.BlockSpec((1, c), lambda i: (0, 0))
    wsp = pl.BlockSpec((nh, hd, hd), lambda i: (0, 0, 0))
    return pl.pallas_call(
        body, name=name, grid=(t // tt,),
        in_specs=[cur(2), cur(3), pl.BlockSpec((HALO_B, c), lambda i: (jnp.maximum(i * hb - 1, 0), 2)),
                  pl.BlockSpec((kb, c), lambda i: (0, 0)), vec, wsp, vec, wsp, vec, vec],
        out_specs=[pl.BlockSpec((tt, c), lambda i: (i, 0))] * 2,
        out_shape=[SDS((t, c), BF16), SDS((t, c), F32)],
        scratch_shapes=[pltpu.VMEM((tt + HALO_B, c), F32), pltpu.VMEM((tt, c), F32),
                        pltpu.VMEM((tt, c), F32), pltpu.VMEM((1, c), F32)],
        compiler_params=_cp("arbitrary"),
    )(u, u, u, bw, bb, wa, ba, wx, bx, lam)


def rglru_bwd(u, hs_all, dcat, bw, bb, wa, ba, wx, bx, lam, name):
    t = u.shape[0]
    kb, c = bw.shape
    nh, hd, _ = wa.shape
    tt = _tile(t, 256, HALO_B)
    nt, hb = t // tt, tt // HALO_B
    off = HALO_B - (kb - 1)

    def body(xr_ref, gr_ref, px_ref, hs_ref, ph_ref, dyb_ref, bw_ref, bb_ref, wa_ref, ba_ref, wx_ref, bx_ref,
             lam_ref, du_ref, pgrad_ref, dwa_ref, dwx_ref, xbuf, hbuf, a_s, g_s, dxbuf, cg, cdx):
        i = pl.program_id(0)

        @pl.when(i == 0)
        def _():
            cg[...] = jnp.zeros_like(cg)
            cdx[...] = jnp.zeros_like(cdx)
            pgrad_ref[...] = jnp.zeros_like(pgrad_ref)
            dwa_ref[...] = jnp.zeros_like(dwa_ref)
            dwx_ref[...] = jnp.zeros_like(dwx_ref)

        live = (i < nt - 1).astype(F32)
        xbuf[0:HALO_B, :] = px_ref[...] * live
        xbuf[HALO_B:, :] = xr_ref[...]
        hbuf[0:HALO_B, :] = ph_ref[...] * live
        hbuf[HALO_B:, :] = hs_ref[...]
        xc = _conv_b(xbuf, bw_ref, bb_ref[...], tt, kb)
        lam_v = lam_ref[...]
        sp = _softplus(-lam_v)
        gates = _rg_gates(xc, wa_ref, ba_ref[...], wx_ref, bx_ref[...], sp, nh, hd)
        gr = gr_ref[...]
        dyb = dyb_ref[...].astype(F32)
        g_s[...] = dyb * _gelu(gr)
        for h, (_, _, a, _) in enumerate(gates):
            a_s[:, h * hd:(h + 1) * hd] = a

        def step(g, carry):
            base = pl.multiple_of((tt // 8 - 1 - g) * 8, 8)
            for j in range(7, -1, -1):
                gt = g_s[pl.ds(base + j, 1), :] + carry
                g_s[pl.ds(base + j, 1), :] = gt
                carry = a_s[pl.ds(base + j, 1), :] * gt
            return carry

        cg[...] = lax.fori_loop(0, tt // 8, step, cg[...])
        hprev = hbuf[pl.ds(HALO_B - 1, tt), :]
        gfull = g_s[...]
        for h, (r, ig, a, mult) in enumerate(gates):
            hs = slice(h * hd, (h + 1) * hd)
            g, xch = gfull[:, hs], xc[:, hs]
            d_la = g * hprev[:, hs] * a - g * ig * xch * (a * a) / mult
            d_ig = g * mult * xch
            d_ra = d_la * (-RG_C * sp[:, hs]) * r * (1.0 - r)
            d_ia = d_ig * ig * (1.0 - ig)
            dsp = jnp.sum(d_la * (-RG_C) * r, axis=0, keepdims=True)
            pgrad_ref[kb + 1:kb + 2, hs] += jnp.sum(d_ra, axis=0, keepdims=True)
            pgrad_ref[kb + 2:kb + 3, hs] += jnp.sum(d_ia, axis=0, keepdims=True)
            pgrad_ref[kb + 3:kb + 4, hs] += dsp * (-jax.nn.sigmoid(-lam_v[:, hs]))
            xh, d_ra_b, d_ia_b = xch.astype(BF16), d_ra.astype(BF16), d_ia.astype(BF16)
            dwa_ref[h] += _dot_tn(xh, d_ra_b)
            dwx_ref[h] += _dot_tn(xh, d_ia_b)
            dxbuf[0:tt, hs] = g * mult * ig + _dot_nt(d_ra_b, wa_ref[h]) + _dot_nt(d_ia_b, wx_ref[h])
        dxbuf[tt:, :] = cdx[...]
        cdx[...] = dxbuf[0:HALO_B, :]
        d_xc = dxbuf[0:tt, :]
        pgrad_ref[kb:kb + 1, :] += jnp.sum(d_xc, axis=0, keepdims=True)
        d_xr = jnp.zeros((tt, c), F32)
        for k in range(kb):
            d_xr = d_xr + bw_ref[k:k + 1, :] * dxbuf[pl.ds(kb - 1 - k, tt), :]
            pgrad_ref[k:k + 1, :] += jnp.sum(d_xc * xbuf[pl.ds(off + k, tt), :], axis=0, keepdims=True)
        du_ref[:, 0:c] = d_xr.astype(BF16)
        du_ref[:, c:2 * c] = (dyb * hs_ref[...] * _gelu_grad(gr)).astype(BF16)

    cur = lambda col: pl.BlockSpec((tt, c), lambda i: (nt - 1 - i, col))
    prev = lambda col: pl.BlockSpec((HALO_B, c), lambda i: (jnp.maximum((nt - 1 - i) * hb - 1, 0), col))
    vec = pl.BlockSpec((1, c), lambda i: (0, 0))
    wsp = pl.BlockSpec((nh, hd, hd), lambda i: (0, 0, 0))
    return pl.pallas_call(
        body, name=name, grid=(nt,),
        in_specs=[cur(2), cur(3), prev(2), cur(0), prev(0), cur(1),
                  pl.BlockSpec((kb, c), lambda i: (0, 0)), vec, wsp, vec, wsp, vec, vec],
        out_specs=[pl.BlockSpec((tt, 2 * c), lambda i: (nt - 1 - i, 0)),
                   pl.BlockSpec((kb + 4, c), lambda i: (0, 0)), wsp, wsp],
        out_shape=[SDS((t, 2 * c), BF16), SDS((kb + 4, c), F32), SDS((nh, hd, hd), F32), SDS((nh, hd, hd), F32)],
        scratch_shapes=[pltpu.VMEM((tt + HALO_B, c), F32), pltpu.VMEM((tt + HALO_B, c), F32),
                        pltpu.VMEM((tt, c), F32), pltpu.VMEM((tt, c), F32), pltpu.VMEM((tt + HALO_B, c), F32),
                        pltpu.VMEM((1, c), F32), pltpu.VMEM((HALO_B, c), F32)],
        compiler_params=_cp("arbitrary"),
    )(u, u, u, hs_all, hs_all, dcat, bw, bb, wa, ba, wx, bx, lam)


MASS_CUTOFF = 100.0


def attn_fwd(qkv, nh, name):
    t = qkv.shape[0]
    dh = qkv.shape[1] // (3 * nh)
    bq = _tile(t, 256)
    scale = 1.0 / math.sqrt(dh)

    def body(q_ref, k_ref, v_ref, o_ref, s_ref, start_ref):
        hd, qi = pl.program_id(0), pl.program_id(1)
        q = q_ref[...]
        row = lax.broadcasted_iota(jnp.int32, (bq, bq), 0)
        col = lax.broadcasted_iota(jnp.int32, (bq, bq), 1)
        tri = (row >= col).astype(BF16)

        def block(kb, c, acc, mask):
            ks = pl.ds(pl.multiple_of(kb * bq, bq), bq)
            z = _dot_nt(q, k_ref[ks, :]) * scale
            sp = _softplus(z)
            if mask is not None:
                sp = jnp.where(mask, sp, 0.0)
            lw = z - c - _dot(sp.astype(BF16), tri)
            if mask is not None:
                lw = jnp.where(mask, lw, -1e30)
            acc = acc + _dot(jnp.exp(lw).astype(BF16), v_ref[ks, :])
            return c + jnp.sum(sp, axis=1, keepdims=True), acc

        c, acc = block(qi, jnp.zeros((bq, 1), F32), jnp.zeros((bq, dh), F32), col < row)
        c, acc = block(jnp.maximum(qi - 1, 0), c, acc, qi > 0)

        def more(st):
            return jnp.logical_and(st[0] >= 0, jnp.min(st[1]) < MASS_CUTOFF)

        def step(st):
            c_new, acc_new = block(st[0], st[1], st[2], None)
            return st[0] - 1, c_new, acc_new

        kb, c, acc = lax.while_loop(more, step, (qi - 2, c, acc))
        o_ref[...] = acc.astype(BF16)
        s_ref[...] = jnp.broadcast_to(c, (bq, LANES))
        start_ref[hd, qi] = (kb + 1).astype(F32)

    whole = lambda base: pl.BlockSpec((t, dh), lambda h, qi: (0, base + h))
    return pl.pallas_call(
        body, name=name, grid=(nh, t // bq),
        in_specs=[pl.BlockSpec((bq, dh), lambda h, qi: (qi, h)), whole(nh), whole(2 * nh)],
        out_specs=[pl.BlockSpec((bq, dh), lambda h, qi: (qi, h)),
                   pl.BlockSpec((None, bq, LANES), lambda h, qi: (h, qi, 0)),
                   pl.BlockSpec(memory_space=pltpu.SMEM)],
        out_shape=[SDS((t, nh * dh), BF16), SDS((nh, t, LANES), F32), SDS((nh, t // bq), F32)],
        compiler_params=_cp("arbitrary", "arbitrary"),
    )(qkv, qkv, qkv)


def attn_bwd(qkv, do, s_tot, start, nh, name):
    t = qkv.shape[0]
    dh = qkv.shape[1] // (3 * nh)
    bq = _tile(t, 256)
    scale = 1.0 / math.sqrt(dh)

    def body(start_ref, q_ref, k_ref, v_ref, do_ref, s_ref, dq_ref, dk_ref, dv_ref):
        hd, qi = pl.program_id(0), pl.program_id(1)

        @pl.when(qi == 0)
        def _():
            dk_ref[...] = jnp.zeros_like(dk_ref)
            dv_ref[...] = jnp.zeros_like(dv_ref)

        q, do_b = q_ref[...], do_ref[...]
        s_row = s_ref[:, 0:1]
        row = lax.broadcasted_iota(jnp.int32, (bq, bq), 0)
        col = lax.broadcasted_iota(jnp.int32, (bq, bq), 1)
        tri_suffix = (row >= col).astype(BF16)
        tri_prefix = (row <= col).astype(BF16)

        def block(kb, psp, pg, dq, mask):
            ks = pl.ds(pl.multiple_of(kb * bq, bq), bq)
            k_b, v_b = k_ref[ks, :], v_ref[ks, :]
            z = _dot_nt(q, k_b) * scale
            sp_all = _softplus(z)
            sp = sp_all if mask is None else jnp.where(mask, sp_all, 0.0)
            psp = psp + jnp.sum(sp, axis=1, keepdims=True)
            lw = z - (s_row - psp) - _dot(sp.astype(BF16), tri_suffix)
            if mask is not None:
                lw = jnp.where(mask, lw, -1e30)
            a = jnp.exp(lw)
            g = _dot_nt(do_b, v_b) * a
            dz = g - (pg + _dot(g.astype(BF16), tri_prefix)) * jnp.exp(z - sp_all)
            if mask is not None:
                dz = jnp.where(mask, dz, 0.0)
            dzs = (dz * scale).astype(BF16)
            dk_ref[ks, :] += _dot_tn(dzs, q)
            dv_ref[ks, :] += _dot_tn(a.astype(BF16), do_b)
            return psp, pg + jnp.sum(g, axis=1, keepdims=True), dq + _dot(dzs, k_b)

        first = jnp.clip(start_ref[hd, qi].astype(jnp.int32), 0, qi)
        init = (jnp.zeros((bq, 1), F32), jnp.zeros((bq, 1), F32), jnp.zeros((bq, dh), F32))
        psp, pg, dq = lax.fori_loop(first, jnp.maximum(qi - 1, first),
                                    lambda kb, cr: block(kb, cr[0], cr[1], cr[2], None), init)
        psp, pg, dq = block(jnp.maximum(qi - 1, 0), psp, pg, dq, jnp.logical_and(qi > 0, first < qi))
        _, _, dq = block(qi, psp, pg, dq, col < row)
        dq_ref[...] = dq.astype(BF16)

    whole = lambda base: pl.BlockSpec((t, dh), lambda h, qi: (0, base + h))
    qblk = pl.BlockSpec((bq, dh), lambda h, qi: (qi, h))
    acc = pl.BlockSpec((t, dh), lambda h, qi: (0, h))
    return pl.pallas_call(
        body, name=name, grid=(nh, t // bq),
        in_specs=[pl.BlockSpec(memory_space=pltpu.SMEM), qblk, whole(nh), whole(2 * nh), qblk,
                  pl.BlockSpec((None, bq, LANES), lambda h, qi: (h, qi, 0))],
        out_specs=[qblk, acc, acc],
        out_shape=[SDS((t, nh * dh), BF16), SDS((t, nh * dh), F32), SDS((t, nh * dh), F32)],
        compiler_params=_cp("arbitrary", "arbitrary"),
    )(start, qkv, qkv, qkv, do, s_tot)


def all_gather(x, name, axis=None, first_layer_only=False):
    if axis is None:
        out_shape = (N_DEV,) + x.shape
    else:
        out_shape = x.shape[:axis] + (N_DEV * x.shape[axis],) + x.shape[axis + 1:]

    def body(x_full_ref, out_full_ref, send_sems, recv_sems, local_sem):
        mx, my, mc = lax.axis_index("x"), lax.axis_index("y"), lax.axis_index("c")
        me, sibling = (mx, my, mc), (mx, my, 1 - mc)
        chips = [(1 - mx, my), (mx, 1 - my), (1 - mx, 1 - my)]
        x_ref = x_full_ref.at[0] if first_layer_only else x_full_ref
        out_ref = out_full_ref.at[0] if first_layer_only else out_full_ref
        ax = axis - 1 if first_layer_only else axis

        def slot(px, py, pc):
            return _shard_of(out_ref, ax, 4 * px + 2 * py + pc, None if axis is None else x.shape[axis])

        def copy(k, block, to, src=None):
            return pltpu.make_async_remote_copy(
                src_ref=slot(*block) if src is None else src, dst_ref=slot(*block),
                send_sem=send_sems.at[k], recv_sem=recv_sems.at[k], device_id=to, device_id_type=MESH)

        mine = pltpu.make_async_copy(x_ref, slot(*me), local_sem)
        mine.start()
        first = [copy(0, me, sibling, src=x_ref)]
        first += [copy(1 + j, me, (*chip, mc), src=x_ref) for j, chip in enumerate(chips)]
        for cp in first:
            cp.start()
        passed = [copy(4 + j, (*chip, mc), sibling) for j, chip in enumerate(chips)]
        for j, chip in enumerate(chips):
            copy(1 + j, (*chip, mc), me).wait_recv()
            passed[j].start()
        copy(0, sibling, me).wait_recv()
        for j, chip in enumerate(chips):
            copy(4 + j, (*chip, 1 - mc), me).wait_recv()
        for cp in first + passed:
            cp.wait_send()
        mine.wait()

    return pl.pallas_call(
        body, name=name, out_shape=SDS(out_shape, x.dtype), in_specs=[ANY], out_specs=ANY,
        scratch_shapes=[pltpu.SemaphoreType.DMA((7,)), pltpu.SemaphoreType.DMA((7,)), pltpu.SemaphoreType.DMA],
    )(x)


def reduce_adamw(gs, w, m, v, name):
    s, r, c = gs.shape
    tr = _tile(r, max(16, (128 * 1024) // c), 16)

    def body(gs_ref, w_ref, m_ref, v_ref, g_out, d_out, m_out, v_out):
        g = gs_ref[0].astype(F32)
        for j in range(1, s):
            g = g + gs_ref[j].astype(F32)
        m_new = ADAM_B1 * m_ref[...] + (1.0 - ADAM_B1) * g
        v_new = ADAM_B2 * v_ref[...] + (1.0 - ADAM_B2) * (g * g)
        m_hat = m_new / (1.0 - ADAM_B1 ** ADAM_STEP)
        v_hat = v_new / (1.0 - ADAM_B2 ** ADAM_STEP)
        g_out[...] = g
        d_out[...] = -ADAM_LR * (m_hat / (jnp.sqrt(v_hat) + ADAM_EPS) + ADAM_WD * w_ref[...])
        m_out[...] = m_new
        v_out[...] = v_new

    row = pl.BlockSpec((tr, c), lambda i: (i, 0))
    return pl.pallas_call(
        body, name=name, grid=(r // tr,),
        in_specs=[pl.BlockSpec((s, tr, c), lambda i: (0, i, 0)), row, row, row], out_specs=[row] * 4,
        out_shape=[SDS((r, c), F32)] * 4, compiler_params=_cp("parallel"),
    )(gs, w, m, v)


BIG = {
    "w_in_rec": 2, "w_out_rec": 1, "w_qkv": 2, "w_o_attn": 1,
    "w_mlp_up": 2, "w_mlp_down": 1, "w_ple_proj": 2, "w_ple_gate": 1,
}
REPLICATED = ["norm_mix_g", "norm_mlp_g", "norm_ple_g", "norm_f_g", "conv_a_b", "ln_a_g", "ln_a_b", "conv_b_b",
              "w_rg_a", "b_rg_a", "w_rg_x", "b_rg_x", "rg_lambda"]
CONV_W = ["conv_a_w", "conv_b_w"]
WEIGHTS = ["norm_mix_g", "norm_mlp_g", "norm_ple_g", "norm_f_g", "w_in_rec", "conv_a_w", "conv_a_b", "ln_a_g",
           "ln_a_b", "conv_b_w", "conv_b_b", "w_rg_a", "b_rg_a", "w_rg_x", "b_rg_x", "rg_lambda", "w_out_rec",
           "w_qkv", "w_o_attn", "w_mlp_up", "w_mlp_down", "w_ple_proj", "w_ple_gate"]


def _pack(arrays):
    flat = jnp.concatenate([a.reshape(-1) for a in arrays])
    pad = (-flat.shape[0]) % (8 * LANES)
    return jnp.pad(flat, (0, pad)).reshape(-1, LANES)


def _unpack(packed, shapes, lead=()):
    flat = packed.reshape(lead + (-1,))
    out, pos = [], 0
    for shp in shapes:
        size = math.prod(shp)
        out.append(flat[..., pos:pos + size].reshape(lead + tuple(shp)))
        pos += size
    return out


def _step(x, p, loss_target, w, mom, var):
    dev = 4 * lax.axis_index("x") + 2 * lax.axis_index("y") + lax.axis_index("c")
    depth = w["norm_mix_g"].shape[0]
    h = x[0]
    nh = SB_HEADS

    wb = {n: w[n].astype(BF16) for n in BIG}
    wg = {n: None for n in BIG}
    for n in ("w_in_rec", "w_out_rec", "w_mlp_up", "w_mlp_down", "w_ple_proj", "w_ple_gate"):
        wg[n] = all_gather(wb[n], name=f"ag0_{n}", axis=BIG[n], first_layer_only=True)
    mix_names = lambda layer: ("w_in_rec", "w_out_rec") if layer % 2 == 0 else ("w_qkv", "w_o_attn")

    def mm_gathering(targets, *args, **kw):
        ride = Ride()
        for n, layer in targets:
            ride.gather(wb[n], BIG[n], layer, wg[n])
        outs = mm_w(*args, ride=ride if targets else None, **kw)
        for (n, _), a in zip(targets, outs[len(outs) - len(targets):]):
            wg[n] = a
        return outs[:len(outs) - len(targets)]

    conv_shapes = [w[n].shape for n in CONV_W]
    conv_all = all_gather(_pack([w[n] for n in CONV_W]), name="ag_conv_w")
    conv_full = [jnp.moveaxis(a, 0, -2).reshape(a.shape[1:-1] + (-1,))
                 for a in _unpack(conv_all, conv_shapes, lead=(N_DEV,))]
    conv_a_w, conv_b_w = conv_full
    vec = lambda a: a.reshape(1, -1)

    saved = []
    for i in range(depth):
        j = i // 2
        s = {"h0": h}
        in_name, out_name = mix_names(i)
        s["in_name"], s["out_name"] = in_name, out_name
        nxt = i + 1
        if nxt < depth:
            t_in, t_out = [(mix_names(nxt)[0], nxt // 2)], [(mix_names(nxt)[1], nxt // 2), ("w_ple_gate", nxt)]
            t_up, t_down, t_gate = [("w_mlp_down", nxt)], [("w_mlp_up", nxt)], [("w_ple_proj", nxt)]
        else:
            t_in = t_out = t_up = t_down = t_gate = []
        s["hn1"] = rms_fwd(h, w["norm_mix_g"][i], name="rms_fwd")
        if i % 2 == 0:
            (s["u"],) = mm_gathering(t_in, s["hn1"], wg[in_name], j, transpose_w=False, out_dtypes=[F32],
                                     name="mm_in_rec")
            ya, s["ya1"] = conf_fwd(s["u"], conv_a_w[j], vec(w["conv_a_b"][j]), vec(w["ln_a_g"][j]),
                                    vec(w["ln_a_b"][j]), name="conf_fwd")
            yb, s["hs"] = rglru_fwd(s["u"], conv_b_w[j], vec(w["conv_b_b"][j]), w["w_rg_a"][j].astype(BF16),
                                    vec(w["b_rg_a"][j]), w["w_rg_x"][j].astype(BF16), vec(w["b_rg_x"][j]),
                                    vec(w["rg_lambda"][j]), name="rglru_fwd")
            s["mix_in"] = jnp.concatenate([ya, yb], axis=1)
        else:
            (s["qkv"],) = mm_gathering(t_in, s["hn1"], wg[in_name], j, transpose_w=False, out_dtypes=[BF16],
                                       name="mm_qkv")
            s["mix_in"], s["s_tot"], s["start"] = attn_fwd(s["qkv"], nh, name="attn_fwd")
        (h,) = mm_gathering(t_out, s["mix_in"], wg[out_name], j, transpose_w=False, out_dtypes=[F32],
                            extras=[h], epilogue=lambda acc, res: (res + acc,), name="mm_mix_out")
        s["h1"] = h
        s["hn2"] = rms_fwd(h, w["norm_mlp_g"][i], name="rms_fwd")
        relu = lambda acc: jnp.maximum(acc, 0.0)
        s["up"], s["act"] = mm_gathering(t_up, s["hn2"], wg["w_mlp_up"], i, transpose_w=False,
                                         out_dtypes=[BF16, BF16], name="mm_mlp_up", tm=1024,
                                         epilogue=lambda acc: (acc, relu(acc) * relu(acc)))
        (h,) = mm_gathering(t_down, s["act"], wg["w_mlp_down"], i, transpose_w=False, out_dtypes=[F32],
                            tm=1024, tk=1024, extras=[h], epilogue=lambda acc, res: (res + acc,),
                            name="mm_mlp_down")
        s["h2"] = h
        s["hn3"] = rms_fwd(h, w["norm_ple_g"][i], name="rms_fwd")
        s["p"] = p[i, 0].astype(BF16)
        (s["pp"],) = mm_w(s["p"], wg["w_ple_proj"], i, transpose_w=False, out_dtypes=[F32], name="mm_ple_proj")
        h, s["gz"] = mm_gathering(t_gate, s["hn3"], wg["w_ple_gate"], i, transpose_w=False,
                                  out_dtypes=[F32, BF16], extras=[h, s["pp"]], name="mm_ple_gate",
                                  epilogue=lambda acc, res, pp: (res + pp * jax.nn.sigmoid(acc), acc))
        saved.append(s)

    dh, dg_f, loss_part = loss_head(h, w["norm_f_g"], loss_target[0], name="loss_head")
    loss = lax.psum(loss_part[0, 0], ("x", "y", "c"))

    acc = {n: None for n in BIG}
    part = {n: [None] * w[n].shape[0] for n in REPLICATED + CONV_W if n != "norm_f_g"}
    part["norm_f_g"] = dg_f[0]

    recv = {n: None for n in BIG}

    def exchanging(targets):
        ride = Ride()
        for n, layer in targets:
            ride.exchange(acc[n], BIG[n], layer, recv[n])
        return ride

    def landed(targets, arrays):
        for (n, _), a in zip(targets, arrays):
            recv[n] = a

    def dw(name, a, b, layer, targets=()):
        outs = mm_dw(a, b, acc[name], layer, wg[name].shape[0], name=f"dw_{name}",
                     ride=exchanging(targets) if targets else None)
        acc[name] = outs[0] if targets else outs
        landed(targets, outs[1:] if targets else [])

    def mm_exchanging(targets, *args, **kw):
        outs = mm_w(*args, ride=exchanging(targets), **kw)
        landed(targets, outs[len(outs) - len(targets):])
        return outs[:len(outs) - len(targets)]

    for i in reversed(range(depth)):
        j = i // 2
        s = saved[i]
        in_name, out_name = s["in_name"], s["out_name"]
        d_pp, d_gz = ple_bwd(dh, s["gz"], s["pp"], name="ple_bwd")
        dw("w_ple_proj", s["p"], d_pp, i)
        dw("w_ple_gate", s["hn3"], d_gz, i)
        (d_hn3,) = mm_exchanging([("w_ple_proj", i), ("w_ple_gate", i)], d_gz, wg["w_ple_gate"], i,
                                 transpose_w=True, out_dtypes=[BF16], name="mmT_ple_gate")
        dh, dh_b, dg = rms_bwd(s["h2"], w["norm_ple_g"][i], d_hn3, dh, name="rms_bwd")
        part["norm_ple_g"][i] = dg[0]
        (d_up,) = mm_w(dh_b, wg["w_mlp_down"], i, transpose_w=True, out_dtypes=[BF16], tm=1024,
                       extras=[s["up"]], name="mmT_mlp_down",
                       epilogue=lambda acc_, up: (acc_ * (2.0 * jnp.maximum(up.astype(F32), 0.0)),))
        dw("w_mlp_down", s["act"], dh_b, i)
        dw("w_mlp_up", s["hn2"], d_up, i, targets=[("w_mlp_down", i)])
        (d_hn2,) = mm_exchanging([("w_mlp_up", i)], d_up, wg["w_mlp_up"], i, transpose_w=True,
                                 out_dtypes=[BF16], tm=1024, name="mmT_mlp_up")
        dh, dh_b, dg = rms_bwd(s["h1"], w["norm_mlp_g"][i], d_hn2, dh, name="rms_bwd")
        part["norm_mlp_g"][i] = dg[0]
        dw(out_name, s["mix_in"], dh_b, j)
        if i % 2 == 0:
            (d_cat,) = mm_exchanging([(out_name, j)], dh_b, wg[out_name], j, transpose_w=True,
                                     out_dtypes=[BF16], name="mmT_mix_out")
            du_a, pg_a = conf_bwd(s["u"], s["ya1"], d_cat, conv_a_w[j], vec(w["ln_a_g"][j]), vec(w["ln_a_b"][j]),
                                  name="conf_bwd")
            du_b, pg_b, d_wa, d_wx = rglru_bwd(
                s["u"], s["hs"], d_cat, conv_b_w[j], vec(w["conv_b_b"][j]), w["w_rg_a"][j].astype(BF16),
                vec(w["b_rg_a"][j]), w["w_rg_x"][j].astype(BF16), vec(w["b_rg_x"][j]), vec(w["rg_lambda"][j]),
                name="rglru_bwd")
            ka, kb = conv_a_w.shape[1], conv_b_w.shape[1]
            part["conv_a_w"][j], part["conv_a_b"][j] = pg_a[:ka], pg_a[ka]
            part["ln_a_g"][j], part["ln_a_b"][j] = pg_a[ka + 1], pg_a[ka + 2]
            part["conv_b_w"][j], part["conv_b_b"][j] = pg_b[:kb], pg_b[kb]
            part["b_rg_a"][j], part["b_rg_x"][j], part["rg_lambda"][j] = pg_b[kb + 1], pg_b[kb + 2], pg_b[kb + 3]
            part["w_rg_a"][j], part["w_rg_x"][j] = d_wa, d_wx
            d_mix = jnp.concatenate([du_a, du_b], axis=1)
        else:
            (d_o,) = mm_exchanging([(out_name, j)], dh_b, wg[out_name], j, transpose_w=True,
                                   out_dtypes=[BF16], name="mmT_mix_out")
            dq, dk, dv = attn_bwd(s["qkv"], d_o, s["s_tot"], s["start"], nh, name="attn_bwd")
            d_mix = jnp.concatenate([dq, dk.astype(BF16), dv.astype(BF16)], axis=1)
        dw(in_name, s["hn1"], d_mix, j)
        (d_hn1,) = mm_exchanging([(in_name, j)], d_mix, wg[in_name], j, transpose_w=True, out_dtypes=[BF16],
                                 name="mmT_mix_in")
        dh, _, dg = rms_bwd(s["h0"], w["norm_mix_g"][i], d_hn1, dh, name="rms_bwd")
        part["norm_mix_g"][i] = dg[0]

    grads, deltas, new_m, new_v = {}, {}, {}, {}

    def finish(name, outs, shape):
        for d, o in zip((grads, deltas, new_m, new_v), outs):
            d[name] = o.reshape(shape)

    for n in BIG:
        _, nl, r, c = recv[n].shape
        flat = lambda a: a.reshape(nl * r, c)
        finish(n, reduce_adamw(recv[n].reshape(N_DEV, nl * r, c), flat(w[n]), flat(mom[n]), flat(var[n]),
                               name=f"adamw_{n}"), w[n].shape)

    full = {n: (part[n] if n == "norm_f_g" else jnp.stack(part[n])) for n in REPLICATED + CONV_W}
    rep = _pack([full[n] for n in REPLICATED])
    conv = _pack([full[n] for n in CONV_W])
    small = all_gather(jnp.concatenate([rep, conv]), name="ag_small_grads")
    rep_shapes = [w[n].shape for n in REPLICATED]
    outs = reduce_adamw(small[:, :rep.shape[0]], _pack([w[n] for n in REPLICATED]),
                        _pack([mom[n] for n in REPLICATED]), _pack([var[n] for n in REPLICATED]), name="adamw_small")
    for d, o in zip((grads, deltas, new_m, new_v), outs):
        for n, a in zip(REPLICATED, _unpack(o, rep_shapes)):
            d[n] = a
    conv_parts = _unpack(small[:, rep.shape[0]:], [full[n].shape for n in CONV_W], lead=(N_DEV,))
    width = w["conv_a_w"].shape[-1]
    mine = [lax.dynamic_slice_in_dim(a, dev * width, width, axis=a.ndim - 1) for a in conv_parts]
    packed = jnp.stack([_pack([a[d] for a in mine]) for d in range(N_DEV)])
    outs = reduce_adamw(packed, _pack([w[n] for n in CONV_W]), _pack([mom[n] for n in CONV_W]),
                        _pack([var[n] for n in CONV_W]), name="adamw_conv_w")
    for d, o in zip((grads, deltas, new_m, new_v), outs):
        for n, a in zip(CONV_W, _unpack(o, conv_shapes)):
            d[n] = a
    return loss, dh[None], grads, deltas, new_m, new_v


def kernel(x, p, norm_mix_g, norm_mlp_g, norm_ple_g, norm_f_g, w_in_rec, conv_a_w, conv_a_b, ln_a_g, ln_a_b, conv_b_w, conv_b_b, w_rg_a, b_rg_a, w_rg_x, b_rg_x, rg_lambda, w_out_rec, w_qkv, w_o_attn, w_mlp_up, w_mlp_down, w_ple_proj, w_ple_gate, loss_target, m_norm_mix_g, m_norm_mlp_g, m_norm_ple_g, m_norm_f_g, m_w_in_rec, m_conv_a_w, m_conv_a_b, m_ln_a_g, m_ln_a_b, m_conv_b_w, m_conv_b_b, m_w_rg_a, m_b_rg_a, m_w_rg_x, m_b_rg_x, m_rg_lambda, m_w_out_rec, m_w_qkv, m_w_o_attn, m_w_mlp_up, m_w_mlp_down, m_w_ple_proj, m_w_ple_gate, v_norm_mix_g, v_norm_mlp_g, v_norm_ple_g, v_norm_f_g, v_w_in_rec, v_conv_a_w, v_conv_a_b, v_ln_a_g, v_ln_a_b, v_conv_b_w, v_conv_b_b, v_w_rg_a, v_b_rg_a, v_w_rg_x, v_b_rg_x, v_rg_lambda, v_w_out_rec, v_w_qkv, v_w_o_attn, v_w_mlp_up, v_w_mlp_down, v_w_ple_proj, v_w_ple_gate):
    given = dict(locals())
    w = {n: given[n] for n in WEIGHTS}
    mom = {n: given["m_" + n] for n in WEIGHTS}
    var = {n: given["v_" + n] for n in WEIGHTS}
    loss, grad_x, grads, deltas, new_m, new_v = _step(x, p, loss_target, w, mom, var)
    return (loss, grad_x, *[grads[n] for n in WEIGHTS], *[deltas[n] for n in WEIGHTS],
            *[new_m[n] for n in WEIGHTS], *[new_v[n] for n in WEIGHTS])
```

```python
import functools
import math

import jax
import jax.numpy as jnp
from jax import lax
from jax.experimental import pallas as pl
from jax.experimental.pallas import tpu as pltpu

F32, BF16 = jnp.float32, jnp.bfloat16
EPS = 1e-6
N_DEV = 8
SB_HEADS = 16
RG_C = 8.0
HALO_A = 32
HALO_B = 8
LANES = 128
VMEM_LIMIT = 48 * 1024 * 1024
ADAM_LR, ADAM_B1, ADAM_B2, ADAM_EPS, ADAM_WD, ADAM_STEP = 0.001, 0.9, 0.999, 1e-08, 0.01, 10
MESH = pl.DeviceIdType.MESH
SDS = jax.ShapeDtypeStruct
ANY = pl.BlockSpec(memory_space=pl.ANY)


def _cp(*sem):
    return pltpu.CompilerParams(dimension_semantics=sem, vmem_limit_bytes=VMEM_LIMIT)


def _tile(dim, pref, align=LANES):
    if dim <= pref:
        return dim
    t = (pref // align) * align
    while t >= align:
        if dim % t == 0:
            return t
        t -= align
    return dim


def _softplus(z):
    return jnp.maximum(z, 0.0) + jnp.log(1.0 + jnp.exp(-jnp.abs(z)))


def _expm1(x):
    t = x * (1.0 + x * (0.5 + x * (1.0 / 6.0 + x * (1.0 / 24.0 + x * (1.0 / 120.0)))))
    return jnp.where(jnp.abs(x) < 0.1, t, jnp.exp(x) - 1.0)


_GELU_C = math.sqrt(2.0 / math.pi)


def _gelu(x):
    return 0.5 * x * (1.0 + jnp.tanh(_GELU_C * (x + 0.044715 * x * x * x)))


def _gelu_grad(x):
    th = jnp.tanh(_GELU_C * (x + 0.044715 * x * x * x))
    return 0.5 * (1.0 + th) + 0.5 * x * (1.0 - th * th) * _GELU_C * (1.0 + 3.0 * 0.044715 * x * x)


def _dot(a, b):
    return jnp.dot(a, b, preferred_element_type=F32)


def _dot_nt(a, b):
    return lax.dot_general(a, b, (((1,), (1,)), ((), ())), preferred_element_type=F32)


def _dot_tn(a, b):
    return lax.dot_general(a, b, (((0,), (0,)), ((), ())), preferred_element_type=F32)


def _shard_of(ref, axis, dev, size):
    if axis is None:
        return ref.at[dev]
    return ref.at[(slice(None),) * axis + (pl.ds(pl.multiple_of(dev * size, size), size),)]


def _peer(k, mx, my, mc):
    return (1 - mx if k & 4 else mx, 1 - my if k & 2 else my, 1 - mc if k & 1 else mc)


class Ride:
    def __init__(self):
        self.items = []

    def gather(self, shard, axis, layer, whole_prev):
        shape = shard.shape[:axis] + (N_DEV * shard.shape[axis],) + shard.shape[axis + 1:]
        self.items.append(("gather", shard, whole_prev, SDS(shape, shard.dtype), axis, layer, shard.shape[axis]))
        return self

    def exchange(self, grad, axis, layer, recv_prev):
        size = grad.shape[axis] // N_DEV
        shape = (N_DEV,) + grad.shape[:axis] + (size,) + grad.shape[axis + 1:]
        self.items.append(("exchange", grad, recv_prev, SDS(shape, grad.dtype), axis, layer, size))
        return self

    def operands(self):
        return [a for it in self.items for a in ([it[1]] if it[2] is None else [it[1], it[2]])]

    def out_shapes(self):
        return [it[3] for it in self.items]

    def aliases(self, first_in, first_out):
        out, pos = {}, first_in
        for t, it in enumerate(self.items):
            pos += 1
            if it[2] is not None:
                out[pos] = first_out + t
                pos += 1
        return out

    def scratch(self):
        n = len(self.items)
        return [pltpu.SemaphoreType.DMA((n, N_DEV - 1)), pltpu.SemaphoreType.DMA((n, N_DEV - 1)),
                pltpu.SemaphoreType.DMA((n,))]

    def plan(self, in_refs, out_refs, send_sems, recv_sems, local_sems):
        mx, my, mc = lax.axis_index("x"), lax.axis_index("y"), lax.axis_index("c")
        me, sibling = (mx, my, mc), (mx, my, 1 - mc)
        lin = lambda d: 4 * d[0] + 2 * d[1] + d[2]
        first, middle, last, pos = [], [], [], 0
        for t, (kind, _, prev, _, axis, layer, size) in enumerate(self.items):
            src = in_refs[pos].at[layer]
            pos += 1 if prev is None else 2

            def remote(act, k, src_fn, dst_fn, to, t=t):
                return lambda: getattr(pltpu.make_async_remote_copy(
                    src_ref=src_fn(), dst_ref=dst_fn(), send_sem=send_sems.at[t, k], recv_sem=recv_sems.at[t, k],
                    device_id=to, device_id_type=MESH), act)()

            def local(act, src_fn, dst_fn, t=t):
                return lambda: getattr(pltpu.make_async_copy(src_fn(), dst_fn(), local_sems.at[t]), act)()

            src_fn = lambda src=src: src
            if kind == "exchange":
                dst_fn = lambda t=t, layer=layer: out_refs[t].at[lin(me), layer]
                shard = lambda d, src=src, axis=axis, size=size: (lambda: _shard_of(src, axis - 1, lin(d), size))
                for act, phase in (("start", first), ("wait", last)):
                    phase.append(local(act, shard(me), dst_fn))
                    for k in range(1, N_DEV):
                        peer = _peer(k, mx, my, mc)
                        phase.append(remote(act, k - 1, shard(peer), dst_fn, peer))
                continue
            whole = out_refs[t].at[layer]
            slot = lambda d, whole=whole, axis=axis, size=size: (lambda: _shard_of(whole, axis - 1, lin(d), size))
            chips = [(1 - mx, my), (mx, 1 - my), (1 - mx, 1 - my)]
            own = [(0, sibling)] + [(1 + j, (*chip, mc)) for j, chip in enumerate(chips)]
            first.append(local("start", src_fn, slot(me)))
            first += [remote("start", k, src_fn, slot(me), to) for k, to in own]
            for j, chip in enumerate(chips):
                landed = slot((*chip, mc))
                middle += [remote("wait_recv", 1 + j, src_fn, landed, me), remote("start", 4 + j, landed, landed, sibling)]
                last.append(remote("wait_send", 4 + j, landed, landed, sibling))
            last.append(remote("wait_recv", 0, src_fn, slot(sibling), me))
            last += [remote("wait_recv", 4 + j, src_fn, slot((*chip, 1 - mc)), me) for j, chip in enumerate(chips)]
            last += [remote("wait_send", k, src_fn, slot(me), to) for k, to in own]
            last.append(local("wait", src_fn, slot(me)))
        return first, middle, last


def _ride_hooks(ride, grid, in_refs, out_refs, sems):
    step, total = 0, 1
    for d, g in enumerate(grid):
        step, total = step * g + pl.program_id(d), total * g

    def at(which, when):
        @pl.when(step == when)
        def _():
            for act in ride.plan(in_refs, out_refs, *sems)[which]:
                act()

    def start():
        at(0, 0)
        at(1, total // 2)

    def finish():
        at(2, total - 1)

    return start, finish


def hosted_call(body, ride, *, name, grid, in_specs, out_specs, out_shape, scratch_shapes, operands):
    if not ride:
        return pl.pallas_call(
            body, name=name, grid=grid, in_specs=in_specs, out_specs=out_specs, out_shape=out_shape,
            scratch_shapes=scratch_shapes, compiler_params=_cp(*["arbitrary"] * len(grid)))(*operands)
    n_in, n_out, n_scratch = len(in_specs), len(out_specs), len(scratch_shapes)
    ride_in = ride.operands()

    def carrying(*refs):
        ins, refs = refs[:n_in], refs[n_in:]
        ride_in_refs, refs = refs[:len(ride_in)], refs[len(ride_in):]
        outs, refs = refs[:n_out], refs[n_out:]
        ride_out_refs, refs = refs[:len(ride.items)], refs[len(ride.items):]
        start, finish = _ride_hooks(ride, grid, ride_in_refs, ride_out_refs, refs[n_scratch:])
        start()
        body(*ins, *outs, *refs[:n_scratch])
        finish()

    return pl.pallas_call(
        carrying, name=name, grid=grid, in_specs=list(in_specs) + [ANY] * len(ride_in),
        out_specs=list(out_specs) + [ANY] * len(ride.items), out_shape=list(out_shape) + ride.out_shapes(),
        scratch_shapes=list(scratch_shapes) + ride.scratch(), input_output_aliases=ride.aliases(n_in, n_out),
        compiler_params=_cp(*["arbitrary"] * len(grid)))(*operands, *ride_in)


def mm_w(a, w, layer, *, transpose_w, out_dtypes, name, extras=(), epilogue=None, ride=None,
         tm=512, tn=1024, tk=2048):
    m, k_dim = a.shape
    _, rows, cols = w.shape
    n = rows if transpose_w else cols
    assert k_dim == (cols if transpose_w else rows), (a.shape, w.shape)
    tm, tn, tk = _tile(m, tm, 8), _tile(n, tn), _tile(k_dim, tk)
    nk = k_dim // tk
    grid = (m // tm, n // tn, nk)
    n_extra, n_out = len(extras), len(out_dtypes)
    ride_in = ride.operands() if ride else []
    n_ride_out = len(ride.items) if ride else 0
    if epilogue is None:
        epilogue = lambda acc: (acc,)
    dot = _dot_nt if transpose_w else _dot

    def body(a_ref, w_ref, *rest):
        extra_refs, rest = rest[:n_extra], rest[n_extra:]
        ride_in_refs, rest = rest[:len(ride_in)], rest[len(ride_in):]
        out_refs, rest = rest[:n_out], rest[n_out:]
        ride_out_refs, scratch = rest[:n_ride_out], rest[n_ride_out:]
        if ride:
            start, finish = _ride_hooks(ride, grid, ride_in_refs, ride_out_refs, scratch[-3:])
            start()

        def write(acc):
            outs = epilogue(acc, *[r[...] for r in extra_refs])
            for o_ref, o in zip(out_refs, outs):
                o_ref[...] = o.astype(o_ref.dtype)

        if nk == 1:
            write(dot(a_ref[...], w_ref[...]))
        else:
            acc = scratch[0]
            k = pl.program_id(2)

            @pl.when(k == 0)
            def _():
                acc[...] = jnp.zeros_like(acc)

            acc[...] += dot(a_ref[...], w_ref[...])

            @pl.when(k == nk - 1)
            def _():
                write(acc[...])

        if ride:
            finish()

    if transpose_w:
        w_spec = pl.BlockSpec((None, tn, tk), lambda i, j, k: (layer, j, k))
    else:
        w_spec = pl.BlockSpec((None, tk, tn), lambda i, j, k: (layer, k, j))
    tile_spec = pl.BlockSpec((tm, tn), lambda i, j, k: (i, j))
    return pl.pallas_call(
        body, name=name, grid=grid,
        in_specs=[pl.BlockSpec((tm, tk), lambda i, j, k: (i, k)), w_spec] + [tile_spec] * n_extra
        + [ANY] * len(ride_in),
        out_specs=[tile_spec] * n_out + [ANY] * n_ride_out,
        out_shape=[SDS((m, n), dt) for dt in out_dtypes] + (ride.out_shapes() if ride else []),
        scratch_shapes=([pltpu.VMEM((tm, tn), F32)] if nk > 1 else []) + (ride.scratch() if ride else []),
        input_output_aliases=ride.aliases(2 + n_extra, n_out) if ride else {},
        compiler_params=_cp("arbitrary", "arbitrary", "arbitrary") if ride
        else _cp("parallel", "parallel", "arbitrary"),
    )(a, w, *extras, *ride_in)


def mm_dw(a, b, buf, layer, n_layers, name, ride=None, tm=512, tn=1024, tk=2048):
    tokens, m = a.shape
    _, n = b.shape
    tm, tn, tk = _tile(m, tm), _tile(n, tn), _tile(tokens, tk)
    nk = tokens // tk
    grid = (m // tm, n // tn, nk)
    n_buf = 0 if buf is None else 1
    ride_in = ride.operands() if ride else []
    n_ride_out = len(ride.items) if ride else 0

    def body(a_ref, b_ref, *rest):
        rest = rest[n_buf:]
        ride_in_refs, rest = rest[:len(ride_in)], rest[len(ride_in):]
        o_ref, rest = rest[0], rest[1:]
        ride_out_refs, scratch = rest[:n_ride_out], rest[n_ride_out:]
        acc = scratch[0]
        if ride:
            start, finish = _ride_hooks(ride, grid, ride_in_refs, ride_out_refs, scratch[-3:])
            start()
        k = pl.program_id(2)

        @pl.when(k == 0)
        def _():
            acc[...] = jnp.zeros_like(acc)

        acc[...] += _dot_tn(a_ref[...], b_ref[...])

        @pl.when(k == nk - 1)
        def _():
            o_ref[...] = acc[...].astype(BF16)

        if ride:
            finish()

    aliases = {} if buf is None else {2: 0}
    if ride:
        aliases.update(ride.aliases(2 + n_buf, 1))
    outs = pl.pallas_call(
        body, name=name, grid=grid,
        in_specs=[pl.BlockSpec((tk, tm), lambda i, j, k: (k, i)),
                  pl.BlockSpec((tk, tn), lambda i, j, k: (k, j))] + [ANY] * (n_buf + len(ride_in)),
        out_specs=[pl.BlockSpec((None, tm, tn), lambda i, j, k: (layer, i, j))] + [ANY] * n_ride_out,
        out_shape=[SDS((n_layers, m, n), BF16)] + (ride.out_shapes() if ride else []),
        scratch_shapes=[pltpu.VMEM((tm, tn), F32)] + (ride.scratch() if ride else []),
        input_output_aliases=aliases,
        compiler_params=_cp("arbitrary", "arbitrary", "arbitrary") if ride
        else _cp("parallel", "parallel", "arbitrary"),
    )(a, b, *([] if buf is None else [buf]), *ride_in)
    return outs if ride else outs[0]


def rms_fwd(h, g, name):
    t, d = h.shape
    tt = _tile(t, 512, 8)

    def body(h_ref, g_ref, o_ref):
        x = h_ref[...]
        r = lax.rsqrt(jnp.mean(x * x, axis=-1, keepdims=True) + EPS)
        o_ref[...] = (x * r * g_ref[...]).astype(BF16)

    row = pl.BlockSpec((tt, d), lambda i: (i, 0))
    return pl.pallas_call(
        body, name=name, grid=(t // tt,),
        in_specs=[row, pl.BlockSpec((1, d), lambda i: (0, 0))], out_specs=row,
        out_shape=SDS((t, d), BF16), compiler_params=_cp("parallel"),
    )(h, g.reshape(1, d))


def _rms_bwd_math(x, g, dhn):
    r = lax.rsqrt(jnp.mean(x * x, axis=-1, keepdims=True) + EPS)
    xn = x * r
    dxn = dhn * g
    dx = r * (dxn - xn * jnp.mean(dxn * xn, axis=-1, keepdims=True))
    return dx, jnp.sum(dhn * xn, axis=0, keepdims=True)


def rms_bwd(h, g, dhn, dres, name):
    t, d = h.shape
    tt = _tile(t, 256, 8)

    def body(h_ref, g_ref, dhn_ref, dres_ref, dh_ref, dhb_ref, dg_ref):
        @pl.when(pl.program_id(0) == 0)
        def _():
            dg_ref[...] = jnp.zeros_like(dg_ref)

        dx, dg = _rms_bwd_math(h_ref[...], g_ref[...], dhn_ref[...].astype(F32))
        dh = dres_ref[...] + dx
        dh_ref[...] = dh
        dhb_ref[...] = dh.astype(BF16)
        dg_ref[...] += dg

    row = pl.BlockSpec((tt, d), lambda i: (i, 0))
    vec = pl.BlockSpec((1, d), lambda i: (0, 0))
    return pl.pallas_call(
        body, name=name, grid=(t // tt,),
        in_specs=[row, vec, row, row], out_specs=[row, row, vec],
        out_shape=[SDS((t, d), F32), SDS((t, d), BF16), SDS((1, d), F32)],
        compiler_params=_cp("arbitrary"),
    )(h, g.reshape(1, d), dhn, dres)


def loss_head(h, g, target, name):
    t, d = h.shape
    tt = _tile(t, 256, 8)

    def body(h_ref, g_ref, t_ref, dh_ref, dg_ref, loss_ref):
        @pl.when(pl.program_id(0) == 0)
        def _():
            dg_ref[...] = jnp.zeros_like(dg_ref)
            loss_ref[...] = jnp.zeros_like(loss_ref)

        x, gain = h_ref[...], g_ref[...]
        r = lax.rsqrt(jnp.mean(x * x, axis=-1, keepdims=True) + EPS)
        err = x * r * gain - t_ref[...]
        loss_ref[...] += 0.5 * jnp.sum(jnp.mean(err * err, axis=-1, keepdims=True))
        dx, dg = _rms_bwd_math(x, gain, err * (1.0 / d))
        dh_ref[...] = dx
        dg_ref[...] += dg

    row = pl.BlockSpec((tt, d), lambda i: (i, 0))
    vec = pl.BlockSpec((1, d), lambda i: (0, 0))
    return pl.pallas_call(
        body, name=name, grid=(t // tt,),
        in_specs=[row, vec, row],
        out_specs=[row, vec, pl.BlockSpec((8, LANES), lambda i: (0, 0))],
        out_shape=[SDS((t, d), F32), SDS((1, d), F32), SDS((8, LANES), F32)],
        compiler_params=_cp("arbitrary"),
    )(h, g.reshape(1, d), target)


def ple_bwd(dh, gz, pp, name):
    t, d = dh.shape
    tt = _tile(t, 256, 8)

    def body(dh_ref, gz_ref, pp_ref, dpp_ref, dgz_ref):
        g = dh_ref[...]
        gate = jax.nn.sigmoid(gz_ref[...].astype(F32))
        dpp_ref[...] = (g * gate).astype(BF16)
        dgz_ref[...] = (g * pp_ref[...].astype(F32) * gate * (1.0 - gate)).astype(BF16)

    row = pl.BlockSpec((tt, d), lambda i: (i, 0))
    return pl.pallas_call(
        body, name=name, grid=(t // tt,), in_specs=[row, row, row], out_specs=[row, row],
        out_shape=[SDS((t, d), BF16), SDS((t, d), BF16)], compiler_params=_cp("parallel"),
    )(dh, gz, pp)


def _layer_norm_parts(y1):
    mu = jnp.mean(y1, axis=-1, keepdims=True)
    dlt = y1 - mu
    rstd = lax.rsqrt(jnp.mean(dlt * dlt, axis=-1, keepdims=True) + EPS)
    return dlt * rstd, rstd


def _fill_shifted(buf, shifted, tt):
    for r in range(1, 8):
        shifted[r - 1] = buf[pl.ds(r, tt + HALO_A - 8), :]


def _rows_from(buf, shifted, start, tt, cs):
    q, r = divmod(start, 8)
    if r == 0:
        return buf[pl.ds(8 * q, tt), cs]
    return shifted[r - 1, pl.ds(8 * q, tt), cs]


def conf_fwd(u, cw, cb, lg, lb, name):
    t = u.shape[0]
    ka, c = cw.shape
    tt = _tile(t, 256, HALO_A)
    hb = tt // HALO_A
    off = HALO_A - (ka - 1)

    def body(av_ref, ag_ref, pv_ref, pg_ref, cw_ref, cb_ref, lg_ref, lb_ref, ya_ref, ya1_ref, buf, shifted):
        live = (pl.program_id(0) > 0).astype(F32)
        buf[0:HALO_A, :] = pv_ref[...] * jax.nn.sigmoid(pg_ref[...]) * live
        buf[HALO_A:, :] = av_ref[...] * jax.nn.sigmoid(ag_ref[...])
        _fill_shifted(buf, shifted, tt)
        for c0 in range(0, c, LANES):
            cs = pl.ds(c0, LANES)
            acc = jnp.broadcast_to(cb_ref[:, cs], (tt, LANES))
            for k in range(ka):
                acc = acc + cw_ref[k:k + 1, cs] * _rows_from(buf, shifted, off + k, tt, cs)
            ya1_ref[:, cs] = acc
        yn, _ = _layer_norm_parts(ya1_ref[...])
        y2 = yn * lg_ref[...] + lb_ref[...]
        ya_ref[...] = (y2 * jax.nn.sigmoid(y2)).astype(BF16)

    cur = lambda col: pl.BlockSpec((tt, c), lambda i: (i, col))
    prev = lambda col: pl.BlockSpec((HALO_A, c), lambda i: (jnp.maximum(i * hb - 1, 0), col))
    vec = pl.BlockSpec((1, c), lambda i: (0, 0))
    return pl.pallas_call(
        body, name=name, grid=(t // tt,),
        in_specs=[cur(0), cur(1), prev(0), prev(1), pl.BlockSpec((ka, c), lambda i: (0, 0)), vec, vec, vec],
        out_specs=[pl.BlockSpec((tt, c), lambda i: (i, 0))] * 2,
        out_shape=[SDS((t, c), BF16), SDS((t, c), F32)],
        scratch_shapes=[pltpu.VMEM((tt + HALO_A, c), F32), pltpu.VMEM((7, tt + HALO_A - 8, c), F32)],
        compiler_params=_cp("parallel"),
    )(u, u, u, u, cw, cb, lg, lb)


def conf_bwd(u, ya1, dcat, cw, lg, lb, name, ride=None):
    t = u.shape[0]
    ka, c = cw.shape
    tt = _tile(t, 256, HALO_A)
    nt, hb = t // tt, tt // HALO_A
    off = HALO_A - (ka - 1)

    def body(av_ref, ag_ref, pv_ref, pg_ref, y1_ref, dya_ref, cw_ref, lg_ref, lb_ref,
             du_ref, pgrad_ref, ybuf, dbuf, carry, yshift, dshift):
        i = pl.program_id(0)

        @pl.when(i == 0)
        def _():
            carry[...] = jnp.zeros_like(carry)
            pgrad_ref[...] = jnp.zeros_like(pgrad_ref)

        live = (i < nt - 1).astype(F32)
        av = av_ref[...]
        sg = jax.nn.sigmoid(ag_ref[...])
        ybuf[0:HALO_A, :] = pv_ref[...] * jax.nn.sigmoid(pg_ref[...]) * live
        ybuf[HALO_A:, :] = av * sg
        yn, rstd = _layer_norm_parts(y1_ref[...])
        gain = lg_ref[...]
        y2 = yn * gain + lb_ref[...]
        s2 = jax.nn.sigmoid(y2)
        dy2 = dya_ref[...].astype(F32) * (s2 * (1.0 + y2 * (1.0 - s2)))
        pgrad_ref[ka + 1:ka + 2, :] += jnp.sum(dy2 * yn, axis=0, keepdims=True)
        pgrad_ref[ka + 2:ka + 3, :] += jnp.sum(dy2, axis=0, keepdims=True)
        dyn = dy2 * gain
        dy1 = rstd * (dyn - jnp.mean(dyn, axis=-1, keepdims=True)
                      - yn * jnp.mean(dyn * yn, axis=-1, keepdims=True))
        pgrad_ref[ka:ka + 1, :] += jnp.sum(dy1, axis=0, keepdims=True)
        dbuf[0:tt, :] = dy1
        dbuf[tt:, :] = carry[...]
        carry[...] = dbuf[0:HALO_A, :]
        _fill_shifted(ybuf, yshift, tt)
        _fill_shifted(dbuf, dshift, tt)
        for c0 in range(0, c, LANES):
            cs = pl.ds(c0, LANES)
            d_cur = dbuf[0:tt, cs]
            acc = jnp.zeros((tt, LANES), F32)
            for k in range(ka):
                acc = acc + cw_ref[k:k + 1, cs] * _rows_from(dbuf, dshift, ka - 1 - k, tt, cs)
                pgrad_ref[k:k + 1, cs] += jnp.sum(d_cur * _rows_from(ybuf, yshift, off + k, tt, cs),
                                                  axis=0, keepdims=True)
            sgc, avc = sg[:, c0:c0 + LANES], av[:, c0:c0 + LANES]
            du_ref[:, cs] = (acc * sgc).astype(BF16)
            du_ref[:, pl.ds(c + c0, LANES)] = (acc * avc * sgc * (1.0 - sgc)).astype(BF16)

    cur = lambda col: pl.BlockSpec((tt, c), lambda i: (nt - 1 - i, col))
    prev = lambda col: pl.BlockSpec((HALO_A, c), lambda i: (jnp.maximum((nt - 1 - i) * hb - 1, 0), col))
    vec = pl.BlockSpec((1, c), lambda i: (0, 0))
    return hosted_call(
        body, ride, name=name, grid=(nt,),
        in_specs=[cur(0), cur(1), prev(0), prev(1), cur(0), cur(0),
                  pl.BlockSpec((ka, c), lambda i: (0, 0)), vec, vec],
        out_specs=[pl.BlockSpec((tt, 2 * c), lambda i: (nt - 1 - i, 0)),
                   pl.BlockSpec((ka + 3, c), lambda i: (0, 0))],
        out_shape=[SDS((t, 2 * c), BF16), SDS((ka + 3, c), F32)],
        scratch_shapes=[pltpu.VMEM((tt + HALO_A, c), F32), pltpu.VMEM((tt + HALO_A, c), F32),
                        pltpu.VMEM((HALO_A, c), F32), pltpu.VMEM((7, tt + HALO_A - 8, c), F32),
                        pltpu.VMEM((7, tt + HALO_A - 8, c), F32)],
        operands=(u, u, u, u, ya1, dcat, cw, lg, lb))


def _rg_gates(xc, wa_ref, ba, wx_ref, bx, sp, nh, hd):
    parts = []
    for h in range(nh):
        hs = slice(h * hd, (h + 1) * hd)
        xh = xc[:, hs].astype(BF16)
        r = jax.nn.sigmoid(_dot(xh, wa_ref[h]) + ba[:, hs])
        ig = jax.nn.sigmoid(_dot(xh, wx_ref[h]) + bx[:, hs])
        log_a = -RG_C * r * sp[:, hs]
        parts.append((r, ig, jnp.exp(log_a), jnp.sqrt(-_expm1(2.0 * log_a))))
    return parts


def _conv_b(xbuf, bw_ref, bb, tt, kb):
    off = HALO_B - (kb - 1)
    xc = bb
    for k in range(kb):
        xc = xc + bw_ref[k:k + 1, :] * xbuf[pl.ds(off + k, tt), :]
    return xc


def rglru_fwd(u, bw, bb, wa, ba, wx, bx, lam, name):
    t = u.shape[0]
    kb, c = bw.shape
    nh, hd, _ = wa.shape
    tt = _tile(t, 256, HALO_B)
    hb = tt // HALO_B

    def body(xr_ref, gr_ref, px_ref, bw_ref, bb_ref, wa_ref, ba_ref, wx_ref, bx_ref, lam_ref,
             yb_ref, hs_ref, xbuf, a_s, u_s, hc):
        i = pl.program_id(0)

        @pl.when(i == 0)
        def _():
            hc[...] = jnp.zeros_like(hc)

        xbuf[0:HALO_B, :] = px_ref[...] * (i > 0).astype(F32)
        xbuf[HALO_B:, :] = xr_ref[...]
        xc = _conv_b(xbuf, bw_ref, bb_ref[...], tt, kb)
        sp = _softplus(-lam_ref[...])
        gates = _rg_gates(xc, wa_ref, ba_ref[...], wx_ref, bx_ref[...], sp, nh, hd)
        for h, (_, ig, a, mult) in enumerate(gates):
            hs = slice(h * hd, (h + 1) * hd)
            a_s[:, hs] = a
            u_s[:, hs] = mult * ig * xc[:, hs]

        def step(g, hcur):
            base = pl.multiple_of(g * 8, 8)
            for j in range(8):
                hcur = a_s[pl.ds(base + j, 1), :] * hcur + u_s[pl.ds(base + j, 1), :]
                hs_ref[pl.ds(base + j, 1), :] = hcur
            return hcur

        hc[...] = lax.fori_loop(0, tt // 8, step, hc[...])
        yb_ref[...] = (hs_ref[...] * _gelu(gr_ref[...])).astype(BF16)

    cur = lambda col: pl.BlockSpec((tt, c), lambda i: (i, col))
    vec = pl.BlockSpec((1, c), lambda i: (0, 0))
    wsp = pl.BlockSpec((nh, hd, hd), lambda i: (0, 0, 0))
    return pl.pallas_call(
        body, name=name, grid=(t // tt,),
        in_specs=[cur(2), cur(3), pl.BlockSpec((HALO_B, c), lambda i: (jnp.maximum(i * hb - 1, 0), 2)),
                  pl.BlockSpec((kb, c), lambda i: (0, 0)), vec, wsp, vec, wsp, vec, vec],
        out_specs=[pl.BlockSpec((tt, c), lambda i: (i, 0))] * 2,
        out_shape=[SDS((t, c), BF16), SDS((t, c), F32)],
        scratch_shapes=[pltpu.VMEM((tt + HALO_B, c), F32), pltpu.VMEM((tt, c), F32),
                        pltpu.VMEM((tt, c), F32), pltpu.VMEM((1, c), F32)],
        compiler_params=_cp("arbitrary"),
    )(u, u, u, bw, bb, wa, ba, wx, bx, lam)


def rglru_bwd(u, hs_all, dcat, bw, bb, wa, ba, wx, bx, lam, name):
    t = u.shape[0]
    kb, c = bw.shape
    nh, hd, _ = wa.shape
    tt = _tile(t, 256, HALO_B)
    nt, hb = t // tt, tt // HALO_B
    off = HALO_B - (kb - 1)

    def body(xr_ref, gr_ref, px_ref, hs_ref, ph_ref, dyb_ref, bw_ref, bb_ref, wa_ref, ba_ref, wx_ref, bx_ref,
             lam_ref, du_ref, pgrad_ref, dwa_ref, dwx_ref, xbuf, hbuf, a_s, g_s, dxbuf, cg, cdx):
        i = pl.program_id(0)

        @pl.when(i == 0)
        def _():
            cg[...] = jnp.zeros_like(cg)
            cdx[...] = jnp.zeros_like(cdx)
            pgrad_ref[...] = jnp.zeros_like(pgrad_ref)
            dwa_ref[...] = jnp.zeros_like(dwa_ref)
            dwx_ref[...] = jnp.zeros_like(dwx_ref)

        live = (i < nt - 1).astype(F32)
        xbuf[0:HALO_B, :] = px_ref[...] * live
        xbuf[HALO_B:, :] = xr_ref[...]
        hbuf[0:HALO_B, :] = ph_ref[...] * live
        hbuf[HALO_B:, :] = hs_ref[...]
        xc = _conv_b(xbuf, bw_ref, bb_ref[...], tt, kb)
        lam_v = lam_ref[...]
        sp = _softplus(-lam_v)
        gates = _rg_gates(xc, wa_ref, ba_ref[...], wx_ref, bx_ref[...], sp, nh, hd)
        gr = gr_ref[...]
        dyb = dyb_ref[...].astype(F32)
        g_s[...] = dyb * _gelu(gr)
        for h, (_, _, a, _) in enumerate(gates):
            a_s[:, h * hd:(h + 1) * hd] = a

        def step(g, carry):
            base = pl.multiple_of((tt // 8 - 1 - g) * 8, 8)
            for j in range(7, -1, -1):
                gt = g_s[pl.ds(base + j, 1), :] + carry
                g_s[pl.ds(base + j, 1), :] = gt
                carry = a_s[pl.ds(base + j, 1), :] * gt
            return carry

        cg[...] = lax.fori_loop(0, tt // 8, step, cg[...])
        hprev = hbuf[pl.ds(HALO_B - 1, tt), :]
        gfull = g_s[...]
        for h, (r, ig, a, mult) in enumerate(gates):
            hs = slice(h * hd, (h + 1) * hd)
            g, xch = gfull[:, hs], xc[:, hs]
            d_la = g * hprev[:, hs] * a - g * ig * xch * (a * a) / mult
            d_ig = g * mult * xch
            d_ra = d_la * (-RG_C * sp[:, hs]) * r * (1.0 - r)
            d_ia = d_ig * ig * (1.0 - ig)
            dsp = jnp.sum(d_la * (-RG_C) * r, axis=0, keepdims=True)
            pgrad_ref[kb + 1:kb + 2, hs] += jnp.sum(d_ra, axis=0, keepdims=True)
            pgrad_ref[kb + 2:kb + 3, hs] += jnp.sum(d_ia, axis=0, keepdims=True)
            pgrad_ref[kb + 3:kb + 4, hs] += dsp * (-jax.nn.sigmoid(-lam_v[:, hs]))
            xh, d_ra_b, d_ia_b = xch.astype(BF16), d_ra.astype(BF16), d_ia.astype(BF16)
            dwa_ref[h] += _dot_tn(xh, d_ra_b)
            dwx_ref[h] += _dot_tn(xh, d_ia_b)
            dxbuf[0:tt, hs] = g * mult * ig + _dot_nt(d_ra_b, wa_ref[h]) + _dot_nt(d_ia_b, wx_ref[h])
        dxbuf[tt:, :] = cdx[...]
        cdx[...] = dxbuf[0:HALO_B, :]
        d_xc = dxbuf[0:tt, :]
        pgrad_ref[kb:kb + 1, :] += jnp.sum(d_xc, axis=0, keepdims=True)
        d_xr = jnp.zeros((tt, c), F32)
        for k in range(kb):
            d_xr = d_xr + bw_ref[k:k + 1, :] * dxbuf[pl.ds(kb - 1 - k, tt), :]
            pgrad_ref[k:k + 1, :] += jnp.sum(d_xc * xbuf[pl.ds(off + k, tt), :], axis=0, keepdims=True)
        du_ref[:, 0:c] = d_xr.astype(BF16)
        du_ref[:, c:2 * c] = (dyb * hs_ref[...] * _gelu_grad(gr)).astype(BF16)

    cur = lambda col: pl.BlockSpec((tt, c), lambda i: (nt - 1 - i, col))
    prev = lambda col: pl.BlockSpec((HALO_B, c), lambda i: (jnp.maximum((nt - 1 - i) * hb - 1, 0), col))
    vec = pl.BlockSpec((1, c), lambda i: (0, 0))
    wsp = pl.BlockSpec((nh, hd, hd), lambda i: (0, 0, 0))
    return pl.pallas_call(
        body, name=name, grid=(nt,),
        in_specs=[cur(2), cur(3), prev(2), cur(0), prev(0), cur(1),
                  pl.BlockSpec((kb, c), lambda i: (0, 0)), vec, wsp, vec, wsp, vec, vec],
        out_specs=[pl.BlockSpec((tt, 2 * c), lambda i: (nt - 1 - i, 0)),
                   pl.BlockSpec((kb + 4, c), lambda i: (0, 0)), wsp, wsp],
        out_shape=[SDS((t, 2 * c), BF16), SDS((kb + 4, c), F32), SDS((nh, hd, hd), F32), SDS((nh, hd, hd), F32)],
        scratch_shapes=[pltpu.VMEM((tt + HALO_B, c), F32), pltpu.VMEM((tt + HALO_B, c), F32),
                        pltpu.VMEM((tt, c), F32), pltpu.VMEM((tt, c), F32), pltpu.VMEM((tt + HALO_B, c), F32),
                        pltpu.VMEM((1, c), F32), pltpu.VMEM((HALO_B, c), F32)],
        compiler_params=_cp("arbitrary"),
    )(u, u, u, hs_all, hs_all, dcat, bw, bb, wa, ba, wx, bx, lam)


MASS_CUTOFF = 100.0


def attn_fwd(qkv, nh, name):
    t = qkv.shape[0]
    dh = qkv.shape[1] // (3 * nh)
    bq = _tile(t, 256)
    scale = 1.0 / math.sqrt(dh)
    hp = 2 if nh % 2 == 0 else 1

    def body(q_ref, k_ref, v_ref, o_ref, s_ref, start_ref):
        pair, qi = pl.program_id(0), pl.program_id(1)
        row = lax.broadcasted_iota(jnp.int32, (bq, bq), 0)
        col = lax.broadcasted_iota(jnp.int32, (bq, bq), 1)
        tri = (row >= col).astype(BF16)

        def block(hh, kb, c, acc, mask):
            ks = pl.ds(pl.multiple_of(kb * bq, bq), bq)
            hs = pl.ds(hh * dh, dh)
            z = _dot_nt(q_ref[:, hs], k_ref[ks, hs]) * scale
            sp = _softplus(z)
            if mask is not None:
                sp = jnp.where(mask, sp, 0.0)
            lw = z - c - _dot(sp.astype(BF16), tri)
            if mask is not None:
                lw = jnp.where(mask, lw, -1e30)
            acc = acc + _dot(jnp.exp(lw).astype(BF16), v_ref[ks, hs])
            return c + jnp.sum(sp, axis=1, keepdims=True), acc

        state = []
        for hh in range(hp):
            c, acc = block(hh, qi, jnp.zeros((bq, 1), F32), jnp.zeros((bq, dh), F32), col < row)
            state.append(block(hh, jnp.maximum(qi - 1, 0), c, acc, qi > 0))

        def more(st):
            return jnp.logical_and(st[0] >= 0, jnp.min(st[1]) < MASS_CUTOFF)

        for hh, (c, acc) in enumerate(state):
            def step(st, hh=hh):
                c_new, acc_new = block(hh, st[0], st[1], st[2], None)
                return st[0] - 1, c_new, acc_new

            kb, c, acc = lax.while_loop(more, step, (qi - 2, c, acc))
            o_ref[:, pl.ds(hh * dh, dh)] = acc.astype(BF16)
            s_ref[hh] = jnp.broadcast_to(c, (bq, LANES))
            start_ref[hp * pair + hh, qi] = (kb + 1).astype(F32)

    whole = lambda base: pl.BlockSpec((t, hp * dh), lambda h, qi: (0, base + h))
    return pl.pallas_call(
        body, name=name, grid=(nh // hp, t // bq),
        in_specs=[pl.BlockSpec((bq, hp * dh), lambda h, qi: (qi, h)), whole(nh // hp), whole(2 * nh // hp)],
        out_specs=[pl.BlockSpec((bq, hp * dh), lambda h, qi: (qi, h)),
                   pl.BlockSpec((hp, bq, LANES), lambda h, qi: (h, qi, 0)),
                   pl.BlockSpec(memory_space=pltpu.SMEM)],
        out_shape=[SDS((t, nh * dh), BF16), SDS((nh, t, LANES), F32), SDS((nh, t // bq), F32)],
        compiler_params=_cp("arbitrary", "arbitrary"),
    )(qkv, qkv, qkv)


def attn_bwd(qkv, do, s_tot, start, nh, name, ride=None):
    t = qkv.shape[0]
    dh = qkv.shape[1] // (3 * nh)
    bq = _tile(t, 256)
    scale = 1.0 / math.sqrt(dh)

    def body(start_ref, q_ref, k_ref, v_ref, do_ref, s_ref, dq_ref, dk_ref, dv_ref):
        hd, qi = pl.program_id(0), pl.program_id(1)

        @pl.when(qi == 0)
        def _():
            dk_ref[...] = jnp.zeros_like(dk_ref)
            dv_ref[...] = jnp.zeros_like(dv_ref)

        q, do_b = q_ref[...], do_ref[...]
        s_row = s_ref[:, 0:1]
        row = lax.broadcasted_iota(jnp.int32, (bq, bq), 0)
        col = lax.broadcasted_iota(jnp.int32, (bq, bq), 1)
        tri_suffix = (row >= col).astype(BF16)
        tri_prefix = (row <= col).astype(BF16)

        def block(kb, psp, pg, dq, mask):
            ks = pl.ds(pl.multiple_of(kb * bq, bq), bq)
            k_b, v_b = k_ref[ks, :], v_ref[ks, :]
            z = _dot_nt(q, k_b) * scale
            sp_all = _softplus(z)
            sp = sp_all if mask is None else jnp.where(mask, sp_all, 0.0)
            psp = psp + jnp.sum(sp, axis=1, keepdims=True)
            lw = z - (s_row - psp) - _dot(sp.astype(BF16), tri_suffix)
            if mask is not None:
                lw = jnp.where(mask, lw, -1e30)
            a = jnp.exp(lw)
            g = _dot_nt(do_b, v_b) * a
            dz = g - (pg + _dot(g.astype(BF16), tri_prefix)) * jnp.exp(z - sp_all)
            if mask is not None:
                dz = jnp.where(mask, dz, 0.0)
            dzs = (dz * scale).astype(BF16)
            dk_ref[ks, :] += _dot_tn(dzs, q)
            dv_ref[ks, :] += _dot_tn(a.astype(BF16), do_b)
            return psp, pg + jnp.sum(g, axis=1, keepdims=True), dq + _dot(dzs, k_b)

        first = jnp.clip(start_ref[hd, qi].astype(jnp.int32), 0, qi)
        init = (jnp.zeros((bq, 1), F32), jnp.zeros((bq, 1), F32), jnp.zeros((bq, dh), F32))
        psp, pg, dq = lax.fori_loop(first, jnp.maximum(qi - 1, first),
                                    lambda kb, cr: block(kb, cr[0], cr[1], cr[2], None), init)
        psp, pg, dq = block(jnp.maximum(qi - 1, 0), psp, pg, dq, jnp.logical_and(qi > 0, first < qi))
        _, _, dq = block(qi, psp, pg, dq, col < row)
        dq_ref[...] = dq.astype(BF16)

    whole = lambda base: pl.BlockSpec((t, dh), lambda h, qi: (0, base + h))
    qblk = pl.BlockSpec((bq, dh), lambda h, qi: (qi, h))
    acc = pl.BlockSpec((t, dh), lambda h, qi: (0, h))
    return hosted_call(
        body, ride, name=name, grid=(nh, t // bq),
        in_specs=[pl.BlockSpec(memory_space=pltpu.SMEM), qblk, whole(nh), whole(2 * nh), qblk,
                  pl.BlockSpec((None, bq, LANES), lambda h, qi: (h, qi, 0))],
        out_specs=[qblk, acc, acc],
        out_shape=[SDS((t, nh * dh), BF16), SDS((t, nh * dh), F32), SDS((t, nh * dh), F32)],
        scratch_shapes=[], operands=(start, qkv, qkv, qkv, do, s_tot))


def all_gather(x, name, axis=None, first_layer_only=False):
    if axis is None:
        out_shape = (N_DEV,) + x.shape
    else:
        out_shape = x.shape[:axis] + (N_DEV * x.shape[axis],) + x.shape[axis + 1:]

    def body(x_full_ref, out_full_ref, send_sems, recv_sems, local_sem):
        mx, my, mc = lax.axis_index("x"), lax.axis_index("y"), lax.axis_index("c")
        me, sibling = (mx, my, mc), (mx, my, 1 - mc)
        chips = [(1 - mx, my), (mx, 1 - my), (1 - mx, 1 - my)]
        x_ref = x_full_ref.at[0] if first_layer_only else x_full_ref
        out_ref = out_full_ref.at[0] if first_layer_only else out_full_ref
        ax = axis - 1 if first_layer_only else axis

        def slot(px, py, pc):
            return _shard_of(out_ref, ax, 4 * px + 2 * py + pc, None if axis is None else x.shape[axis])

        def copy(k, block, to, src=None):
            return pltpu.make_async_remote_copy(
                src_ref=slot(*block) if src is None else src, dst_ref=slot(*block),
                send_sem=send_sems.at[k], recv_sem=recv_sems.at[k], device_id=to, device_id_type=MESH)

        mine = pltpu.make_async_copy(x_ref, slot(*me), local_sem)
        mine.start()
        first = [copy(0, me, sibling, src=x_ref)]
        first += [copy(1 + j, me, (*chip, mc), src=x_ref) for j, chip in enumerate(chips)]
        for cp in first:
            cp.start()
        passed = [copy(4 + j, (*chip, mc), sibling) for j, chip in enumerate(chips)]
        for j, chip in enumerate(chips):
            copy(1 + j, (*chip, mc), me).wait_recv()
            passed[j].start()
        copy(0, sibling, me).wait_recv()
        for j, chip in enumerate(chips):
            copy(4 + j, (*chip, 1 - mc), me).wait_recv()
        for cp in first + passed:
            cp.wait_send()
        mine.wait()

    return pl.pallas_call(
        body, name=name, out_shape=SDS(out_shape, x.dtype), in_specs=[ANY], out_specs=ANY,
        scratch_shapes=[pltpu.SemaphoreType.DMA((7,)), pltpu.SemaphoreType.DMA((7,)), pltpu.SemaphoreType.DMA],
    )(x)


def reduce_adamw(gs, w, m, v, name):
    s, r, c = gs.shape
    tr = _tile(r, max(16, (128 * 1024) // c), 16)

    def body(gs_ref, w_ref, m_ref, v_ref, g_out, d_out, m_out, v_out):
        g = gs_ref[0].astype(F32)
        for j in range(1, s):
            g = g + gs_ref[j].astype(F32)
        m_new = ADAM_B1 * m_ref[...] + (1.0 - ADAM_B1) * g
        v_new = ADAM_B2 * v_ref[...] + (1.0 - ADAM_B2) * (g * g)
        m_hat = m_new / (1.0 - ADAM_B1 ** ADAM_STEP)
        v_hat = v_new / (1.0 - ADAM_B2 ** ADAM_STEP)
        g_out[...] = g
        d_out[...] = -ADAM_LR * (m_hat / (jnp.sqrt(v_hat) + ADAM_EPS) + ADAM_WD * w_ref[...])
        m_out[...] = m_new
        v_out[...] = v_new

    row = pl.BlockSpec((tr, c), lambda i: (i, 0))
    return pl.pallas_call(
        body, name=name, grid=(r // tr,),
        in_specs=[pl.BlockSpec((s, tr, c), lambda i: (0, i, 0)), row, row, row], out_specs=[row] * 4,
        out_shape=[SDS((r, c), F32)] * 4, compiler_params=_cp("parallel"),
    )(gs, w, m, v)


BIG = {
    "w_in_rec": 2, "w_out_rec": 1, "w_qkv": 2, "w_o_attn": 1,
    "w_mlp_up": 2, "w_mlp_down": 1, "w_ple_proj": 2, "w_ple_gate": 1,
}
REPLICATED = ["norm_mix_g", "norm_mlp_g", "norm_ple_g", "norm_f_g", "conv_a_b", "ln_a_g", "ln_a_b", "conv_b_b",
              "w_rg_a", "b_rg_a", "w_rg_x", "b_rg_x", "rg_lambda"]
CONV_W = ["conv_a_w", "conv_b_w"]
WEIGHTS = ["norm_mix_g", "norm_mlp_g", "norm_ple_g", "norm_f_g", "w_in_rec", "conv_a_w", "conv_a_b", "ln_a_g",
           "ln_a_b", "conv_b_w", "conv_b_b", "w_rg_a", "b_rg_a", "w_rg_x", "b_rg_x", "rg_lambda", "w_out_rec",
           "w_qkv", "w_o_attn", "w_mlp_up", "w_mlp_down", "w_ple_proj", "w_ple_gate"]


def _pack(arrays):
    flat = jnp.concatenate([a.reshape(-1) for a in arrays])
    pad = (-flat.shape[0]) % (8 * LANES)
    return jnp.pad(flat, (0, pad)).reshape(-1, LANES)


def _unpack(packed, shapes, lead=()):
    flat = packed.reshape(lead + (-1,))
    out, pos = [], 0
    for shp in shapes:
        size = math.prod(shp)
        out.append(flat[..., pos:pos + size].reshape(lead + tuple(shp)))
        pos += size
    return out


def _step(x, p, loss_target, w, mom, var):
    dev = 4 * lax.axis_index("x") + 2 * lax.axis_index("y") + lax.axis_index("c")
    depth = w["norm_mix_g"].shape[0]
    h = x[0]
    nh = SB_HEADS

    wb = {n: w[n].astype(BF16) for n in BIG}
    wg = {n: None for n in BIG}
    for n in ("w_in_rec", "w_out_rec", "w_mlp_up", "w_mlp_down", "w_ple_proj", "w_ple_gate"):
        wg[n] = all_gather(wb[n], name=f"ag0_{n}", axis=BIG[n], first_layer_only=True)
    mix_names = lambda layer: ("w_in_rec", "w_out_rec") if layer % 2 == 0 else ("w_qkv", "w_o_attn")

    def mm_gathering(targets, *args, **kw):
        ride = Ride()
        for n, layer in targets:
            ride.gather(wb[n], BIG[n], layer, wg[n])
        outs = mm_w(*args, ride=ride if targets else None, **kw)
        for (n, _), a in zip(targets, outs[len(outs) - len(targets):]):
            wg[n] = a
        return outs[:len(outs) - len(targets)]

    conv_shapes = [w[n].shape for n in CONV_W]
    conv_all = all_gather(_pack([w[n] for n in CONV_W]), name="ag_conv_w")
    conv_full = [jnp.moveaxis(a, 0, -2).reshape(a.shape[1:-1] + (-1,))
                 for a in _unpack(conv_all, conv_shapes, lead=(N_DEV,))]
    conv_a_w, conv_b_w = conv_full
    vec = lambda a: a.reshape(1, -1)

    saved = []
    for i in range(depth):
        j = i // 2
        s = {"h0": h}
        in_name, out_name = mix_names(i)
        s["in_name"], s["out_name"] = in_name, out_name
        nxt = i + 1
        if nxt < depth:
            t_in, t_out = [(mix_names(nxt)[0], nxt // 2)], [(mix_names(nxt)[1], nxt // 2), ("w_ple_gate", nxt)]
            t_up, t_down, t_gate = [("w_mlp_down", nxt)], [("w_mlp_up", nxt)], [("w_ple_proj", nxt)]
        else:
            t_in = t_out = t_up = t_down = t_gate = []
        s["hn1"] = rms_fwd(h, w["norm_mix_g"][i], name="rms_fwd")
        if i % 2 == 0:
            (s["u"],) = mm_gathering(t_in, s["hn1"], wg[in_name], j, transpose_w=False, out_dtypes=[F32],
                                     name="mm_in_rec")
            ya, s["ya1"] = conf_fwd(s["u"], conv_a_w[j], vec(w["conv_a_b"][j]), vec(w["ln_a_g"][j]),
                                    vec(w["ln_a_b"][j]), name="conf_fwd")
            yb, s["hs"] = rglru_fwd(s["u"], conv_b_w[j], vec(w["conv_b_b"][j]), w["w_rg_a"][j].astype(BF16),
                                    vec(w["b_rg_a"][j]), w["w_rg_x"][j].astype(BF16), vec(w["b_rg_x"][j]),
                                    vec(w["rg_lambda"][j]), name="rglru_fwd")
            s["mix_in"] = jnp.concatenate([ya, yb], axis=1)
        else:
            (s["qkv"],) = mm_gathering(t_in, s["hn1"], wg[in_name], j, transpose_w=False, out_dtypes=[BF16],
                                       name="mm_qkv")
            s["mix_in"], s["s_tot"], s["start"] = attn_fwd(s["qkv"], nh, name="attn_fwd")
        (h,) = mm_gathering(t_out, s["mix_in"], wg[out_name], j, transpose_w=False, out_dtypes=[F32],
                            extras=[h], epilogue=lambda acc, res: (res + acc,), name="mm_mix_out")
        s["h1"] = h
        s["hn2"] = rms_fwd(h, w["norm_mlp_g"][i], name="rms_fwd")
        relu = lambda acc: jnp.maximum(acc, 0.0)
        s["up"], s["act"] = mm_gathering(t_up, s["hn2"], wg["w_mlp_up"], i, transpose_w=False,
                                         out_dtypes=[BF16, BF16], name="mm_mlp_up", tm=1024,
                                         epilogue=lambda acc: (acc, relu(acc) * relu(acc)))
        (h,) = mm_gathering(t_down, s["act"], wg["w_mlp_down"], i, transpose_w=False, out_dtypes=[F32],
                            tm=1024, extras=[h], epilogue=lambda acc, res: (res + acc,),
                            name="mm_mlp_down")
        s["h2"] = h
        s["hn3"] = rms_fwd(h, w["norm_ple_g"][i], name="rms_fwd")
        s["p"] = p[i, 0].astype(BF16)
        (s["pp"],) = mm_w(s["p"], wg["w_ple_proj"], i, transpose_w=False, out_dtypes=[F32], name="mm_ple_proj")
        h, s["gz"] = mm_gathering(t_gate, s["hn3"], wg["w_ple_gate"], i, transpose_w=False,
                                  out_dtypes=[F32, BF16], extras=[h, s["pp"]], name="mm_ple_gate",
                                  epilogue=lambda acc, res, pp: (res + pp * jax.nn.sigmoid(acc), acc))
        saved.append(s)

    dh, dg_f, loss_part = loss_head(h, w["norm_f_g"], loss_target[0], name="loss_head")
    loss = lax.psum(loss_part[0, 0], ("x", "y", "c"))

    acc = {n: None for n in BIG}
    part = {n: [None] * w[n].shape[0] for n in REPLICATED + CONV_W if n != "norm_f_g"}
    part["norm_f_g"] = dg_f[0]

    recv = {n: None for n in BIG}

    def exchanging(targets):
        ride = Ride()
        for n, layer in targets:
            ride.exchange(acc[n], BIG[n], layer, recv[n])
        return ride

    def landed(targets, arrays):
        for (n, _), a in zip(targets, arrays):
            recv[n] = a

    def dw(name, a, b, layer, targets=()):
        outs = mm_dw(a, b, acc[name], layer, wg[name].shape[0], name=f"dw_{name}",
                     ride=exchanging(targets) if targets else None)
        acc[name] = outs[0] if targets else outs
        landed(targets, outs[1:] if targets else [])

    def mm_exchanging(targets, *args, **kw):
        outs = mm_w(*args, ride=exchanging(targets), **kw)
        landed(targets, outs[len(outs) - len(targets):])
        return outs[:len(outs) - len(targets)]

    for i in reversed(range(depth)):
        j = i // 2
        s = saved[i]
        in_name, out_name = s["in_name"], s["out_name"]
        d_pp, d_gz = ple_bwd(dh, s["gz"], s["pp"], name="ple_bwd")
        dw("w_ple_proj", s["p"], d_pp, i)
        dw("w_ple_gate", s["hn3"], d_gz, i)
        (d_hn3,) = mm_exchanging([("w_ple_proj", i), ("w_ple_gate", i)], d_gz, wg["w_ple_gate"], i,
                                 transpose_w=True, out_dtypes=[BF16], name="mmT_ple_gate")
        dh, dh_b, dg = rms_bwd(s["h2"], w["norm_ple_g"][i], d_hn3, dh, name="rms_bwd")
        part["norm_ple_g"][i] = dg[0]
        (d_up,) = mm_w(dh_b, wg["w_mlp_down"], i, transpose_w=True, out_dtypes=[BF16], tm=1024,
                       extras=[s["up"]], name="mmT_mlp_down",
                       epilogue=lambda acc_, up: (acc_ * (2.0 * jnp.maximum(up.astype(F32), 0.0)),))
        dw("w_mlp_down", s["act"], dh_b, i)
        dw("w_mlp_up", s["hn2"], d_up, i, targets=[("w_mlp_down", i)])
        (d_hn2,) = mm_w(d_up, wg["w_mlp_up"], i, transpose_w=True, out_dtypes=[BF16], tm=1024, name="mmT_mlp_up")
        up_grad = [("w_mlp_up", i)]
        dh, dh_b, dg = rms_bwd(s["h1"], w["norm_mlp_g"][i], d_hn2, dh, name="rms_bwd")
        part["norm_mlp_g"][i] = dg[0]
        dw(out_name, s["mix_in"], dh_b, j)
        if i % 2 == 0:
            (d_cat,) = mm_exchanging([(out_name, j)], dh_b, wg[out_name], j, transpose_w=True,
                                     out_dtypes=[BF16], name="mmT_mix_out")
            du_a, pg_a, *sent = conf_bwd(s["u"], s["ya1"], d_cat, conv_a_w[j], vec(w["ln_a_g"][j]),
                                         vec(w["ln_a_b"][j]), name="conf_bwd", ride=exchanging(up_grad))
            landed(up_grad, sent)
            du_b, pg_b, d_wa, d_wx = rglru_bwd(
                s["u"], s["hs"], d_cat, conv_b_w[j], vec(w["conv_b_b"][j]), w["w_rg_a"][j].astype(BF16),
                vec(w["b_rg_a"][j]), w["w_rg_x"][j].astype(BF16), vec(w["b_rg_x"][j]), vec(w["rg_lambda"][j]),
                name="rglru_bwd")
            ka, kb = conv_a_w.shape[1], conv_b_w.shape[1]
            part["conv_a_w"][j], part["conv_a_b"][j] = pg_a[:ka], pg_a[ka]
            part["ln_a_g"][j], part["ln_a_b"][j] = pg_a[ka + 1], pg_a[ka + 2]
            part["conv_b_w"][j], part["conv_b_b"][j] = pg_b[:kb], pg_b[kb]
            part["b_rg_a"][j], part["b_rg_x"][j], part["rg_lambda"][j] = pg_b[kb + 1], pg_b[kb + 2], pg_b[kb + 3]
            part["w_rg_a"][j], part["w_rg_x"][j] = d_wa, d_wx
            d_mix = jnp.concatenate([du_a, du_b], axis=1)
        else:
            (d_o,) = mm_exchanging([(out_name, j)], dh_b, wg[out_name], j, transpose_w=True,
                                   out_dtypes=[BF16], name="mmT_mix_out")
            dq, dk, dv, *sent = attn_bwd(s["qkv"], d_o, s["s_tot"], s["start"], nh, name="attn_bwd",
                                         ride=exchanging(up_grad))
            landed(up_grad, sent)
            d_mix = jnp.concatenate([dq, dk.astype(BF16), dv.astype(BF16)], axis=1)
        dw(in_name, s["hn1"], d_mix, j)
        (d_hn1,) = mm_exchanging([(in_name, j)], d_mix, wg[in_name], j, transpose_w=True, out_dtypes=[BF16],
                                 name="mmT_mix_in")
        dh, _, dg = rms_bwd(s["h0"], w["norm_mix_g"][i], d_hn1, dh, name="rms_bwd")
        part["norm_mix_g"][i] = dg[0]

    grads, deltas, new_m, new_v = {}, {}, {}, {}

    def finish(name, outs, shape):
        for d, o in zip((grads, deltas, new_m, new_v), outs):
            d[name] = o.reshape(shape)

    for n in BIG:
        _, nl, r, c = recv[n].shape
        flat = lambda a: a.reshape(nl * r, c)
        finish(n, reduce_adamw(recv[n].reshape(N_DEV, nl * r, c), flat(w[n]), flat(mom[n]), flat(var[n]),
                               name=f"adamw_{n}"), w[n].shape)

    full = {n: (part[n] if n == "norm_f_g" else jnp.stack(part[n])) for n in REPLICATED + CONV_W}
    rep = _pack([full[n] for n in REPLICATED])
    conv = _pack([full[n] for n in CONV_W])
    small = all_gather(jnp.concatenate([rep, conv]), name="ag_small_grads")
    rep_shapes = [w[n].shape for n in REPLICATED]
    outs = reduce_adamw(small[:, :rep.shape[0]], _pack([w[n] for n in REPLICATED]),
                        _pack([mom[n] for n in REPLICATED]), _pack([var[n] for n in REPLICATED]), name="adamw_small")
    for d, o in zip((grads, deltas, new_m, new_v), outs):
        for n, a in zip(REPLICATED, _unpack(o, rep_shapes)):
            d[n] = a
    conv_parts = _unpack(small[:, rep.shape[0]:], [full[n].shape for n in CONV_W], lead=(N_DEV,))
    width = w["conv_a_w"].shape[-1]
    mine = [lax.dynamic_slice_in_dim(a, dev * width, width, axis=a.ndim - 1) for a in conv_parts]
    packed = jnp.stack([_pack([a[d] for a in mine]) for d in range(N_DEV)])
    outs = reduce_adamw(packed, _pack([w[n] for n in CONV_W]), _pack([mom[n] for n in CONV_W]),
                        _pack([var[n] for n in CONV_W]), name="adamw_conv_w")
    for d, o in zip((grads, deltas, new_m, new_v), outs):
        for n, a in zip(CONV_W, _unpack(o, conv_shapes)):
            d[n] = a
    return loss, dh[None], grads, deltas, new_m, new_v


def kernel(x, p, norm_mix_g, norm_mlp_g, norm_ple_g, norm_f_g, w_in_rec, conv_a_w, conv_a_b, ln_a_g, ln_a_b, conv_b_w, conv_b_b, w_rg_a, b_rg_a, w_rg_x, b_rg_x, rg_lambda, w_out_rec, w_qkv, w_o_attn, w_mlp_up, w_mlp_down, w_ple_proj, w_ple_gate, loss_target, m_norm_mix_g, m_norm_mlp_g, m_norm_ple_g, m_norm_f_g, m_w_in_rec, m_conv_a_w, m_conv_a_b, m_ln_a_g, m_ln_a_b, m_conv_b_w, m_conv_b_b, m_w_rg_a, m_b_rg_a, m_w_rg_x, m_b_rg_x, m_rg_lambda, m_w_out_rec, m_w_qkv, m_w_o_attn, m_w_mlp_up, m_w_mlp_down, m_w_ple_proj, m_w_ple_gate, v_norm_mix_g, v_norm_mlp_g, v_norm_ple_g, v_norm_f_g, v_w_in_rec, v_conv_a_w, v_conv_a_b, v_ln_a_g, v_ln_a_b, v_conv_b_w, v_conv_b_b, v_w_rg_a, v_b_rg_a, v_w_rg_x, v_b_rg_x, v_rg_lambda, v_w_out_rec, v_w_qkv, v_w_o_attn, v_w_mlp_up, v_w_mlp_down, v_w_ple_proj, v_w_ple_gate):
    given = dict(locals())
    w = {n: given[n] for n in WEIGHTS}
    mom = {n: given["m_" + n] for n in WEIGHTS}
    var = {n: given["v_" + n] for n in WEIGHTS}
    loss, grad_x, grads, deltas, new_m, new_v = _step(x, p, loss_target, w, mom, var)
    return (loss, grad_x, *[grads[n] for n in WEIGHTS], *[deltas[n] for n in WEIGHTS],
            *[new_m[n] for n in WEIGHTS], *[new_v[n] for n in WEIGHTS])
```

```python
import functools
import math

import jax
import jax.numpy as jnp
from jax import lax
from jax.experimental import pallas as pl
from jax.experimental.pallas import tpu as pltpu

F32, BF16 = jnp.float32, jnp.bfloat16
EPS = 1e-6
N_DEV = 8
SB_HEADS = 16
RG_C = 8.0
HALO_A = 32
HALO_B = 8
CONV_ROWS = 64
LANES = 128
VMEM_LIMIT = 48 * 1024 * 1024
ADAM_LR, ADAM_B1, ADAM_B2, ADAM_EPS, ADAM_WD, ADAM_STEP = 0.001, 0.9, 0.999, 1e-08, 0.01, 10
MESH = pl.DeviceIdType.MESH
SDS = jax.ShapeDtypeStruct
ANY = pl.BlockSpec(memory_space=pl.ANY)


def _cp(*sem):
    return pltpu.CompilerParams(dimension_semantics=sem, vmem_limit_bytes=VMEM_LIMIT)


def _tile(dim, pref, align=LANES):
    if dim <= pref:
        return dim
    t = (pref // align) * align
    while t >= align:
        if dim % t == 0:
            return t
        t -= align
    return dim


def _softplus(z):
    return jnp.maximum(z, 0.0) + jnp.log(1.0 + jnp.exp(-jnp.abs(z)))


def _expm1(x):
    t = x * (1.0 + x * (0.5 + x * (1.0 / 6.0 + x * (1.0 / 24.0 + x * (1.0 / 120.0)))))
    return jnp.where(jnp.abs(x) < 0.1, t, jnp.exp(x) - 1.0)


_GELU_C = math.sqrt(2.0 / math.pi)


def _gelu(x):
    return 0.5 * x * (1.0 + jnp.tanh(_GELU_C * (x + 0.044715 * x * x * x)))


def _gelu_grad(x):
    th = jnp.tanh(_GELU_C * (x + 0.044715 * x * x * x))
    return 0.5 * (1.0 + th) + 0.5 * x * (1.0 - th * th) * _GELU_C * (1.0 + 3.0 * 0.044715 * x * x)


def _dot(a, b):
    return jnp.dot(a, b, preferred_element_type=F32)


def _dot_nt(a, b):
    return lax.dot_general(a, b, (((1,), (1,)), ((), ())), preferred_element_type=F32)


def _dot_tn(a, b):
    return lax.dot_general(a, b, (((0,), (0,)), ((), ())), preferred_element_type=F32)


def _shard_of(ref, axis, dev, size):
    if axis is None:
        return ref.at[dev]
    return ref.at[(slice(None),) * axis + (pl.ds(pl.multiple_of(dev * size, size), size),)]


def _peer(k, mx, my, mc):
    return (1 - mx if k & 4 else mx, 1 - my if k & 2 else my, 1 - mc if k & 1 else mc)


class Ride:
    def __init__(self):
        self.items = []

    def gather(self, shard, axis, layer, whole_prev):
        shape = shard.shape[:axis] + (N_DEV * shard.shape[axis],) + shard.shape[axis + 1:]
        self.items.append(("gather", shard, whole_prev, SDS(shape, shard.dtype), axis, layer, shard.shape[axis]))
        return self

    def exchange(self, grad, axis, layer, recv_prev):
        size = grad.shape[axis] // N_DEV
        shape = (N_DEV,) + grad.shape[:axis] + (size,) + grad.shape[axis + 1:]
        self.items.append(("exchange", grad, recv_prev, SDS(shape, grad.dtype), axis, layer, size))
        return self

    def operands(self):
        return [a for it in self.items for a in ([it[1]] if it[2] is None else [it[1], it[2]])]

    def out_shapes(self):
        return [it[3] for it in self.items]

    def aliases(self, first_in, first_out):
        out, pos = {}, first_in
        for t, it in enumerate(self.items):
            pos += 1
            if it[2] is not None:
                out[pos] = first_out + t
                pos += 1
        return out

    def scratch(self):
        n = len(self.items)
        return [pltpu.SemaphoreType.DMA((n, N_DEV - 1)), pltpu.SemaphoreType.DMA((n, N_DEV - 1)),
                pltpu.SemaphoreType.DMA((n,))]

    def plan(self, in_refs, out_refs, send_sems, recv_sems, local_sems):
        mx, my, mc = lax.axis_index("x"), lax.axis_index("y"), lax.axis_index("c")
        me, sibling = (mx, my, mc), (mx, my, 1 - mc)
        lin = lambda d: 4 * d[0] + 2 * d[1] + d[2]
        first, middle, last, pos = [], [], [], 0
        for t, (kind, _, prev, _, axis, layer, size) in enumerate(self.items):
            src = in_refs[pos].at[layer]
            pos += 1 if prev is None else 2

            def remote(act, k, src_fn, dst_fn, to, t=t):
                return lambda: getattr(pltpu.make_async_remote_copy(
                    src_ref=src_fn(), dst_ref=dst_fn(), send_sem=send_sems.at[t, k], recv_sem=recv_sems.at[t, k],
                    device_id=to, device_id_type=MESH), act)()

            def local(act, src_fn, dst_fn, t=t):
                return lambda: getattr(pltpu.make_async_copy(src_fn(), dst_fn(), local_sems.at[t]), act)()

            src_fn = lambda src=src: src
            if kind == "exchange":
                dst_fn = lambda t=t, layer=layer: out_refs[t].at[lin(me), layer]
                shard = lambda d, src=src, axis=axis, size=size: (lambda: _shard_of(src, axis - 1, lin(d), size))
                for act, phase in (("start", first), ("wait", last)):
                    phase.append(local(act, shard(me), dst_fn))
                    for k in range(1, N_DEV):
                        peer = _peer(k, mx, my, mc)
                        phase.append(remote(act, k - 1, shard(peer), dst_fn, peer))
                continue
            whole = out_refs[t].at[layer]
            slot = lambda d, whole=whole, axis=axis, size=size: (lambda: _shard_of(whole, axis - 1, lin(d), size))
            chips = [(1 - mx, my), (mx, 1 - my), (1 - mx, 1 - my)]
            own = [(0, sibling)] + [(1 + j, (*chip, mc)) for j, chip in enumerate(chips)]
            first.append(local("start", src_fn, slot(me)))
            first += [remote("start", k, src_fn, slot(me), to) for k, to in own]
            for j, chip in enumerate(chips):
                landed = slot((*chip, mc))
                middle += [remote("wait_recv", 1 + j, src_fn, landed, me), remote("start", 4 + j, landed, landed, sibling)]
                last.append(remote("wait_send", 4 + j, landed, landed, sibling))
            last.append(remote("wait_recv", 0, src_fn, slot(sibling), me))
            last += [remote("wait_recv", 4 + j, src_fn, slot((*chip, 1 - mc)), me) for j, chip in enumerate(chips)]
            last += [remote("wait_send", k, src_fn, slot(me), to) for k, to in own]
            last.append(local("wait", src_fn, slot(me)))
        return first, middle, last


def _ride_hooks(ride, grid, in_refs, out_refs, sems):
    step, total = 0, 1
    for d, g in enumerate(grid):
        step, total = step * g + pl.program_id(d), total * g

    def at(which, when):
        @pl.when(step == when)
        def _():
            for act in ride.plan(in_refs, out_refs, *sems)[which]:
                act()

    def start():
        at(0, 0)
        at(1, total // 2)

    def finish():
        at(2, total - 1)

    return start, finish


def hosted_call(body, ride, *, name, grid, in_specs, out_specs, out_shape, scratch_shapes, operands):
    if not ride:
        return pl.pallas_call(
            body, name=name, grid=grid, in_specs=in_specs, out_specs=out_specs, out_shape=out_shape,
            scratch_shapes=scratch_shapes, compiler_params=_cp(*["arbitrary"] * len(grid)))(*operands)
    n_in, n_out, n_scratch = len(in_specs), len(out_specs), len(scratch_shapes)
    ride_in = ride.operands()

    def carrying(*refs):
        ins, refs = refs[:n_in], refs[n_in:]
        ride_in_refs, refs = refs[:len(ride_in)], refs[len(ride_in):]
        outs, refs = refs[:n_out], refs[n_out:]
        ride_out_refs, refs = refs[:len(ride.items)], refs[len(ride.items):]
        start, finish = _ride_hooks(ride, grid, ride_in_refs, ride_out_refs, refs[n_scratch:])
        start()
        body(*ins, *outs, *refs[:n_scratch])
        finish()

    return pl.pallas_call(
        carrying, name=name, grid=grid, in_specs=list(in_specs) + [ANY] * len(ride_in),
        out_specs=list(out_specs) + [ANY] * len(ride.items), out_shape=list(out_shape) + ride.out_shapes(),
        scratch_shapes=list(scratch_shapes) + ride.scratch(), input_output_aliases=ride.aliases(n_in, n_out),
        compiler_params=_cp(*["arbitrary"] * len(grid)))(*operands, *ride_in)


def mm_w(a, w, layer, *, transpose_w, out_dtypes, name, extras=(), epilogue=None, ride=None,
         tm=512, tn=1024, tk=2048):
    m, k_dim = a.shape
    _, rows, cols = w.shape
    n = rows if transpose_w else cols
    assert k_dim == (cols if transpose_w else rows), (a.shape, w.shape)
    tm, tn, tk = _tile(m, tm, 8), _tile(n, tn), _tile(k_dim, tk)
    nk = k_dim // tk
    grid = (m // tm, n // tn, nk)
    n_extra, n_out = len(extras), len(out_dtypes)
    ride_in = ride.operands() if ride else []
    n_ride_out = len(ride.items) if ride else 0
    if epilogue is None:
        epilogue = lambda acc: (acc,)
    dot = _dot_nt if transpose_w else _dot

    def body(a_ref, w_ref, *rest):
        extra_refs, rest = rest[:n_extra], rest[n_extra:]
        ride_in_refs, rest = rest[:len(ride_in)], rest[len(ride_in):]
        out_refs, rest = rest[:n_out], rest[n_out:]
        ride_out_refs, scratch = rest[:n_ride_out], rest[n_ride_out:]
        if ride:
            start, finish = _ride_hooks(ride, grid, ride_in_refs, ride_out_refs, scratch[-3:])
            start()

        def write(acc):
            outs = epilogue(acc, *[r[...] for r in extra_refs])
            for o_ref, o in zip(out_refs, outs):
                o_ref[...] = o.astype(o_ref.dtype)

        if nk == 1:
            write(dot(a_ref[...], w_ref[...]))
        else:
            acc = scratch[0]
            k = pl.program_id(2)

            @pl.when(k == 0)
            def _():
                acc[...] = jnp.zeros_like(acc)

            acc[...] += dot(a_ref[...], w_ref[...])

            @pl.when(k == nk - 1)
            def _():
                write(acc[...])

        if ride:
            finish()

    if transpose_w:
        w_spec = pl.BlockSpec((None, tn, tk), lambda i, j, k: (layer, j, k))
    else:
        w_spec = pl.BlockSpec((None, tk, tn), lambda i, j, k: (layer, k, j))
    tile_spec = pl.BlockSpec((tm, tn), lambda i, j, k: (i, j))
    return pl.pallas_call(
        body, name=name, grid=grid,
        in_specs=[pl.BlockSpec((tm, tk), lambda i, j, k: (i, k)), w_spec] + [tile_spec] * n_extra
        + [ANY] * len(ride_in),
        out_specs=[tile_spec] * n_out + [ANY] * n_ride_out,
        out_shape=[SDS((m, n), dt) for dt in out_dtypes] + (ride.out_shapes() if ride else []),
        scratch_shapes=([pltpu.VMEM((tm, tn), F32)] if nk > 1 else []) + (ride.scratch() if ride else []),
        input_output_aliases=ride.aliases(2 + n_extra, n_out) if ride else {},
        compiler_params=_cp("arbitrary", "arbitrary", "arbitrary") if ride
        else _cp("parallel", "parallel", "arbitrary"),
    )(a, w, *extras, *ride_in)


def mm_dw(a, b, buf, layer, n_layers, name, ride=None, tm=512, tn=1024, tk=2048):
    tokens, m = a.shape
    _, n = b.shape
    tm, tn, tk = _tile(m, tm), _tile(n, tn), _tile(tokens, tk)
    nk = tokens // tk
    grid = (m // tm, n // tn, nk)
    n_buf = 0 if buf is None else 1
    ride_in = ride.operands() if ride else []
    n_ride_out = len(ride.items) if ride else 0

    def body(a_ref, b_ref, *rest):
        rest = rest[n_buf:]
        ride_in_refs, rest = rest[:len(ride_in)], rest[len(ride_in):]
        o_ref, rest = rest[0], rest[1:]
        ride_out_refs, scratch = rest[:n_ride_out], rest[n_ride_out:]
        acc = scratch[0]
        if ride:
            start, finish = _ride_hooks(ride, grid, ride_in_refs, ride_out_refs, scratch[-3:])
            start()
        k = pl.program_id(2)

        @pl.when(k == 0)
        def _():
            acc[...] = jnp.zeros_like(acc)

        acc[...] += _dot_tn(a_ref[...], b_ref[...])

        @pl.when(k == nk - 1)
        def _():
            o_ref[...] = acc[...].astype(BF16)

        if ride:
            finish()

    aliases = {} if buf is None else {2: 0}
    if ride:
        aliases.update(ride.aliases(2 + n_buf, 1))
    outs = pl.pallas_call(
        body, name=name, grid=grid,
        in_specs=[pl.BlockSpec((tk, tm), lambda i, j, k: (k, i)),
                  pl.BlockSpec((tk, tn), lambda i, j, k: (k, j))] + [ANY] * (n_buf + len(ride_in)),
        out_specs=[pl.BlockSpec((None, tm, tn), lambda i, j, k: (layer, i, j))] + [ANY] * n_ride_out,
        out_shape=[SDS((n_layers, m, n), BF16)] + (ride.out_shapes() if ride else []),
        scratch_shapes=[pltpu.VMEM((tm, tn), F32)] + (ride.scratch() if ride else []),
        input_output_aliases=aliases,
        compiler_params=_cp("arbitrary", "arbitrary", "arbitrary") if ride
        else _cp("parallel", "parallel", "arbitrary"),
    )(a, b, *([] if buf is None else [buf]), *ride_in)
    return outs if ride else outs[0]


def rms_fwd(h, g, name):
    t, d = h.shape
    tt = _tile(t, 512, 8)

    def body(h_ref, g_ref, o_ref):
        x = h_ref[...]
        r = lax.rsqrt(jnp.mean(x * x, axis=-1, keepdims=True) + EPS)
        o_ref[...] = (x * r * g_ref[...]).astype(BF16)

    row = pl.BlockSpec((tt, d), lambda i: (i, 0))
    return pl.pallas_call(
        body, name=name, grid=(t // tt,),
        in_specs=[row, pl.BlockSpec((1, d), lambda i: (0, 0))], out_specs=row,
        out_shape=SDS((t, d), BF16), compiler_params=_cp("parallel"),
    )(h, g.reshape(1, d))


def _rms_bwd_math(x, g, dhn):
    r = lax.rsqrt(jnp.mean(x * x, axis=-1, keepdims=True) + EPS)
    xn = x * r
    dxn = dhn * g
    dx = r * (dxn - xn * jnp.mean(dxn * xn, axis=-1, keepdims=True))
    return dx, jnp.sum(dhn * xn, axis=0, keepdims=True)


def rms_bwd(h, g, dhn, dres, name):
    t, d = h.shape
    tt = _tile(t, 256, 8)

    def body(h_ref, g_ref, dhn_ref, dres_ref, dh_ref, dhb_ref, dg_ref):
        @pl.when(pl.program_id(0) == 0)
        def _():
            dg_ref[...] = jnp.zeros_like(dg_ref)

        dx, dg = _rms_bwd_math(h_ref[...], g_ref[...], dhn_ref[...].astype(F32))
        dh = dres_ref[...] + dx
        dh_ref[...] = dh
        dhb_ref[...] = dh.astype(BF16)
        dg_ref[...] += dg

    row = pl.BlockSpec((tt, d), lambda i: (i, 0))
    vec = pl.BlockSpec((1, d), lambda i: (0, 0))
    return pl.pallas_call(
        body, name=name, grid=(t // tt,),
        in_specs=[row, vec, row, row], out_specs=[row, row, vec],
        out_shape=[SDS((t, d), F32), SDS((t, d), BF16), SDS((1, d), F32)],
        compiler_params=_cp("arbitrary"),
    )(h, g.reshape(1, d), dhn, dres)


def loss_head(h, g, target, name):
    t, d = h.shape
    tt = _tile(t, 256, 8)

    def body(h_ref, g_ref, t_ref, dh_ref, dg_ref, loss_ref):
        @pl.when(pl.program_id(0) == 0)
        def _():
            dg_ref[...] = jnp.zeros_like(dg_ref)
            loss_ref[...] = jnp.zeros_like(loss_ref)

        x, gain = h_ref[...], g_ref[...]
        r = lax.rsqrt(jnp.mean(x * x, axis=-1, keepdims=True) + EPS)
        err = x * r * gain - t_ref[...]
        loss_ref[...] += 0.5 * jnp.sum(jnp.mean(err * err, axis=-1, keepdims=True))
        dx, dg = _rms_bwd_math(x, gain, err * (1.0 / d))
        dh_ref[...] = dx
        dg_ref[...] += dg

    row = pl.BlockSpec((tt, d), lambda i: (i, 0))
    vec = pl.BlockSpec((1, d), lambda i: (0, 0))
    return pl.pallas_call(
        body, name=name, grid=(t // tt,),
        in_specs=[row, vec, row],
        out_specs=[row, vec, pl.BlockSpec((8, LANES), lambda i: (0, 0))],
        out_shape=[SDS((t, d), F32), SDS((1, d), F32), SDS((8, LANES), F32)],
        compiler_params=_cp("arbitrary"),
    )(h, g.reshape(1, d), target)


def ple_bwd(dh, gz, pp, name):
    t, d = dh.shape
    tt = _tile(t, 256, 8)

    def body(dh_ref, gz_ref, pp_ref, dpp_ref, dgz_ref):
        g = dh_ref[...]
        gate = jax.nn.sigmoid(gz_ref[...].astype(F32))
        dpp_ref[...] = (g * gate).astype(BF16)
        dgz_ref[...] = (g * pp_ref[...].astype(F32) * gate * (1.0 - gate)).astype(BF16)

    row = pl.BlockSpec((tt, d), lambda i: (i, 0))
    return pl.pallas_call(
        body, name=name, grid=(t // tt,), in_specs=[row, row, row], out_specs=[row, row],
        out_shape=[SDS((t, d), BF16), SDS((t, d), BF16)], compiler_params=_cp("parallel"),
    )(dh, gz, pp)


def _layer_norm_parts(y1):
    mu = jnp.mean(y1, axis=-1, keepdims=True)
    dlt = y1 - mu
    rstd = lax.rsqrt(jnp.mean(dlt * dlt, axis=-1, keepdims=True) + EPS)
    return dlt * rstd, rstd


def _fill_shifted(buf, shifted, tt):
    for r in range(1, 8):
        shifted[r - 1] = buf[pl.ds(r, tt + HALO_A - 8), :]


def _rows_from(buf, shifted, start, tt, cs):
    q, r = divmod(start, 8)
    if r == 0:
        return buf[pl.ds(8 * q, tt), cs]
    return shifted[r - 1, pl.ds(8 * q, tt), cs]


def conf_fwd(u, cw, cb, lg, lb, name):
    t = u.shape[0]
    ka, c = cw.shape
    tt = _tile(t, 256, HALO_A)
    hb = tt // HALO_A
    off = HALO_A - (ka - 1)
    grp = _tile(tt, CONV_ROWS, 8)

    def body(av_ref, ag_ref, pv_ref, pg_ref, cw_ref, cb_ref, lg_ref, lb_ref, ya_ref, ya1_ref, buf, shifted):
        live = (pl.program_id(0) > 0).astype(F32)
        buf[0:HALO_A, :] = pv_ref[...] * jax.nn.sigmoid(pg_ref[...]) * live
        buf[HALO_A:, :] = av_ref[...] * jax.nn.sigmoid(ag_ref[...])
        _fill_shifted(buf, shifted, tt)
        for c0 in range(0, c, LANES):
            cs = pl.ds(c0, LANES)
            for g0 in range(0, tt, grp):
                acc = jnp.broadcast_to(cb_ref[:, cs], (grp, LANES))
                for k in range(ka):
                    acc = acc + cw_ref[k:k + 1, cs] * _rows_from(buf, shifted, off + k + g0, grp, cs)
                ya1_ref[pl.ds(g0, grp), cs] = acc
        yn, _ = _layer_norm_parts(ya1_ref[...])
        y2 = yn * lg_ref[...] + lb_ref[...]
        ya_ref[...] = (y2 * jax.nn.sigmoid(y2)).astype(BF16)

    cur = lambda col: pl.BlockSpec((tt, c), lambda i: (i, col))
    prev = lambda col: pl.BlockSpec((HALO_A, c), lambda i: (jnp.maximum(i * hb - 1, 0), col))
    vec = pl.BlockSpec((1, c), lambda i: (0, 0))
    return pl.pallas_call(
        body, name=name, grid=(t // tt,),
        in_specs=[cur(0), cur(1), prev(0), prev(1), pl.BlockSpec((ka, c), lambda i: (0, 0)), vec, vec, vec],
        out_specs=[pl.BlockSpec((tt, c), lambda i: (i, 0))] * 2,
        out_shape=[SDS((t, c), BF16), SDS((t, c), F32)],
        scratch_shapes=[pltpu.VMEM((tt + HALO_A, c), F32), pltpu.VMEM((7, tt + HALO_A - 8, c), F32)],
        compiler_params=_cp("parallel"),
    )(u, u, u, u, cw, cb, lg, lb)


def conf_bwd(u, ya1, dcat, cw, lg, lb, name, ride=None):
    t = u.shape[0]
    ka, c = cw.shape
    tt = _tile(t, 256, HALO_A)
    nt, hb = t // tt, tt // HALO_A
    off = HALO_A - (ka - 1)
    grp = _tile(tt, CONV_ROWS, 8)

    def body(av_ref, ag_ref, pv_ref, pg_ref, y1_ref, dya_ref, cw_ref, lg_ref, lb_ref,
             du_ref, pgrad_ref, ybuf, dbuf, carry, yshift, dshift):
        i = pl.program_id(0)

        @pl.when(i == 0)
        def _():
            carry[...] = jnp.zeros_like(carry)
            pgrad_ref[...] = jnp.zeros_like(pgrad_ref)

        live = (i < nt - 1).astype(F32)
        av = av_ref[...]
        sg = jax.nn.sigmoid(ag_ref[...])
        ybuf[0:HALO_A, :] = pv_ref[...] * jax.nn.sigmoid(pg_ref[...]) * live
        ybuf[HALO_A:, :] = av * sg
        yn, rstd = _layer_norm_parts(y1_ref[...])
        gain = lg_ref[...]
        y2 = yn * gain + lb_ref[...]
        s2 = jax.nn.sigmoid(y2)
        dy2 = dya_ref[...].astype(F32) * (s2 * (1.0 + y2 * (1.0 - s2)))
        pgrad_ref[ka + 1:ka + 2, :] += jnp.sum(dy2 * yn, axis=0, keepdims=True)
        pgrad_ref[ka + 2:ka + 3, :] += jnp.sum(dy2, axis=0, keepdims=True)
        dyn = dy2 * gain
        dy1 = rstd * (dyn - jnp.mean(dyn, axis=-1, keepdims=True)
                      - yn * jnp.mean(dyn * yn, axis=-1, keepdims=True))
        pgrad_ref[ka:ka + 1, :] += jnp.sum(dy1, axis=0, keepdims=True)
        dbuf[0:tt, :] = dy1
        dbuf[tt:, :] = carry[...]
        carry[...] = dbuf[0:HALO_A, :]
        _fill_shifted(ybuf, yshift, tt)
        _fill_shifted(dbuf, dshift, tt)
        groups = range(0, tt, grp)
        for c0 in range(0, c, LANES):
            cs = pl.ds(c0, LANES)
            accs = [jnp.zeros((grp, LANES), F32) for _ in groups]
            for k in range(ka):
                tap = cw_ref[k:k + 1, cs]
                dw_k = jnp.zeros((8, LANES), F32)
                for gi, g0 in enumerate(groups):
                    prod = dbuf[pl.ds(g0, grp), cs] * _rows_from(ybuf, yshift, off + k + g0, grp, cs)
                    for r0 in range(0, grp, 8):
                        dw_k = dw_k + prod[r0:r0 + 8]
                    accs[gi] = accs[gi] + tap * _rows_from(dbuf, dshift, ka - 1 - k + g0, grp, cs)
                pgrad_ref[k:k + 1, cs] += jnp.sum(dw_k, axis=0, keepdims=True)
            for acc, g0 in zip(accs, groups):
                sgc, avc = sg[g0:g0 + grp, c0:c0 + LANES], av[g0:g0 + grp, c0:c0 + LANES]
                du_ref[pl.ds(g0, grp), cs] = (acc * sgc).astype(BF16)
                du_ref[pl.ds(g0, grp), pl.ds(c + c0, LANES)] = (acc * avc * sgc * (1.0 - sgc)).astype(BF16)

    cur = lambda col: pl.BlockSpec((tt, c), lambda i: (nt - 1 - i, col))
    prev = lambda col: pl.BlockSpec((HALO_A, c), lambda i: (jnp.maximum((nt - 1 - i) * hb - 1, 0), col))
    vec = pl.BlockSpec((1, c), lambda i: (0, 0))
    return hosted_call(
        body, ride, name=name, grid=(nt,),
        in_specs=[cur(0), cur(1), prev(0), prev(1), cur(0), cur(0),
                  pl.BlockSpec((ka, c), lambda i: (0, 0)), vec, vec],
        out_specs=[pl.BlockSpec((tt, 2 * c), lambda i: (nt - 1 - i, 0)),
                   pl.BlockSpec((ka + 3, c), lambda i: (0, 0))],
        out_shape=[SDS((t, 2 * c), BF16), SDS((ka + 3, c), F32)],
        scratch_shapes=[pltpu.VMEM((tt + HALO_A, c), F32), pltpu.VMEM((tt + HALO_A, c), F32),
                        pltpu.VMEM((HALO_A, c), F32), pltpu.VMEM((7, tt + HALO_A - 8, c), F32),
                        pltpu.VMEM((7, tt + HALO_A - 8, c), F32)],
        operands=(u, u, u, u, ya1, dcat, cw, lg, lb))


def _rg_gates(xc, wa_ref, ba, wx_ref, bx, sp, nh, hd):
    parts = []
    for h in range(nh):
        hs = slice(h * hd, (h + 1) * hd)
        xh = xc[:, hs].astype(BF16)
        r = jax.nn.sigmoid(_dot(xh, wa_ref[h]) + ba[:, hs])
        ig = jax.nn.sigmoid(_dot(xh, wx_ref[h]) + bx[:, hs])
        log_a = -RG_C * r * sp[:, hs]
        parts.append((r, ig, jnp.exp(log_a), jnp.sqrt(-_expm1(2.0 * log_a))))
    return parts


def _conv_b(xbuf, bw_ref, bb, tt, kb):
    off = HALO_B - (kb - 1)
    xc = bb
    for k in range(kb):
        xc = xc + bw_ref[k:k + 1, :] * xbuf[pl.ds(off + k, tt), :]
    return xc


def rglru_fwd(u, bw, bb, wa, ba, wx, bx, lam, name):
    t = u.shape[0]
    kb, c = bw.shape
    nh, hd, _ = wa.shape
    tt = _tile(t, 256, HALO_B)
    hb = tt // HALO_B

    def body(xr_ref, gr_ref, px_ref, bw_ref, bb_ref, wa_ref, ba_ref, wx_ref, bx_ref, lam_ref,
             yb_ref, hs_ref, xbuf, a_s, u_s, hc):
        i = pl.program_id(0)

        @pl.when(i == 0)
        def _():
            hc[...] = jnp.zeros_like(hc)

        xbuf[0:HALO_B, :] = px_ref[...] * (i > 0).astype(F32)
        xbuf[HALO_B:, :] = xr_ref[...]
        xc = _conv_b(xbuf, bw_ref, bb_ref[...], tt, kb)
        sp = _softplus(-lam_ref[...])
        gates = _rg_gates(xc, wa_ref, ba_ref[...], wx_ref, bx_ref[...], sp, nh, hd)
        for h, (_, ig, a, mult) in enumerate(gates):
            hs = slice(h * hd, (h + 1) * hd)
            a_s[:, hs] = a
            u_s[:, hs] = mult * ig * xc[:, hs]

        def step(g, hcur):
            base = pl.multiple_of(g * 8, 8)
            for j in range(8):
                hcur = a_s[pl.ds(base + j, 1), :] * hcur + u_s[pl.ds(base + j, 1), :]
                hs_ref[pl.ds(base + j, 1), :] = hcur
            return hcur

        hc[...] = lax.fori_loop(0, tt // 8, step, hc[...])
        yb_ref[...] = (hs_ref[...] * _gelu(gr_ref[...])).astype(BF16)

    cur = lambda col: pl.BlockSpec((tt, c), lambda i: (i, col))
    vec = pl.BlockSpec((1, c), lambda i: (0, 0))
    wsp = pl.BlockSpec((nh, hd, hd), lambda i: (0, 0, 0))
    return pl.pallas_call(
        body, name=name, grid=(t // tt,),
        in_specs=[cur(2), cur(3), pl.BlockSpec((HALO_B, c), lambda i: (jnp.maximum(i * hb - 1, 0), 2)),
                  pl.BlockSpec((kb, c), lambda i: (0, 0)), vec, wsp, vec, wsp, vec, vec],
        out_specs=[pl.BlockSpec((tt, c), lambda i: (i, 0))] * 2,
        out_shape=[SDS((t, c), BF16), SDS((t, c), F32)],
        scratch_shapes=[pltpu.VMEM((tt + HALO_B, c), F32), pltpu.VMEM((tt, c), F32),
                        pltpu.VMEM((tt, c), F32), pltpu.VMEM((1, c), F32)],
        compiler_params=_cp("arbitrary"),
    )(u, u, u, bw, bb, wa, ba, wx, bx, lam)


def rglru_bwd(u, hs_all, dcat, bw, bb, wa, ba, wx, bx, lam, name):
    t = u.shape[0]
    kb, c = bw.shape
    nh, hd, _ = wa.shape
    tt = _tile(t, 256, HALO_B)
    nt, hb = t // tt, tt // HALO_B
    off = HALO_B - (kb - 1)

    def body(xr_ref, gr_ref, px_ref, hs_ref, ph_ref, dyb_ref, bw_ref, bb_ref, wa_ref, ba_ref, wx_ref, bx_ref,
             lam_ref, du_ref, pgrad_ref, dwa_ref, dwx_ref, xbuf, hbuf, a_s, g_s, dxbuf, cg, cdx):
        i = pl.program_id(0)

        @pl.when(i == 0)
        def _():
            cg[...] = jnp.zeros_like(cg)
            cdx[...] = jnp.zeros_like(cdx)
            pgrad_ref[...] = jnp.zeros_like(pgrad_ref)
            dwa_ref[...] = jnp.zeros_like(dwa_ref)
            dwx_ref[...] = jnp.zeros_like(dwx_ref)

        live = (i < nt - 1).astype(F32)
        xbuf[0:HALO_B, :] = px_ref[...] * live
        xbuf[HALO_B:, :] = xr_ref[...]
        hbuf[0:HALO_B, :] = ph_ref[...] * live
        hbuf[HALO_B:, :] = hs_ref[...]
        xc = _conv_b(xbuf, bw_ref, bb_ref[...], tt, kb)
        lam_v = lam_ref[...]
        sp = _softplus(-lam_v)
        gates = _rg_gates(xc, wa_ref, ba_ref[...], wx_ref, bx_ref[...], sp, nh, hd)
        gr = gr_ref[...]
        dyb = dyb_ref[...].astype(F32)
        g_s[...] = dyb * _gelu(gr)
        for h, (_, _, a, _) in enumerate(gates):
            a_s[:, h * hd:(h + 1) * hd] = a

        def step(g, carry):
            base = pl.multiple_of((tt // 8 - 1 - g) * 8, 8)
            for j in range(7, -1, -1):
                gt = g_s[pl.ds(base + j, 1), :] + carry
                g_s[pl.ds(base + j, 1), :] = gt
                carry = a_s[pl.ds(base + j, 1), :] * gt
            return carry

        cg[...] = lax.fori_loop(0, tt // 8, step, cg[...])
        hprev = hbuf[pl.ds(HALO_B - 1, tt), :]
        gfull = g_s[...]
        for h, (r, ig, a, mult) in enumerate(gates):
            hs = slice(h * hd, (h + 1) * hd)
            g, xch = gfull[:, hs], xc[:, hs]
            d_la = g * hprev[:, hs] * a - g * ig * xch * (a * a) / mult
            d_ig = g * mult * xch
            d_ra = d_la * (-RG_C * sp[:, hs]) * r * (1.0 - r)
            d_ia = d_ig * ig * (1.0 - ig)
            dsp = jnp.sum(d_la * (-RG_C) * r, axis=0, keepdims=True)
            pgrad_ref[kb + 1:kb + 2, hs] += jnp.sum(d_ra, axis=0, keepdims=True)
            pgrad_ref[kb + 2:kb + 3, hs] += jnp.sum(d_ia, axis=0, keepdims=True)
            pgrad_ref[kb + 3:kb + 4, hs] += dsp * (-jax.nn.sigmoid(-lam_v[:, hs]))
            xh, d_ra_b, d_ia_b = xch.astype(BF16), d_ra.astype(BF16), d_ia.astype(BF16)
            dwa_ref[h] += _dot_tn(xh, d_ra_b)
            dwx_ref[h] += _dot_tn(xh, d_ia_b)
            dxbuf[0:tt, hs] = g * mult * ig + _dot_nt(d_ra_b, wa_ref[h]) + _dot_nt(d_ia_b, wx_ref[h])
        dxbuf[tt:, :] = cdx[...]
        cdx[...] = dxbuf[0:HALO_B, :]
        d_xc = dxbuf[0:tt, :]
        pgrad_ref[kb:kb + 1, :] += jnp.sum(d_xc, axis=0, keepdims=True)
        d_xr = jnp.zeros((tt, c), F32)
        for k in range(kb):
            d_xr = d_xr + bw_ref[k:k + 1, :] * dxbuf[pl.ds(kb - 1 - k, tt), :]
            pgrad_ref[k:k + 1, :] += jnp.sum(d_xc * xbuf[pl.ds(off + k, tt), :], axis=0, keepdims=True)
        du_ref[:, 0:c] = d_xr.astype(BF16)
        du_ref[:, c:2 * c] = (dyb * hs_ref[...] * _gelu_grad(gr)).astype(BF16)

    cur = lambda col: pl.BlockSpec((tt, c), lambda i: (nt - 1 - i, col))
    prev = lambda col: pl.BlockSpec((HALO_B, c), lambda i: (jnp.maximum((nt - 1 - i) * hb - 1, 0), col))
    vec = pl.BlockSpec((1, c), lambda i: (0, 0))
    wsp = pl.BlockSpec((nh, hd, hd), lambda i: (0, 0, 0))
    return pl.pallas_call(
        body, name=name, grid=(nt,),
        in_specs=[cur(2), cur(3), prev(2), cur(0), prev(0), cur(1),
                  pl.BlockSpec((kb, c), lambda i: (0, 0)), vec, wsp, vec, wsp, vec, vec],
        out_specs=[pl.BlockSpec((tt, 2 * c), lambda i: (nt - 1 - i, 0)),
                   pl.BlockSpec((kb + 4, c), lambda i: (0, 0)), wsp, wsp],
        out_shape=[SDS((t, 2 * c), BF16), SDS((kb + 4, c), F32), SDS((nh, hd, hd), F32), SDS((nh, hd, hd), F32)],
        scratch_shapes=[pltpu.VMEM((tt + HALO_B, c), F32), pltpu.VMEM((tt + HALO_B, c), F32),
                        pltpu.VMEM((tt, c), F32), pltpu.VMEM((tt, c), F32), pltpu.VMEM((tt + HALO_B, c), F32),
                        pltpu.VMEM((1, c), F32), pltpu.VMEM((HALO_B, c), F32)],
        compiler_params=_cp("arbitrary"),
    )(u, u, u, hs_all, hs_all, dcat, bw, bb, wa, ba, wx, bx, lam)


MASS_CUTOFF = 100.0


def attn_fwd(qkv, nh, name):
    t = qkv.shape[0]
    dh = qkv.shape[1] // (3 * nh)
    bq = _tile(t, 256)
    scale = 1.0 / math.sqrt(dh)
    hp = 2 if nh % 2 == 0 else 1

    def body(q_ref, k_ref, v_ref, o_ref, s_ref, start_ref):
        pair, qi = pl.program_id(0), pl.program_id(1)
        row = lax.broadcasted_iota(jnp.int32, (bq, bq), 0)
        col = lax.broadcasted_iota(jnp.int32, (bq, bq), 1)
        tri = (row >= col).astype(BF16)

        def block(hh, kb, c, acc, mask):
            ks = pl.ds(pl.multiple_of(kb * bq, bq), bq)
            hs = pl.ds(hh * dh, dh)
            z = _dot_nt(q_ref[:, hs], k_ref[ks, hs]) * scale
            sp = _softplus(z)
            if mask is not None:
                sp = jnp.where(mask, sp, 0.0)
            lw = z - c - _dot(sp.astype(BF16), tri)
            if mask is not None:
                lw = jnp.where(mask, lw, -1e30)
            acc = acc + _dot(jnp.exp(lw).astype(BF16), v_ref[ks, hs])
            return c + jnp.sum(sp, axis=1, keepdims=True), acc

        state = []
        for hh in range(hp):
            c, acc = block(hh, qi, jnp.zeros((bq, 1), F32), jnp.zeros((bq, dh), F32), col < row)
            state.append(block(hh, jnp.maximum(qi - 1, 0), c, acc, qi > 0))

        def more(st):
            return jnp.logical_and(st[0] >= 0, jnp.min(st[1]) < MASS_CUTOFF)

        for hh, (c, acc) in enumerate(state):
            def step(st, hh=hh):
                c_new, acc_new = block(hh, st[0], st[1], st[2], None)
                return st[0] - 1, c_new, acc_new

            kb, c, acc = lax.while_loop(more, step, (qi - 2, c, acc))
            o_ref[:, pl.ds(hh * dh, dh)] = acc.astype(BF16)
            s_ref[hh] = jnp.broadcast_to(c, (bq, LANES))
            start_ref[hp * pair + hh, qi] = (kb + 1).astype(F32)

    whole = lambda base: pl.BlockSpec((t, hp * dh), lambda h, qi: (0, base + h))
    return pl.pallas_call(
        body, name=name, grid=(nh // hp, t // bq),
        in_specs=[pl.BlockSpec((bq, hp * dh), lambda h, qi: (qi, h)), whole(nh // hp), whole(2 * nh // hp)],
        out_specs=[pl.BlockSpec((bq, hp * dh), lambda h, qi: (qi, h)),
                   pl.BlockSpec((hp, bq, LANES), lambda h, qi: (h, qi, 0)),
                   pl.BlockSpec(memory_space=pltpu.SMEM)],
        out_shape=[SDS((t, nh * dh), BF16), SDS((nh, t, LANES), F32), SDS((nh, t // bq), F32)],
        compiler_params=_cp("arbitrary", "arbitrary"),
    )(qkv, qkv, qkv)


def attn_bwd(qkv, do, s_tot, start, nh, name, ride=None):
    t = qkv.shape[0]
    dh = qkv.shape[1] // (3 * nh)
    bq = _tile(t, 256)
    scale = 1.0 / math.sqrt(dh)
    hp = 2 if nh % 2 == 0 else 1

    def body(start_ref, q_ref, k_ref, v_ref, do_ref, s_ref, dq_ref, dk_ref, dv_ref):
        pair, qi = pl.program_id(0), pl.program_id(1)

        @pl.when(qi == 0)
        def _():
            dk_ref[...] = jnp.zeros_like(dk_ref)
            dv_ref[...] = jnp.zeros_like(dv_ref)

        row = lax.broadcasted_iota(jnp.int32, (bq, bq), 0)
        col = lax.broadcasted_iota(jnp.int32, (bq, bq), 1)
        tri_suffix = (row >= col).astype(BF16)
        tri_prefix = (row <= col).astype(BF16)

        def block(hh, kb, psp, pg, dq, mask):
            ks = pl.ds(pl.multiple_of(kb * bq, bq), bq)
            hs = pl.ds(hh * dh, dh)
            q, do_b, k_b, v_b = q_ref[:, hs], do_ref[:, hs], k_ref[ks, hs], v_ref[ks, hs]
            z = _dot_nt(q, k_b) * scale
            sp_all = _softplus(z)
            sp = sp_all if mask is None else jnp.where(mask, sp_all, 0.0)
            psp = psp + jnp.sum(sp, axis=1, keepdims=True)
            lw = z - (s_ref[hh, :, 0:1] - psp) - _dot(sp.astype(BF16), tri_suffix)
            if mask is not None:
                lw = jnp.where(mask, lw, -1e30)
            a = jnp.exp(lw)
            g = _dot_nt(do_b, v_b) * a
            dz = g - (pg + _dot(g.astype(BF16), tri_prefix)) * jnp.exp(z - sp_all)
            if mask is not None:
                dz = jnp.where(mask, dz, 0.0)
            dzs = (dz * scale).astype(BF16)
            dk_ref[ks, hs] += _dot_tn(dzs, q)
            dv_ref[ks, hs] += _dot_tn(a.astype(BF16), do_b)
            return psp, pg + jnp.sum(g, axis=1, keepdims=True), dq + _dot(dzs, k_b)

        state = []
        for hh in range(hp):
            first = jnp.clip(start_ref[hp * pair + hh, qi].astype(jnp.int32), 0, qi)
            init = (jnp.zeros((bq, 1), F32), jnp.zeros((bq, 1), F32), jnp.zeros((bq, dh), F32))
            state.append((first,) + lax.fori_loop(
                first, jnp.maximum(qi - 1, first),
                lambda kb, cr, hh=hh: block(hh, kb, cr[0], cr[1], cr[2], None), init))
        for hh, (first, psp, pg, dq) in enumerate(state):
            psp, pg, dq = block(hh, jnp.maximum(qi - 1, 0), psp, pg, dq, jnp.logical_and(qi > 0, first < qi))
            _, _, dq = block(hh, qi, psp, pg, dq, col < row)
            dq_ref[:, pl.ds(hh * dh, dh)] = dq.astype(BF16)

    once = dict(pipeline_mode=pl.Buffered(1))
    whole = lambda base: pl.BlockSpec((t, hp * dh), lambda h, qi: (0, base + h), **once)
    qblk = pl.BlockSpec((bq, hp * dh), lambda h, qi: (qi, h))
    acc = pl.BlockSpec((t, hp * dh), lambda h, qi: (0, h))
    return hosted_call(
        body, ride, name=name, grid=(nh // hp, t // bq),
        in_specs=[pl.BlockSpec(memory_space=pltpu.SMEM), qblk, whole(nh // hp), whole(2 * nh // hp), qblk,
                  pl.BlockSpec((hp, bq, LANES), lambda h, qi: (h, qi, 0))],
        out_specs=[qblk, acc, acc],
        out_shape=[SDS((t, nh * dh), BF16), SDS((t, nh * dh), F32), SDS((t, nh * dh), F32)],
        scratch_shapes=[], operands=(start, qkv, qkv, qkv, do, s_tot))


def all_gather(x, name, axis=None, first_layer_only=False):
    if axis is None:
        out_shape = (N_DEV,) + x.shape
    else:
        out_shape = x.shape[:axis] + (N_DEV * x.shape[axis],) + x.shape[axis + 1:]

    def body(x_full_ref, out_full_ref, send_sems, recv_sems, local_sem):
        mx, my, mc = lax.axis_index("x"), lax.axis_index("y"), lax.axis_index("c")
        me, sibling = (mx, my, mc), (mx, my, 1 - mc)
        chips = [(1 - mx, my), (mx, 1 - my), (1 - mx, 1 - my)]
        x_ref = x_full_ref.at[0] if first_layer_only else x_full_ref
        out_ref = out_full_ref.at[0] if first_layer_only else out_full_ref
        ax = axis - 1 if first_layer_only else axis

        def slot(px, py, pc):
            return _shard_of(out_ref, ax, 4 * px + 2 * py + pc, None if axis is None else x.shape[axis])

        def copy(k, block, to, src=None):
            return pltpu.make_async_remote_copy(
                src_ref=slot(*block) if src is None else src, dst_ref=slot(*block),
                send_sem=send_sems.at[k], recv_sem=recv_sems.at[k], device_id=to, device_id_type=MESH)

        mine = pltpu.make_async_copy(x_ref, slot(*me), local_sem)
        mine.start()
        first = [copy(0, me, sibling, src=x_ref)]
        first += [copy(1 + j, me, (*chip, mc), src=x_ref) for j, chip in enumerate(chips)]
        for cp in first:
            cp.start()
        passed = [copy(4 + j, (*chip, mc), sibling) for j, chip in enumerate(chips)]
        for j, chip in enumerate(chips):
            copy(1 + j, (*chip, mc), me).wait_recv()
            passed[j].start()
        copy(0, sibling, me).wait_recv()
        for j, chip in enumerate(chips):
            copy(4 + j, (*chip, 1 - mc), me).wait_recv()
        for cp in first + passed:
            cp.wait_send()
        mine.wait()

    return pl.pallas_call(
        body, name=name, out_shape=SDS(out_shape, x.dtype), in_specs=[ANY], out_specs=ANY,
        scratch_shapes=[pltpu.SemaphoreType.DMA((7,)), pltpu.SemaphoreType.DMA((7,)), pltpu.SemaphoreType.DMA],
    )(x)


def reduce_adamw(gs, w, m, v, name):
    s, r, c = gs.shape
    tr = _tile(r, max(16, (128 * 1024) // c), 16)

    def body(gs_ref, w_ref, m_ref, v_ref, g_out, d_out, m_out, v_out):
        g = gs_ref[0].astype(F32)
        for j in range(1, s):
            g = g + gs_ref[j].astype(F32)
        m_new = ADAM_B1 * m_ref[...] + (1.0 - ADAM_B1) * g
        v_new = ADAM_B2 * v_ref[...] + (1.0 - ADAM_B2) * (g * g)
        m_hat = m_new / (1.0 - ADAM_B1 ** ADAM_STEP)
        v_hat = v_new / (1.0 - ADAM_B2 ** ADAM_STEP)
        g_out[...] = g
        d_out[...] = -ADAM_LR * (m_hat / (jnp.sqrt(v_hat) + ADAM_EPS) + ADAM_WD * w_ref[...])
        m_out[...] = m_new
        v_out[...] = v_new

    row = pl.BlockSpec((tr, c), lambda i: (i, 0))
    return pl.pallas_call(
        body, name=name, grid=(r // tr,),
        in_specs=[pl.BlockSpec((s, tr, c), lambda i: (0, i, 0)), row, row, row], out_specs=[row] * 4,
        out_shape=[SDS((r, c), F32)] * 4, compiler_params=_cp("parallel"),
    )(gs, w, m, v)


BIG = {
    "w_in_rec": 2, "w_out_rec": 1, "w_qkv": 2, "w_o_attn": 1,
    "w_mlp_up": 2, "w_mlp_down": 1, "w_ple_proj": 2, "w_ple_gate": 1,
}
REPLICATED = ["norm_mix_g", "norm_mlp_g", "norm_ple_g", "norm_f_g", "conv_a_b", "ln_a_g", "ln_a_b", "conv_b_b",
              "w_rg_a", "b_rg_a", "w_rg_x", "b_rg_x", "rg_lambda"]
CONV_W = ["conv_a_w", "conv_b_w"]
WEIGHTS = ["norm_mix_g", "norm_mlp_g", "norm_ple_g", "norm_f_g", "w_in_rec", "conv_a_w", "conv_a_b", "ln_a_g",
           "ln_a_b", "conv_b_w", "conv_b_b", "w_rg_a", "b_rg_a", "w_rg_x", "b_rg_x", "rg_lambda", "w_out_rec",
           "w_qkv", "w_o_attn", "w_mlp_up", "w_mlp_down", "w_ple_proj", "w_ple_gate"]


def _pack(arrays):
    flat = jnp.concatenate([a.reshape(-1) for a in arrays])
    pad = (-flat.shape[0]) % (8 * LANES)
    return jnp.pad(flat, (0, pad)).reshape(-1, LANES)


def _unpack(packed, shapes, lead=()):
    flat = packed.reshape(lead + (-1,))
    out, pos = [], 0
    for shp in shapes:
        size = math.prod(shp)
        out.append(flat[..., pos:pos + size].reshape(lead + tuple(shp)))
        pos += size
    return out


def _step(x, p, loss_target, w, mom, var):
    dev = 4 * lax.axis_index("x") + 2 * lax.axis_index("y") + lax.axis_index("c")
    depth = w["norm_mix_g"].shape[0]
    h = x[0]
    nh = SB_HEADS

    wb = {n: w[n].astype(BF16) for n in BIG}
    wg = {n: None for n in BIG}
    for n in ("w_in_rec", "w_out_rec", "w_mlp_up", "w_mlp_down", "w_ple_proj", "w_ple_gate"):
        wg[n] = all_gather(wb[n], name=f"ag0_{n}", axis=BIG[n], first_layer_only=True)
    mix_names = lambda layer: ("w_in_rec", "w_out_rec") if layer % 2 == 0 else ("w_qkv", "w_o_attn")

    def mm_gathering(targets, *args, **kw):
        ride = Ride()
        for n, layer in targets:
            ride.gather(wb[n], BIG[n], layer, wg[n])
        outs = mm_w(*args, ride=ride if targets else None, **kw)
        for (n, _), a in zip(targets, outs[len(outs) - len(targets):]):
            wg[n] = a
        return outs[:len(outs) - len(targets)]

    conv_shapes = [w[n].shape for n in CONV_W]
    conv_all = all_gather(_pack([w[n] for n in CONV_W]), name="ag_conv_w")
    conv_full = [jnp.moveaxis(a, 0, -2).reshape(a.shape[1:-1] + (-1,))
                 for a in _unpack(conv_all, conv_shapes, lead=(N_DEV,))]
    conv_a_w, conv_b_w = conv_full
    vec = lambda a: a.reshape(1, -1)

    saved = []
    for i in range(depth):
        j = i // 2
        s = {"h0": h}
        in_name, out_name = mix_names(i)
        s["in_name"], s["out_name"] = in_name, out_name
        nxt = i + 1
        if nxt < depth:
            t_in, t_out = [(mix_names(nxt)[0], nxt // 2)], [(mix_names(nxt)[1], nxt // 2), ("w_ple_gate", nxt)]
            t_up, t_down, t_gate = [("w_mlp_down", nxt)], [("w_mlp_up", nxt)], [("w_ple_proj", nxt)]
        else:
            t_in = t_out = t_up = t_down = t_gate = []
        s["hn1"] = rms_fwd(h, w["norm_mix_g"][i], name="rms_fwd")
        if i % 2 == 0:
            (s["u"],) = mm_gathering(t_in, s["hn1"], wg[in_name], j, transpose_w=False, out_dtypes=[F32],
                                     name="mm_in_rec")
            ya, s["ya1"] = conf_fwd(s["u"], conv_a_w[j], vec(w["conv_a_b"][j]), vec(w["ln_a_g"][j]),
                                    vec(w["ln_a_b"][j]), name="conf_fwd")
            yb, s["hs"] = rglru_fwd(s["u"], conv_b_w[j], vec(w["conv_b_b"][j]), w["w_rg_a"][j].astype(BF16),
                                    vec(w["b_rg_a"][j]), w["w_rg_x"][j].astype(BF16), vec(w["b_rg_x"][j]),
                                    vec(w["rg_lambda"][j]), name="rglru_fwd")
            s["mix_in"] = jnp.concatenate([ya, yb], axis=1)
        else:
            (s["qkv"],) = mm_gathering(t_in, s["hn1"], wg[in_name], j, transpose_w=False, out_dtypes=[BF16],
                                       name="mm_qkv")
            s["mix_in"], s["s_tot"], s["start"] = attn_fwd(s["qkv"], nh, name="attn_fwd")
        (h,) = mm_gathering(t_out, s["mix_in"], wg[out_name], j, transpose_w=False, out_dtypes=[F32],
                            extras=[h], epilogue=lambda acc, res: (res + acc,), name="mm_mix_out")
        s["h1"] = h
        s["hn2"] = rms_fwd(h, w["norm_mlp_g"][i], name="rms_fwd")
        relu = lambda acc: jnp.maximum(acc, 0.0)
        s["up"], s["act"] = mm_gathering(t_up, s["hn2"], wg["w_mlp_up"], i, transpose_w=False,
                                         out_dtypes=[BF16, BF16], name="mm_mlp_up", tm=1024,
                                         epilogue=lambda acc: (acc, relu(acc) * relu(acc)))
        (h,) = mm_gathering(t_down, s["act"], wg["w_mlp_down"], i, transpose_w=False, out_dtypes=[F32],
                            tm=1024, extras=[h], epilogue=lambda acc, res: (res + acc,),
                            name="mm_mlp_down")
        s["h2"] = h
        s["hn3"] = rms_fwd(h, w["norm_ple_g"][i], name="rms_fwd")
        s["p"] = p[i, 0].astype(BF16)
        (s["pp"],) = mm_w(s["p"], wg["w_ple_proj"], i, transpose_w=False, out_dtypes=[F32], name="mm_ple_proj")
        h, s["gz"] = mm_gathering(t_gate, s["hn3"], wg["w_ple_gate"], i, transpose_w=False,
                                  out_dtypes=[F32, BF16], extras=[h, s["pp"]], name="mm_ple_gate",
                                  epilogue=lambda acc, res, pp: (res + pp * jax.nn.sigmoid(acc), acc))
        saved.append(s)

    dh, dg_f, loss_part = loss_head(h, w["norm_f_g"], loss_target[0], name="loss_head")
    loss = lax.psum(loss_part[0, 0], ("x", "y", "c"))

    acc = {n: None for n in BIG}
    part = {n: [None] * w[n].shape[0] for n in REPLICATED + CONV_W if n != "norm_f_g"}
    part["norm_f_g"] = dg_f[0]

    recv = {n: None for n in BIG}

    def exchanging(targets):
        ride = Ride()
        for n, layer in targets:
            ride.exchange(acc[n], BIG[n], layer, recv[n])
        return ride

    def landed(targets, arrays):
        for (n, _), a in zip(targets, arrays):
            recv[n] = a

    def dw(name, a, b, layer, targets=()):
        outs = mm_dw(a, b, acc[name], layer, wg[name].shape[0], name=f"dw_{name}",
                     ride=exchanging(targets) if targets else None)
        acc[name] = outs[0] if targets else outs
        landed(targets, outs[1:] if targets else [])

    def mm_exchanging(targets, *args, **kw):
        outs = mm_w(*args, ride=exchanging(targets), **kw)
        landed(targets, outs[len(outs) - len(targets):])
        return outs[:len(outs) - len(targets)]

    for i in reversed(range(depth)):
        j = i // 2
        s = saved[i]
        in_name, out_name = s["in_name"], s["out_name"]
        d_pp, d_gz = ple_bwd(dh, s["gz"], s["pp"], name="ple_bwd")
        dw("w_ple_proj", s["p"], d_pp, i)
        dw("w_ple_gate", s["hn3"], d_gz, i)
        (d_hn3,) = mm_exchanging([("w_ple_proj", i), ("w_ple_gate", i)], d_gz, wg["w_ple_gate"], i,
                                 transpose_w=True, out_dtypes=[BF16], name="mmT_ple_gate")
        dh, dh_b, dg = rms_bwd(s["h2"], w["norm_ple_g"][i], d_hn3, dh, name="rms_bwd")
        part["norm_ple_g"][i] = dg[0]
        (d_up,) = mm_w(dh_b, wg["w_mlp_down"], i, transpose_w=True, out_dtypes=[BF16], tm=1024,
                       extras=[s["up"]], name="mmT_mlp_down",
                       epilogue=lambda acc_, up: (acc_ * (2.0 * jnp.maximum(up.astype(F32), 0.0)),))
        dw("w_mlp_down", s["act"], dh_b, i)
        dw("w_mlp_up", s["hn2"], d_up, i, targets=[("w_mlp_down", i)])
        (d_hn2,) = mm_w(d_up, wg["w_mlp_up"], i, transpose_w=True, out_dtypes=[BF16], tm=1024, name="mmT_mlp_up")
        up_grad = [("w_mlp_up", i)]
        dh, dh_b, dg = rms_bwd(s["h1"], w["norm_mlp_g"][i], d_hn2, dh, name="rms_bwd")
        part["norm_mlp_g"][i] = dg[0]
        dw(out_name, s["mix_in"], dh_b, j)
        if i % 2 == 0:
            (d_cat,) = mm_exchanging([(out_name, j)], dh_b, wg[out_name], j, transpose_w=True,
                                     out_dtypes=[BF16], name="mmT_mix_out")
            du_a, pg_a, *sent = conf_bwd(s["u"], s["ya1"], d_cat, conv_a_w[j], vec(w["ln_a_g"][j]),
                                         vec(w["ln_a_b"][j]), name="conf_bwd", ride=exchanging(up_grad))
            landed(up_grad, sent)
            du_b, pg_b, d_wa, d_wx = rglru_bwd(
                s["u"], s["hs"], d_cat, conv_b_w[j], vec(w["conv_b_b"][j]), w["w_rg_a"][j].astype(BF16),
                vec(w["b_rg_a"][j]), w["w_rg_x"][j].astype(BF16), vec(w["b_rg_x"][j]), vec(w["rg_lambda"][j]),
                name="rglru_bwd")
            ka, kb = conv_a_w.shape[1], conv_b_w.shape[1]
            part["conv_a_w"][j], part["conv_a_b"][j] = pg_a[:ka], pg_a[ka]
            part["ln_a_g"][j], part["ln_a_b"][j] = pg_a[ka + 1], pg_a[ka + 2]
            part["conv_b_w"][j], part["conv_b_b"][j] = pg_b[:kb], pg_b[kb]
            part["b_rg_a"][j], part["b_rg_x"][j], part["rg_lambda"][j] = pg_b[kb + 1], pg_b[kb + 2], pg_b[kb + 3]
            part["w_rg_a"][j], part["w_rg_x"][j] = d_wa, d_wx
            d_mix = jnp.concatenate([du_a, du_b], axis=1)
        else:
            (d_o,) = mm_exchanging([(out_name, j)], dh_b, wg[out_name], j, transpose_w=True,
                                   out_dtypes=[BF16], name="mmT_mix_out")
            dq, dk, dv, *sent = attn_bwd(s["qkv"], d_o, s["s_tot"], s["start"], nh, name="attn_bwd",
                                         ride=exchanging(up_grad))
            landed(up_grad, sent)
            d_mix = jnp.concatenate([dq, dk.astype(BF16), dv.astype(BF16)], axis=1)
        dw(in_name, s["hn1"], d_mix, j)
        (d_hn1,) = mm_exchanging([(in_name, j)], d_mix, wg[in_name], j, transpose_w=True, out_dtypes=[BF16],
                                 name="mmT_mix_in")
        dh, _, dg = rms_bwd(s["h0"], w["norm_mix_g"][i], d_hn1, dh, name="rms_bwd")
        part["norm_mix_g"][i] = dg[0]

    grads, deltas, new_m, new_v = {}, {}, {}, {}

    def finish(name, outs, shape):
        for d, o in zip((grads, deltas, new_m, new_v), outs):
            d[name] = o.reshape(shape)

    for n in BIG:
        _, nl, r, c = recv[n].shape
        flat = lambda a: a.reshape(nl * r, c)
        finish(n, reduce_adamw(recv[n].reshape(N_DEV, nl * r, c), flat(w[n]), flat(mom[n]), flat(var[n]),
                               name=f"adamw_{n}"), w[n].shape)

    full = {n: (part[n] if n == "norm_f_g" else jnp.stack(part[n])) for n in REPLICATED + CONV_W}
    rep = _pack([full[n] for n in REPLICATED])
    conv = _pack([full[n] for n in CONV_W])
    small = all_gather(jnp.concatenate([rep, conv]), name="ag_small_grads")
    rep_shapes = [w[n].shape for n in REPLICATED]
    outs = reduce_adamw(small[:, :rep.shape[0]], _pack([w[n] for n in REPLICATED]),
                        _pack([mom[n] for n in REPLICATED]), _pack([var[n] for n in REPLICATED]), name="adamw_small")
    for d, o in zip((grads, deltas, new_m, new_v), outs):
        for n, a in zip(REPLICATED, _unpack(o, rep_shapes)):
            d[n] = a
    conv_parts = _unpack(small[:, rep.shape[0]:], [full[n].shape for n in CONV_W], lead=(N_DEV,))
    width = w["conv_a_w"].shape[-1]
    mine = [lax.dynamic_slice_in_dim(a, dev * width, width, axis=a.ndim - 1) for a in conv_parts]
    packed = jnp.stack([_pack([a[d] for a in mine]) for d in range(N_DEV)])
    outs = reduce_adamw(packed, _pack([w[n] for n in CONV_W]), _pack([mom[n] for n in CONV_W]),
                        _pack([var[n] for n in CONV_W]), name="adamw_conv_w")
    for d, o in zip((grads, deltas, new_m, new_v), outs):
        for n, a in zip(CONV_W, _unpack(o, conv_shapes)):
            d[n] = a
    return loss, dh[None], grads, deltas, new_m, new_v


def kernel(x, p, norm_mix_g, norm_mlp_g, norm_ple_g, norm_f_g, w_in_rec, conv_a_w, conv_a_b, ln_a_g, ln_a_b, conv_b_w, conv_b_b, w_rg_a, b_rg_a, w_rg_x, b_rg_x, rg_lambda, w_out_rec, w_qkv, w_o_attn, w_mlp_up, w_mlp_down, w_ple_proj, w_ple_gate, loss_target, m_norm_mix_g, m_norm_mlp_g, m_norm_ple_g, m_norm_f_g, m_w_in_rec, m_conv_a_w, m_conv_a_b, m_ln_a_g, m_ln_a_b, m_conv_b_w, m_conv_b_b, m_w_rg_a, m_b_rg_a, m_w_rg_x, m_b_rg_x, m_rg_lambda, m_w_out_rec, m_w_qkv, m_w_o_attn, m_w_mlp_up, m_w_mlp_down, m_w_ple_proj, m_w_ple_gate, v_norm_mix_g, v_norm_mlp_g, v_norm_ple_g, v_norm_f_g, v_w_in_rec, v_conv_a_w, v_conv_a_b, v_ln_a_g, v_ln_a_b, v_conv_b_w, v_conv_b_b, v_w_rg_a, v_b_rg_a, v_w_rg_x, v_b_rg_x, v_rg_lambda, v_w_out_rec, v_w_qkv, v_w_o_attn, v_w_mlp_up, v_w_mlp_down, v_w_ple_proj, v_w_ple_gate):
    given = dict(locals())
    w = {n: given[n] for n in WEIGHTS}
    mom = {n: given["m_" + n] for n in WEIGHTS}
    var = {n: given["v_" + n] for n in WEIGHTS}
    loss, grad_x, grads, deltas, new_m, new_v = _step(x, p, loss_target, w, mom, var)
    return (loss, grad_x, *[grads[n] for n in WEIGHTS], *[deltas[n] for n in WEIGHTS],
            *[new_m[n] for n in WEIGHTS], *[new_v[n] for n in WEIGHTS])
```

```python
import functools
import math

import jax
import jax.numpy as jnp
from jax import lax
from jax.experimental import pallas as pl
from jax.experimental.pallas import tpu as pltpu

F32, BF16 = jnp.float32, jnp.bfloat16
EPS = 1e-6
N_DEV = 8
SB_HEADS = 16
RG_C = 8.0
HALO_A = 32
HALO_B = 8
CONV_ROWS = 64
LANES = 128
VMEM_LIMIT = 48 * 1024 * 1024
ADAM_LR, ADAM_B1, ADAM_B2, ADAM_EPS, ADAM_WD, ADAM_STEP = 0.001, 0.9, 0.999, 1e-08, 0.01, 10
MESH = pl.DeviceIdType.MESH
SDS = jax.ShapeDtypeStruct
ANY = pl.BlockSpec(memory_space=pl.ANY)


def _cp(*sem):
    return pltpu.CompilerParams(dimension_semantics=sem, vmem_limit_bytes=VMEM_LIMIT)


def _tile(dim, pref, align=LANES):
    if dim <= pref:
        return dim
    t = (pref // align) * align
    while t >= align:
        if dim % t == 0:
            return t
        t -= align
    return dim


def _softplus(z):
    return jnp.maximum(z, 0.0) + jnp.log(1.0 + jnp.exp(-jnp.abs(z)))


def _expm1(x):
    t = x * (1.0 + x * (0.5 + x * (1.0 / 6.0 + x * (1.0 / 24.0 + x * (1.0 / 120.0)))))
    return jnp.where(jnp.abs(x) < 0.1, t, jnp.exp(x) - 1.0)


_GELU_C = math.sqrt(2.0 / math.pi)


def _gelu(x):
    return 0.5 * x * (1.0 + jnp.tanh(_GELU_C * (x + 0.044715 * x * x * x)))


def _gelu_grad(x):
    th = jnp.tanh(_GELU_C * (x + 0.044715 * x * x * x))
    return 0.5 * (1.0 + th) + 0.5 * x * (1.0 - th * th) * _GELU_C * (1.0 + 3.0 * 0.044715 * x * x)


def _dot(a, b):
    return jnp.dot(a, b, preferred_element_type=F32)


def _dot_nt(a, b):
    return lax.dot_general(a, b, (((1,), (1,)), ((), ())), preferred_element_type=F32)


def _dot_tn(a, b):
    return lax.dot_general(a, b, (((0,), (0,)), ((), ())), preferred_element_type=F32)


def _shard_of(ref, axis, dev, size):
    if axis is None:
        return ref.at[dev]
    return ref.at[(slice(None),) * axis + (pl.ds(pl.multiple_of(dev * size, size), size),)]


def _peer(k, mx, my, mc):
    return (1 - mx if k & 4 else mx, 1 - my if k & 2 else my, 1 - mc if k & 1 else mc)


class Ride:
    def __init__(self):
        self.items = []

    def gather(self, shard, axis, layer, whole_prev):
        shape = shard.shape[:axis] + (N_DEV * shard.shape[axis],) + shard.shape[axis + 1:]
        self.items.append(("gather", shard, whole_prev, SDS(shape, shard.dtype), axis, layer, shard.shape[axis]))
        return self

    def exchange(self, grad, axis, layer, recv_prev):
        size = grad.shape[axis] // N_DEV
        shape = (N_DEV,) + grad.shape[:axis] + (size,) + grad.shape[axis + 1:]
        self.items.append(("exchange", grad, recv_prev, SDS(shape, grad.dtype), axis, layer, size))
        return self

    def operands(self):
        return [a for it in self.items for a in ([it[1]] if it[2] is None else [it[1], it[2]])]

    def out_shapes(self):
        return [it[3] for it in self.items]

    def aliases(self, first_in, first_out):
        out, pos = {}, first_in
        for t, it in enumerate(self.items):
            pos += 1
            if it[2] is not None:
                out[pos] = first_out + t
                pos += 1
        return out

    def scratch(self):
        n = len(self.items)
        return [pltpu.SemaphoreType.DMA((n, N_DEV - 1)), pltpu.SemaphoreType.DMA((n, N_DEV - 1)),
                pltpu.SemaphoreType.DMA((n,))]

    def plan(self, in_refs, out_refs, send_sems, recv_sems, local_sems):
        mx, my, mc = lax.axis_index("x"), lax.axis_index("y"), lax.axis_index("c")
        me, sibling = (mx, my, mc), (mx, my, 1 - mc)
        lin = lambda d: 4 * d[0] + 2 * d[1] + d[2]
        first, middle, last, pos = [], [], [], 0
        for t, (kind, _, prev, _, axis, layer, size) in enumerate(self.items):
            src = in_refs[pos].at[layer]
            pos += 1 if prev is None else 2

            def remote(act, k, src_fn, dst_fn, to, t=t):
                return lambda: getattr(pltpu.make_async_remote_copy(
                    src_ref=src_fn(), dst_ref=dst_fn(), send_sem=send_sems.at[t, k], recv_sem=recv_sems.at[t, k],
                    device_id=to, device_id_type=MESH), act)()

            def local(act, src_fn, dst_fn, t=t):
                return lambda: getattr(pltpu.make_async_copy(src_fn(), dst_fn(), local_sems.at[t]), act)()

            src_fn = lambda src=src: src
            if kind == "exchange":
                dst_fn = lambda t=t, layer=layer: out_refs[t].at[lin(me), layer]
                shard = lambda d, src=src, axis=axis, size=size: (lambda: _shard_of(src, axis - 1, lin(d), size))
                for act, phase in (("start", first), ("wait", last)):
                    phase.append(local(act, shard(me), dst_fn))
                    for k in range(1, N_DEV):
                        peer = _peer(k, mx, my, mc)
                        phase.append(remote(act, k - 1, shard(peer), dst_fn, peer))
                continue
            whole = out_refs[t].at[layer]
            slot = lambda d, whole=whole, axis=axis, size=size: (lambda: _shard_of(whole, axis - 1, lin(d), size))
            chips = [(1 - mx, my), (mx, 1 - my), (1 - mx, 1 - my)]
            own = [(0, sibling)] + [(1 + j, (*chip, mc)) for j, chip in enumerate(chips)]
            first.append(local("start", src_fn, slot(me)))
            first += [remote("start", k, src_fn, slot(me), to) for k, to in own]
            for j, chip in enumerate(chips):
                landed = slot((*chip, mc))
                middle += [remote("wait_recv", 1 + j, src_fn, landed, me), remote("start", 4 + j, landed, landed, sibling)]
                last.append(remote("wait_send", 4 + j, landed, landed, sibling))
            last.append(remote("wait_recv", 0, src_fn, slot(sibling), me))
            last += [remote("wait_recv", 4 + j, src_fn, slot((*chip, 1 - mc)), me) for j, chip in enumerate(chips)]
            last += [remote("wait_send", k, src_fn, slot(me), to) for k, to in own]
            last.append(local("wait", src_fn, slot(me)))
        return first, middle, last


def _ride_hooks(ride, grid, in_refs, out_refs, sems):
    step, total = 0, 1
    for d, g in enumerate(grid):
        step, total = step * g + pl.program_id(d), total * g

    def at(which, when):
        @pl.when(step == when)
        def _():
            for act in ride.plan(in_refs, out_refs, *sems)[which]:
                act()

    def start():
        at(0, 0)
        at(1, total // 2)

    def finish():
        at(2, total - 1)

    return start, finish


def hosted_call(body, ride, *, name, grid, in_specs, out_specs, out_shape, scratch_shapes, operands):
    if not ride:
        return pl.pallas_call(
            body, name=name, grid=grid, in_specs=in_specs, out_specs=out_specs, out_shape=out_shape,
            scratch_shapes=scratch_shapes, compiler_params=_cp(*["arbitrary"] * len(grid)))(*operands)
    n_in, n_out, n_scratch = len(in_specs), len(out_specs), len(scratch_shapes)
    ride_in = ride.operands()

    def carrying(*refs):
        ins, refs = refs[:n_in], refs[n_in:]
        ride_in_refs, refs = refs[:len(ride_in)], refs[len(ride_in):]
        outs, refs = refs[:n_out], refs[n_out:]
        ride_out_refs, refs = refs[:len(ride.items)], refs[len(ride.items):]
        start, finish = _ride_hooks(ride, grid, ride_in_refs, ride_out_refs, refs[n_scratch:])
        start()
        body(*ins, *outs, *refs[:n_scratch])
        finish()

    return pl.pallas_call(
        carrying, name=name, grid=grid, in_specs=list(in_specs) + [ANY] * len(ride_in),
        out_specs=list(out_specs) + [ANY] * len(ride.items), out_shape=list(out_shape) + ride.out_shapes(),
        scratch_shapes=list(scratch_shapes) + ride.scratch(), input_output_aliases=ride.aliases(n_in, n_out),
        compiler_params=_cp(*["arbitrary"] * len(grid)))(*operands, *ride_in)


def mm_w(a, w, layer, *, transpose_w, out_dtypes, name, extras=(), epilogue=None, ride=None,
         tm=512, tn=1024, tk=2048):
    m, k_dim = a.shape
    _, rows, cols = w.shape
    n = rows if transpose_w else cols
    assert k_dim == (cols if transpose_w else rows), (a.shape, w.shape)
    tm, tn, tk = _tile(m, tm, 8), _tile(n, tn), _tile(k_dim, tk)
    nk = k_dim // tk
    grid = (m // tm, n // tn, nk)
    n_extra, n_out = len(extras), len(out_dtypes)
    ride_in = ride.operands() if ride else []
    n_ride_out = len(ride.items) if ride else 0
    if epilogue is None:
        epilogue = lambda acc: (acc,)
    dot = _dot_nt if transpose_w else _dot

    def body(a_ref, w_ref, *rest):
        extra_refs, rest = rest[:n_extra], rest[n_extra:]
        ride_in_refs, rest = rest[:len(ride_in)], rest[len(ride_in):]
        out_refs, rest = rest[:n_out], rest[n_out:]
        ride_out_refs, scratch = rest[:n_ride_out], rest[n_ride_out:]
        if ride:
            start, finish = _ride_hooks(ride, grid, ride_in_refs, ride_out_refs, scratch[-3:])
            start()

        def write(acc):
            outs = epilogue(acc, *[r[...] for r in extra_refs])
            for o_ref, o in zip(out_refs, outs):
                o_ref[...] = o.astype(o_ref.dtype)

        if nk == 1:
            write(dot(a_ref[...], w_ref[...]))
        else:
            acc = scratch[0]
            k = pl.program_id(2)

            @pl.when(k == 0)
            def _():
                acc[...] = jnp.zeros_like(acc)

            acc[...] += dot(a_ref[...], w_ref[...])

            @pl.when(k == nk - 1)
            def _():
                write(acc[...])

        if ride:
            finish()

    if transpose_w:
        w_spec = pl.BlockSpec((None, tn, tk), lambda i, j, k: (layer, j, k))
    else:
        w_spec = pl.BlockSpec((None, tk, tn), lambda i, j, k: (layer, k, j))
    tile_spec = pl.BlockSpec((tm, tn), lambda i, j, k: (i, j))
    return pl.pallas_call(
        body, name=name, grid=grid,
        in_specs=[pl.BlockSpec((tm, tk), lambda i, j, k: (i, k)), w_spec] + [tile_spec] * n_extra
        + [ANY] * len(ride_in),
        out_specs=[tile_spec] * n_out + [ANY] * n_ride_out,
        out_shape=[SDS((m, n), dt) for dt in out_dtypes] + (ride.out_shapes() if ride else []),
        scratch_shapes=([pltpu.VMEM((tm, tn), F32)] if nk > 1 else []) + (ride.scratch() if ride else []),
        input_output_aliases=ride.aliases(2 + n_extra, n_out) if ride else {},
        compiler_params=_cp("arbitrary", "arbitrary", "arbitrary") if ride
        else _cp("parallel", "parallel", "arbitrary"),
    )(a, w, *extras, *ride_in)


def mm_dw(a, b, buf, layer, n_layers, name, ride=None, tm=512, tn=1024, tk=2048):
    tokens, m = a.shape
    _, n = b.shape
    tm, tn, tk = _tile(m, tm), _tile(n, tn), _tile(tokens, tk)
    nk = tokens // tk
    grid = (m // tm, n // tn, nk)
    n_buf = 0 if buf is None else 1
    ride_in = ride.operands() if ride else []
    n_ride_out = len(ride.items) if ride else 0

    def body(a_ref, b_ref, *rest):
        rest = rest[n_buf:]
        ride_in_refs, rest = rest[:len(ride_in)], rest[len(ride_in):]
        o_ref, rest = rest[0], rest[1:]
        ride_out_refs, scratch = rest[:n_ride_out], rest[n_ride_out:]
        acc = scratch[0]
        if ride:
            start, finish = _ride_hooks(ride, grid, ride_in_refs, ride_out_refs, scratch[-3:])
            start()
        k = pl.program_id(2)

        @pl.when(k == 0)
        def _():
            acc[...] = jnp.zeros_like(acc)

        acc[...] += _dot_tn(a_ref[...], b_ref[...])

        @pl.when(k == nk - 1)
        def _():
            o_ref[...] = acc[...].astype(BF16)

        if ride:
            finish()

    aliases = {} if buf is None else {2: 0}
    if ride:
        aliases.update(ride.aliases(2 + n_buf, 1))
    outs = pl.pallas_call(
        body, name=name, grid=grid,
        in_specs=[pl.BlockSpec((tk, tm), lambda i, j, k: (k, i)),
                  pl.BlockSpec((tk, tn), lambda i, j, k: (k, j))] + [ANY] * (n_buf + len(ride_in)),
        out_specs=[pl.BlockSpec((None, tm, tn), lambda i, j, k: (layer, i, j))] + [ANY] * n_ride_out,
        out_shape=[SDS((n_layers, m, n), BF16)] + (ride.out_shapes() if ride else []),
        scratch_shapes=[pltpu.VMEM((tm, tn), F32)] + (ride.scratch() if ride else []),
        input_output_aliases=aliases,
        compiler_params=_cp("arbitrary", "arbitrary", "arbitrary") if ride
        else _cp("parallel", "parallel", "arbitrary"),
    )(a, b, *([] if buf is None else [buf]), *ride_in)
    return outs if ride else outs[0]


def rms_fwd(h, g, name):
    t, d = h.shape
    tt = _tile(t, 512, 8)

    def body(h_ref, g_ref, o_ref):
        x = h_ref[...]
        r = lax.rsqrt(jnp.mean(x * x, axis=-1, keepdims=True) + EPS)
        o_ref[...] = (x * r * g_ref[...]).astype(BF16)

    row = pl.BlockSpec((tt, d), lambda i: (i, 0))
    return pl.pallas_call(
        body, name=name, grid=(t // tt,),
        in_specs=[row, pl.BlockSpec((1, d), lambda i: (0, 0))], out_specs=row,
        out_shape=SDS((t, d), BF16), compiler_params=_cp("parallel"),
    )(h, g.reshape(1, d))


def _rms_bwd_math(x, g, dhn):
    r = lax.rsqrt(jnp.mean(x * x, axis=-1, keepdims=True) + EPS)
    xn = x * r
    dxn = dhn * g
    dx = r * (dxn - xn * jnp.mean(dxn * xn, axis=-1, keepdims=True))
    return dx, jnp.sum(dhn * xn, axis=0, keepdims=True)


def rms_bwd(h, g, dhn, dres, name):
    t, d = h.shape
    tt = _tile(t, 256, 8)

    def body(h_ref, g_ref, dhn_ref, dres_ref, dh_ref, dhb_ref, dg_ref):
        @pl.when(pl.program_id(0) == 0)
        def _():
            dg_ref[...] = jnp.zeros_like(dg_ref)

        dx, dg = _rms_bwd_math(h_ref[...], g_ref[...], dhn_ref[...].astype(F32))
        dh = dres_ref[...] + dx
        dh_ref[...] = dh
        dhb_ref[...] = dh.astype(BF16)
        dg_ref[...] += dg

    row = pl.BlockSpec((tt, d), lambda i: (i, 0))
    vec = pl.BlockSpec((1, d), lambda i: (0, 0))
    return pl.pallas_call(
        body, name=name, grid=(t // tt,),
        in_specs=[row, vec, row, row], out_specs=[row, row, vec],
        out_shape=[SDS((t, d), F32), SDS((t, d), BF16), SDS((1, d), F32)],
        compiler_params=_cp("arbitrary"),
    )(h, g.reshape(1, d), dhn, dres)


def loss_head(h, g, target, name):
    t, d = h.shape
    tt = _tile(t, 256, 8)

    def body(h_ref, g_ref, t_ref, dh_ref, dg_ref, loss_ref):
        @pl.when(pl.program_id(0) == 0)
        def _():
            dg_ref[...] = jnp.zeros_like(dg_ref)
            loss_ref[...] = jnp.zeros_like(loss_ref)

        x, gain = h_ref[...], g_ref[...]
        r = lax.rsqrt(jnp.mean(x * x, axis=-1, keepdims=True) + EPS)
        err = x * r * gain - t_ref[...]
        loss_ref[...] += 0.5 * jnp.sum(jnp.mean(err * err, axis=-1, keepdims=True))
        dx, dg = _rms_bwd_math(x, gain, err * (1.0 / d))
        dh_ref[...] = dx
        dg_ref[...] += dg

    row = pl.BlockSpec((tt, d), lambda i: (i, 0))
    vec = pl.BlockSpec((1, d), lambda i: (0, 0))
    return pl.pallas_call(
        body, name=name, grid=(t // tt,),
        in_specs=[row, vec, row],
        out_specs=[row, vec, pl.BlockSpec((8, LANES), lambda i: (0, 0))],
        out_shape=[SDS((t, d), F32), SDS((1, d), F32), SDS((8, LANES), F32)],
        compiler_params=_cp("arbitrary"),
    )(h, g.reshape(1, d), target)


def ple_bwd(dh, gz, pp, name):
    t, d = dh.shape
    tt = _tile(t, 256, 8)

    def body(dh_ref, gz_ref, pp_ref, dpp_ref, dgz_ref):
        g = dh_ref[...]
        gate = jax.nn.sigmoid(gz_ref[...].astype(F32))
        dpp_ref[...] = (g * gate).astype(BF16)
        dgz_ref[...] = (g * pp_ref[...].astype(F32) * gate * (1.0 - gate)).astype(BF16)

    row = pl.BlockSpec((tt, d), lambda i: (i, 0))
    return pl.pallas_call(
        body, name=name, grid=(t // tt,), in_specs=[row, row, row], out_specs=[row, row],
        out_shape=[SDS((t, d), BF16), SDS((t, d), BF16)], compiler_params=_cp("parallel"),
    )(dh, gz, pp)


def _layer_norm_parts(y1):
    mu = jnp.mean(y1, axis=-1, keepdims=True)
    dlt = y1 - mu
    rstd = lax.rsqrt(jnp.mean(dlt * dlt, axis=-1, keepdims=True) + EPS)
    return dlt * rstd, rstd


def _fill_shifted(buf, shifted, tt):
    for r in range(1, 8):
        shifted[r - 1] = buf[pl.ds(r, tt + HALO_A - 8), :]


def _rows_from(buf, shifted, start, tt, cs):
    q, r = divmod(start, 8)
    if r == 0:
        return buf[pl.ds(8 * q, tt), cs]
    return shifted[r - 1, pl.ds(8 * q, tt), cs]


def conf_fwd(u, cw, cb, lg, lb, name, ride=None):
    t = u.shape[0]
    ka, c = cw.shape
    tt = _tile(t, 256, HALO_A)
    hb = tt // HALO_A
    off = HALO_A - (ka - 1)
    grp = _tile(tt, CONV_ROWS, 8)

    def body(av_ref, ag_ref, pv_ref, pg_ref, cw_ref, cb_ref, lg_ref, lb_ref, ya_ref, ya1_ref, buf, shifted):
        live = (pl.program_id(0) > 0).astype(F32)
        buf[0:HALO_A, :] = pv_ref[...] * jax.nn.sigmoid(pg_ref[...]) * live
        buf[HALO_A:, :] = av_ref[...] * jax.nn.sigmoid(ag_ref[...])
        _fill_shifted(buf, shifted, tt)
        for c0 in range(0, c, LANES):
            cs = pl.ds(c0, LANES)
            for g0 in range(0, tt, grp):
                acc = jnp.broadcast_to(cb_ref[:, cs], (grp, LANES))
                for k in range(ka):
                    acc = acc + cw_ref[k:k + 1, cs] * _rows_from(buf, shifted, off + k + g0, grp, cs)
                ya1_ref[pl.ds(g0, grp), cs] = acc
        yn, _ = _layer_norm_parts(ya1_ref[...])
        y2 = yn * lg_ref[...] + lb_ref[...]
        ya_ref[...] = (y2 * jax.nn.sigmoid(y2)).astype(BF16)

    cur = lambda col: pl.BlockSpec((tt, c), lambda i: (i, col))
    prev = lambda col: pl.BlockSpec((HALO_A, c), lambda i: (jnp.maximum(i * hb - 1, 0), col))
    vec = pl.BlockSpec((1, c), lambda i: (0, 0))
    return hosted_call(
        body, ride, name=name, grid=(t // tt,),
        in_specs=[cur(0), cur(1), prev(0), prev(1), pl.BlockSpec((ka, c), lambda i: (0, 0)), vec, vec, vec],
        out_specs=[pl.BlockSpec((tt, c), lambda i: (i, 0))] * 2,
        out_shape=[SDS((t, c), BF16), SDS((t, c), F32)],
        scratch_shapes=[pltpu.VMEM((tt + HALO_A, c), F32), pltpu.VMEM((7, tt + HALO_A - 8, c), F32)],
        operands=(u, u, u, u, cw, cb, lg, lb))


def conf_bwd(u, ya1, dcat, cw, lg, lb, name, ride=None):
    t = u.shape[0]
    ka, c = cw.shape
    tt = _tile(t, 256, HALO_A)
    nt, hb = t // tt, tt // HALO_A
    off = HALO_A - (ka - 1)
    grp = _tile(tt, CONV_ROWS, 8)

    def body(av_ref, ag_ref, pv_ref, pg_ref, y1_ref, dya_ref, cw_ref, lg_ref, lb_ref,
             du_ref, pgrad_ref, ybuf, dbuf, carry, yshift, dshift):
        i = pl.program_id(0)

        @pl.when(i == 0)
        def _():
            carry[...] = jnp.zeros_like(carry)
            pgrad_ref[...] = jnp.zeros_like(pgrad_ref)

        live = (i < nt - 1).astype(F32)
        av = av_ref[...]
        sg = jax.nn.sigmoid(ag_ref[...])
        ybuf[0:HALO_A, :] = pv_ref[...] * jax.nn.sigmoid(pg_ref[...]) * live
        ybuf[HALO_A:, :] = av * sg
        yn, rstd = _layer_norm_parts(y1_ref[...])
        gain = lg_ref[...]
        y2 = yn * gain + lb_ref[...]
        s2 = jax.nn.sigmoid(y2)
        dy2 = dya_ref[...].astype(F32) * (s2 * (1.0 + y2 * (1.0 - s2)))
        pgrad_ref[ka + 1:ka + 2, :] += jnp.sum(dy2 * yn, axis=0, keepdims=True)
        pgrad_ref[ka + 2:ka + 3, :] += jnp.sum(dy2, axis=0, keepdims=True)
        dyn = dy2 * gain
        dy1 = rstd * (dyn - jnp.mean(dyn, axis=-1, keepdims=True)
                      - yn * jnp.mean(dyn * yn, axis=-1, keepdims=True))
        pgrad_ref[ka:ka + 1, :] += jnp.sum(dy1, axis=0, keepdims=True)
        dbuf[0:tt, :] = dy1
        dbuf[tt:, :] = carry[...]
        carry[...] = dbuf[0:HALO_A, :]
        _fill_shifted(ybuf, yshift, tt)
        _fill_shifted(dbuf, dshift, tt)
        groups = range(0, tt, grp)
        for c0 in range(0, c, LANES):
            cs = pl.ds(c0, LANES)
            accs = [jnp.zeros((grp, LANES), F32) for _ in groups]
            for k in range(ka):
                tap = cw_ref[k:k + 1, cs]
                dw_k = jnp.zeros((8, LANES), F32)
                for gi, g0 in enumerate(groups):
                    prod = dbuf[pl.ds(g0, grp), cs] * _rows_from(ybuf, yshift, off + k + g0, grp, cs)
                    for r0 in range(0, grp, 8):
                        dw_k = dw_k + prod[r0:r0 + 8]
                    accs[gi] = accs[gi] + tap * _rows_from(dbuf, dshift, ka - 1 - k + g0, grp, cs)
                pgrad_ref[k:k + 1, cs] += jnp.sum(dw_k, axis=0, keepdims=True)
            for acc, g0 in zip(accs, groups):
                sgc, avc = sg[g0:g0 + grp, c0:c0 + LANES], av[g0:g0 + grp, c0:c0 + LANES]
                du_ref[pl.ds(g0, grp), cs] = (acc * sgc).astype(BF16)
                du_ref[pl.ds(g0, grp), pl.ds(c + c0, LANES)] = (acc * avc * sgc * (1.0 - sgc)).astype(BF16)

    cur = lambda col: pl.BlockSpec((tt, c), lambda i: (nt - 1 - i, col))
    prev = lambda col: pl.BlockSpec((HALO_A, c), lambda i: (jnp.maximum((nt - 1 - i) * hb - 1, 0), col))
    vec = pl.BlockSpec((1, c), lambda i: (0, 0))
    return hosted_call(
        body, ride, name=name, grid=(nt,),
        in_specs=[cur(0), cur(1), prev(0), prev(1), cur(0), cur(0),
                  pl.BlockSpec((ka, c), lambda i: (0, 0)), vec, vec],
        out_specs=[pl.BlockSpec((tt, 2 * c), lambda i: (nt - 1 - i, 0)),
                   pl.BlockSpec((ka + 3, c), lambda i: (0, 0))],
        out_shape=[SDS((t, 2 * c), BF16), SDS((ka + 3, c), F32)],
        scratch_shapes=[pltpu.VMEM((tt + HALO_A, c), F32), pltpu.VMEM((tt + HALO_A, c), F32),
                        pltpu.VMEM((HALO_A, c), F32), pltpu.VMEM((7, tt + HALO_A - 8, c), F32),
                        pltpu.VMEM((7, tt + HALO_A - 8, c), F32)],
        operands=(u, u, u, u, ya1, dcat, cw, lg, lb))


def _rg_gates(xc, wa_ref, ba, wx_ref, bx, sp, nh, hd):
    parts = []
    for h in range(nh):
        hs = slice(h * hd, (h + 1) * hd)
        xh = xc[:, hs].astype(BF16)
        r = jax.nn.sigmoid(_dot(xh, wa_ref[h]) + ba[:, hs])
        ig = jax.nn.sigmoid(_dot(xh, wx_ref[h]) + bx[:, hs])
        log_a = -RG_C * r * sp[:, hs]
        parts.append((r, ig, jnp.exp(log_a), jnp.sqrt(-_expm1(2.0 * log_a))))
    return parts


def _conv_b(xbuf, bw_ref, bb, tt, kb):
    off = HALO_B - (kb - 1)
    xc = bb
    for k in range(kb):
        xc = xc + bw_ref[k:k + 1, :] * xbuf[pl.ds(off + k, tt), :]
    return xc


def rglru_fwd(u, bw, bb, wa, ba, wx, bx, lam, name, ride=None):
    t = u.shape[0]
    kb, c = bw.shape
    nh, hd, _ = wa.shape
    tt = _tile(t, 256, HALO_B)
    hb = tt // HALO_B

    def body(xr_ref, gr_ref, px_ref, bw_ref, bb_ref, wa_ref, ba_ref, wx_ref, bx_ref, lam_ref,
             yb_ref, hs_ref, xbuf, a_s, u_s, hc):
        i = pl.program_id(0)

        @pl.when(i == 0)
        def _():
            hc[...] = jnp.zeros_like(hc)

        xbuf[0:HALO_B, :] = px_ref[...] * (i > 0).astype(F32)
        xbuf[HALO_B:, :] = xr_ref[...]
        xc = _conv_b(xbuf, bw_ref, bb_ref[...], tt, kb)
        sp = _softplus(-lam_ref[...])
        gates = _rg_gates(xc, wa_ref, ba_ref[...], wx_ref, bx_ref[...], sp, nh, hd)
        for h, (_, ig, a, mult) in enumerate(gates):
            hs = slice(h * hd, (h + 1) * hd)
            a_s[:, hs] = a
            u_s[:, hs] = mult * ig * xc[:, hs]

        def step(g, hcur):
            base = pl.multiple_of(g * 8, 8)
            for j in range(8):
                hcur = a_s[pl.ds(base + j, 1), :] * hcur + u_s[pl.ds(base + j, 1), :]
                hs_ref[pl.ds(base + j, 1), :] = hcur
            return hcur

        hc[...] = lax.fori_loop(0, tt // 8, step, hc[...])
        yb_ref[...] = (hs_ref[...] * _gelu(gr_ref[...])).astype(BF16)

    cur = lambda col: pl.BlockSpec((tt, c), lambda i: (i, col))
    vec = pl.BlockSpec((1, c), lambda i: (0, 0))
    wsp = pl.BlockSpec((nh, hd, hd), lambda i: (0, 0, 0))
    return hosted_call(
        body, ride, name=name, grid=(t // tt,),
        in_specs=[cur(2), cur(3), pl.BlockSpec((HALO_B, c), lambda i: (jnp.maximum(i * hb - 1, 0), 2)),
                  pl.BlockSpec((kb, c), lambda i: (0, 0)), vec, wsp, vec, wsp, vec, vec],
        out_specs=[pl.BlockSpec((tt, c), lambda i: (i, 0))] * 2,
        out_shape=[SDS((t, c), BF16), SDS((t, c), F32)],
        scratch_shapes=[pltpu.VMEM((tt + HALO_B, c), F32), pltpu.VMEM((tt, c), F32),
                        pltpu.VMEM((tt, c), F32), pltpu.VMEM((1, c), F32)],
        operands=(u, u, u, bw, bb, wa, ba, wx, bx, lam))


def rglru_bwd(u, hs_all, dcat, bw, bb, wa, ba, wx, bx, lam, name):
    t = u.shape[0]
    kb, c = bw.shape
    nh, hd, _ = wa.shape
    tt = _tile(t, 256, HALO_B)
    nt, hb = t // tt, tt // HALO_B
    off = HALO_B - (kb - 1)

    def body(xr_ref, gr_ref, px_ref, hs_ref, ph_ref, dyb_ref, bw_ref, bb_ref, wa_ref, ba_ref, wx_ref, bx_ref,
             lam_ref, du_ref, pgrad_ref, dwa_ref, dwx_ref, xbuf, hbuf, a_s, g_s, dxbuf, cg, cdx):
        i = pl.program_id(0)

        @pl.when(i == 0)
        def _():
            cg[...] = jnp.zeros_like(cg)
            cdx[...] = jnp.zeros_like(cdx)
            pgrad_ref[...] = jnp.zeros_like(pgrad_ref)
            dwa_ref[...] = jnp.zeros_like(dwa_ref)
            dwx_ref[...] = jnp.zeros_like(dwx_ref)

        live = (i < nt - 1).astype(F32)
        xbuf[0:HALO_B, :] = px_ref[...] * live
        xbuf[HALO_B:, :] = xr_ref[...]
        hbuf[0:HALO_B, :] = ph_ref[...] * live
        hbuf[HALO_B:, :] = hs_ref[...]
        xc = _conv_b(xbuf, bw_ref, bb_ref[...], tt, kb)
        lam_v = lam_ref[...]
        sp = _softplus(-lam_v)
        gates = _rg_gates(xc, wa_ref, ba_ref[...], wx_ref, bx_ref[...], sp, nh, hd)
        gr = gr_ref[...]
        dyb = dyb_ref[...].astype(F32)
        g_s[...] = dyb * _gelu(gr)
        for h, (_, _, a, _) in enumerate(gates):
            a_s[:, h * hd:(h + 1) * hd] = a

        def step(g, carry):
            base = pl.multiple_of((tt // 8 - 1 - g) * 8, 8)
            for j in range(7, -1, -1):
                gt = g_s[pl.ds(base + j, 1), :] + carry
                g_s[pl.ds(base + j, 1), :] = gt
                carry = a_s[pl.ds(base + j, 1), :] * gt
            return carry

        cg[...] = lax.fori_loop(0, tt // 8, step, cg[...])
        hprev = hbuf[pl.ds(HALO_B - 1, tt), :]
        gfull = g_s[...]
        for h, (r, ig, a, mult) in enumerate(gates):
            hs = slice(h * hd, (h + 1) * hd)
            g, xch = gfull[:, hs], xc[:, hs]
            d_la = g * hprev[:, hs] * a - g * ig * xch * (a * a) / mult
            d_ig = g * mult * xch
            d_ra = d_la * (-RG_C * sp[:, hs]) * r * (1.0 - r)
            d_ia = d_ig * ig * (1.0 - ig)
            dsp = jnp.sum(d_la * (-RG_C) * r, axis=0, keepdims=True)
            pgrad_ref[kb + 1:kb + 2, hs] += jnp.sum(d_ra, axis=0, keepdims=True)
            pgrad_ref[kb + 2:kb + 3, hs] += jnp.sum(d_ia, axis=0, keepdims=True)
            pgrad_ref[kb + 3:kb + 4, hs] += dsp * (-jax.nn.sigmoid(-lam_v[:, hs]))
            xh, d_ra_b, d_ia_b = xch.astype(BF16), d_ra.astype(BF16), d_ia.astype(BF16)
            dwa_ref[h] += _dot_tn(xh, d_ra_b)
            dwx_ref[h] += _dot_tn(xh, d_ia_b)
            dxbuf[0:tt, hs] = g * mult * ig + _dot_nt(d_ra_b, wa_ref[h]) + _dot_nt(d_ia_b, wx_ref[h])
        dxbuf[tt:, :] = cdx[...]
        cdx[...] = dxbuf[0:HALO_B, :]
        d_xc = dxbuf[0:tt, :]
        pgrad_ref[kb:kb + 1, :] += jnp.sum(d_xc, axis=0, keepdims=True)
        d_xr = jnp.zeros((tt, c), F32)
        for k in range(kb):
            d_xr = d_xr + bw_ref[k:k + 1, :] * dxbuf[pl.ds(kb - 1 - k, tt), :]
            pgrad_ref[k:k + 1, :] += jnp.sum(d_xc * xbuf[pl.ds(off + k, tt), :], axis=0, keepdims=True)
        du_ref[:, 0:c] = d_xr.astype(BF16)
        du_ref[:, c:2 * c] = (dyb * hs_ref[...] * _gelu_grad(gr)).astype(BF16)

    cur = lambda col: pl.BlockSpec((tt, c), lambda i: (nt - 1 - i, col))
    prev = lambda col: pl.BlockSpec((HALO_B, c), lambda i: (jnp.maximum((nt - 1 - i) * hb - 1, 0), col))
    vec = pl.BlockSpec((1, c), lambda i: (0, 0))
    wsp = pl.BlockSpec((nh, hd, hd), lambda i: (0, 0, 0))
    return pl.pallas_call(
        body, name=name, grid=(nt,),
        in_specs=[cur(2), cur(3), prev(2), cur(0), prev(0), cur(1),
                  pl.BlockSpec((kb, c), lambda i: (0, 0)), vec, wsp, vec, wsp, vec, vec],
        out_specs=[pl.BlockSpec((tt, 2 * c), lambda i: (nt - 1 - i, 0)),
                   pl.BlockSpec((kb + 4, c), lambda i: (0, 0)), wsp, wsp],
        out_shape=[SDS((t, 2 * c), BF16), SDS((kb + 4, c), F32), SDS((nh, hd, hd), F32), SDS((nh, hd, hd), F32)],
        scratch_shapes=[pltpu.VMEM((tt + HALO_B, c), F32), pltpu.VMEM((tt + HALO_B, c), F32),
                        pltpu.VMEM((tt, c), F32), pltpu.VMEM((tt, c), F32), pltpu.VMEM((tt + HALO_B, c), F32),
                        pltpu.VMEM((1, c), F32), pltpu.VMEM((HALO_B, c), F32)],
        compiler_params=_cp("arbitrary"),
    )(u, u, u, hs_all, hs_all, dcat, bw, bb, wa, ba, wx, bx, lam)


MASS_CUTOFF = 100.0


def attn_fwd(qkv, nh, name):
    t = qkv.shape[0]
    dh = qkv.shape[1] // (3 * nh)
    bq = _tile(t, 256)
    scale = 1.0 / math.sqrt(dh)
    hp = 2 if nh % 2 == 0 else 1

    def body(q_ref, k_ref, v_ref, o_ref, s_ref, start_ref):
        pair, qi = pl.program_id(0), pl.program_id(1)
        row = lax.broadcasted_iota(jnp.int32, (bq, bq), 0)
        col = lax.broadcasted_iota(jnp.int32, (bq, bq), 1)
        tri = (row >= col).astype(BF16)

        def block(hh, kb, c, acc, mask):
            ks = pl.ds(pl.multiple_of(kb * bq, bq), bq)
            hs = pl.ds(hh * dh, dh)
            z = _dot_nt(q_ref[:, hs], k_ref[ks, hs]) * scale
            sp = _softplus(z)
            if mask is not None:
                sp = jnp.where(mask, sp, 0.0)
            lw = z - c - _dot(sp.astype(BF16), tri)
            if mask is not None:
                lw = jnp.where(mask, lw, -1e30)
            acc = acc + _dot(jnp.exp(lw).astype(BF16), v_ref[ks, hs])
            return c + jnp.sum(sp, axis=1, keepdims=True), acc

        state = []
        for hh in range(hp):
            c, acc = block(hh, qi, jnp.zeros((bq, 1), F32), jnp.zeros((bq, dh), F32), col < row)
            state.append(block(hh, jnp.maximum(qi - 1, 0), c, acc, qi > 0))

        def more(st):
            return jnp.logical_and(st[0] >= 0, jnp.min(st[1]) < MASS_CUTOFF)

        for hh, (c, acc) in enumerate(state):
            def step(st, hh=hh):
                c_new, acc_new = block(hh, st[0], st[1], st[2], None)
                return st[0] - 1, c_new, acc_new

            kb, c, acc = lax.while_loop(more, step, (qi - 2, c, acc))
            o_ref[:, pl.ds(hh * dh, dh)] = acc.astype(BF16)
            s_ref[hh] = jnp.broadcast_to(c, (bq, LANES))
            start_ref[hp * pair + hh, qi] = (kb + 1).astype(F32)

    whole = lambda base: pl.BlockSpec((t, hp * dh), lambda h, qi: (0, base + h))
    return pl.pallas_call(
        body, name=name, grid=(nh // hp, t // bq),
        in_specs=[pl.BlockSpec((bq, hp * dh), lambda h, qi: (qi, h)), whole(nh // hp), whole(2 * nh // hp)],
        out_specs=[pl.BlockSpec((bq, hp * dh), lambda h, qi: (qi, h)),
                   pl.BlockSpec((hp, bq, LANES), lambda h, qi: (h, qi, 0)),
                   pl.BlockSpec(memory_space=pltpu.SMEM)],
        out_shape=[SDS((t, nh * dh), BF16), SDS((nh, t, LANES), F32), SDS((nh, t // bq), F32)],
        compiler_params=_cp("arbitrary", "arbitrary"),
    )(qkv, qkv, qkv)


def attn_bwd(qkv, do, s_tot, start, nh, name, ride=None):
    t = qkv.shape[0]
    dh = qkv.shape[1] // (3 * nh)
    bq = _tile(t, 256)
    scale = 1.0 / math.sqrt(dh)
    hp = 2 if nh % 2 == 0 else 1

    def body(start_ref, q_ref, k_ref, v_ref, do_ref, s_ref, dq_ref, dk_ref, dv_ref):
        pair, qi = pl.program_id(0), pl.program_id(1)

        @pl.when(qi == 0)
        def _():
            dk_ref[...] = jnp.zeros_like(dk_ref)
            dv_ref[...] = jnp.zeros_like(dv_ref)

        row = lax.broadcasted_iota(jnp.int32, (bq, bq), 0)
        col = lax.broadcasted_iota(jnp.int32, (bq, bq), 1)
        tri_suffix = (row >= col).astype(BF16)
        tri_prefix = (row <= col).astype(BF16)

        def block(hh, kb, psp, pg, dq, mask):
            ks = pl.ds(pl.multiple_of(kb * bq, bq), bq)
            hs = pl.ds(hh * dh, dh)
            q, do_b, k_b, v_b = q_ref[:, hs], do_ref[:, hs], k_ref[ks, hs], v_ref[ks, hs]
            z = _dot_nt(q, k_b) * scale
            sp_all = _softplus(z)
            sp = sp_all if mask is None else jnp.where(mask, sp_all, 0.0)
            psp = psp + jnp.sum(sp, axis=1, keepdims=True)
            lw = z - (s_ref[hh, :, 0:1] - psp) - _dot(sp.astype(BF16), tri_suffix)
            if mask is not None:
                lw = jnp.where(mask, lw, -1e30)
            a = jnp.exp(lw)
            g = _dot_nt(do_b, v_b) * a
            dz = g - (pg + _dot(g.astype(BF16), tri_prefix)) * jnp.exp(z - sp_all)
            if mask is not None:
                dz = jnp.where(mask, dz, 0.0)
            dzs = (dz * scale).astype(BF16)
            dk_ref[ks, hs] += _dot_tn(dzs, q)
            dv_ref[ks, hs] += _dot_tn(a.astype(BF16), do_b)
            return psp, pg + jnp.sum(g, axis=1, keepdims=True), dq + _dot(dzs, k_b)

        state = []
        for hh in range(hp):
            first = jnp.clip(start_ref[hp * pair + hh, qi].astype(jnp.int32), 0, qi)
            init = (jnp.zeros((bq, 1), F32), jnp.zeros((bq, 1), F32), jnp.zeros((bq, dh), F32))
            state.append((first,) + lax.fori_loop(
                first, jnp.maximum(qi - 1, first),
                lambda kb, cr, hh=hh: block(hh, kb, cr[0], cr[1], cr[2], None), init))
        for hh, (first, psp, pg, dq) in enumerate(state):
            psp, pg, dq = block(hh, jnp.maximum(qi - 1, 0), psp, pg, dq, jnp.logical_and(qi > 0, first < qi))
            _, _, dq = block(hh, qi, psp, pg, dq, col < row)
            dq_ref[:, pl.ds(hh * dh, dh)] = dq.astype(BF16)

    once = dict(pipeline_mode=pl.Buffered(1))
    whole = lambda base: pl.BlockSpec((t, hp * dh), lambda h, qi: (0, base + h), **once)
    qblk = pl.BlockSpec((bq, hp * dh), lambda h, qi: (qi, h))
    acc = pl.BlockSpec((t, hp * dh), lambda h, qi: (0, h))
    return hosted_call(
        body, ride, name=name, grid=(nh // hp, t // bq),
        in_specs=[pl.BlockSpec(memory_space=pltpu.SMEM), qblk, whole(nh // hp), whole(2 * nh // hp), qblk,
                  pl.BlockSpec((hp, bq, LANES), lambda h, qi: (h, qi, 0))],
        out_specs=[qblk, acc, acc],
        out_shape=[SDS((t, nh * dh), BF16), SDS((t, nh * dh), F32), SDS((t, nh * dh), F32)],
        scratch_shapes=[], operands=(start, qkv, qkv, qkv, do, s_tot))


def all_gather(x, name, axis=None, first_layer_only=False):
    if axis is None:
        out_shape = (N_DEV,) + x.shape
    else:
        out_shape = x.shape[:axis] + (N_DEV * x.shape[axis],) + x.shape[axis + 1:]

    def body(x_full_ref, out_full_ref, send_sems, recv_sems, local_sem):
        mx, my, mc = lax.axis_index("x"), lax.axis_index("y"), lax.axis_index("c")
        me, sibling = (mx, my, mc), (mx, my, 1 - mc)
        chips = [(1 - mx, my), (mx, 1 - my), (1 - mx, 1 - my)]
        x_ref = x_full_ref.at[0] if first_layer_only else x_full_ref
        out_ref = out_full_ref.at[0] if first_layer_only else out_full_ref
        ax = axis - 1 if first_layer_only else axis

        def slot(px, py, pc):
            return _shard_of(out_ref, ax, 4 * px + 2 * py + pc, None if axis is None else x.shape[axis])

        def copy(k, block, to, src=None):
            return pltpu.make_async_remote_copy(
                src_ref=slot(*block) if src is None else src, dst_ref=slot(*block),
                send_sem=send_sems.at[k], recv_sem=recv_sems.at[k], device_id=to, device_id_type=MESH)

        mine = pltpu.make_async_copy(x_ref, slot(*me), local_sem)
        mine.start()
        first = [copy(0, me, sibling, src=x_ref)]
        first += [copy(1 + j, me, (*chip, mc), src=x_ref) for j, chip in enumerate(chips)]
        for cp in first:
            cp.start()
        passed = [copy(4 + j, (*chip, mc), sibling) for j, chip in enumerate(chips)]
        for j, chip in enumerate(chips):
            copy(1 + j, (*chip, mc), me).wait_recv()
            passed[j].start()
        copy(0, sibling, me).wait_recv()
        for j, chip in enumerate(chips):
            copy(4 + j, (*chip, 1 - mc), me).wait_recv()
        for cp in first + passed:
            cp.wait_send()
        mine.wait()

    return pl.pallas_call(
        body, name=name, out_shape=SDS(out_shape, x.dtype), in_specs=[ANY], out_specs=ANY,
        scratch_shapes=[pltpu.SemaphoreType.DMA((7,)), pltpu.SemaphoreType.DMA((7,)), pltpu.SemaphoreType.DMA],
    )(x)


def reduce_adamw(gs, w, m, v, name):
    s, r, c = gs.shape
    tr = _tile(r, max(16, (128 * 1024) // c), 16)

    def body(gs_ref, w_ref, m_ref, v_ref, g_out, d_out, m_out, v_out):
        g = gs_ref[0].astype(F32)
        for j in range(1, s):
            g = g + gs_ref[j].astype(F32)
        m_new = ADAM_B1 * m_ref[...] + (1.0 - ADAM_B1) * g
        v_new = ADAM_B2 * v_ref[...] + (1.0 - ADAM_B2) * (g * g)
        m_hat = m_new / (1.0 - ADAM_B1 ** ADAM_STEP)
        v_hat = v_new / (1.0 - ADAM_B2 ** ADAM_STEP)
        g_out[...] = g
        d_out[...] = -ADAM_LR * (m_hat / (jnp.sqrt(v_hat) + ADAM_EPS) + ADAM_WD * w_ref[...])
        m_out[...] = m_new
        v_out[...] = v_new

    row = pl.BlockSpec((tr, c), lambda i: (i, 0))
    return pl.pallas_call(
        body, name=name, grid=(r // tr,),
        in_specs=[pl.BlockSpec((s, tr, c), lambda i: (0, i, 0)), row, row, row], out_specs=[row] * 4,
        out_shape=[SDS((r, c), F32)] * 4, compiler_params=_cp("parallel"),
    )(gs, w, m, v)


BIG = {
    "w_in_rec": 2, "w_out_rec": 1, "w_qkv": 2, "w_o_attn": 1,
    "w_mlp_up": 2, "w_mlp_down": 1, "w_ple_proj": 2, "w_ple_gate": 1,
}
REPLICATED = ["norm_mix_g", "norm_mlp_g", "norm_ple_g", "norm_f_g", "conv_a_b", "ln_a_g", "ln_a_b", "conv_b_b",
              "w_rg_a", "b_rg_a", "w_rg_x", "b_rg_x", "rg_lambda"]
CONV_W = ["conv_a_w", "conv_b_w"]
WEIGHTS = ["norm_mix_g", "norm_mlp_g", "norm_ple_g", "norm_f_g", "w_in_rec", "conv_a_w", "conv_a_b", "ln_a_g",
           "ln_a_b", "conv_b_w", "conv_b_b", "w_rg_a", "b_rg_a", "w_rg_x", "b_rg_x", "rg_lambda", "w_out_rec",
           "w_qkv", "w_o_attn", "w_mlp_up", "w_mlp_down", "w_ple_proj", "w_ple_gate"]


def _pack(arrays):
    flat = jnp.concatenate([a.reshape(-1) for a in arrays])
    pad = (-flat.shape[0]) % (8 * LANES)
    return jnp.pad(flat, (0, pad)).reshape(-1, LANES)


def _unpack(packed, shapes, lead=()):
    flat = packed.reshape(lead + (-1,))
    out, pos = [], 0
    for shp in shapes:
        size = math.prod(shp)
        out.append(flat[..., pos:pos + size].reshape(lead + tuple(shp)))
        pos += size
    return out


def _step(x, p, loss_target, w, mom, var):
    dev = 4 * lax.axis_index("x") + 2 * lax.axis_index("y") + lax.axis_index("c")
    depth = w["norm_mix_g"].shape[0]
    h = x[0]
    nh = SB_HEADS

    wb = {n: w[n].astype(BF16) for n in BIG}
    wg = {n: None for n in BIG}
    wg["w_in_rec"] = all_gather(wb["w_in_rec"], name="ag0_w_in_rec", axis=BIG["w_in_rec"], first_layer_only=True)
    mix_names = lambda layer: ("w_in_rec", "w_out_rec") if layer % 2 == 0 else ("w_qkv", "w_o_attn")

    def gather_ride(targets):
        ride = Ride()
        for n, layer in targets:
            ride.gather(wb[n], BIG[n], layer, wg[n])
        return ride if targets else None

    def gathered(targets, outs):
        for (n, _), a in zip(targets, outs[len(outs) - len(targets):]):
            wg[n] = a
        return outs[:len(outs) - len(targets)]

    def mm_gathering(targets, *args, **kw):
        return gathered(targets, mm_w(*args, ride=gather_ride(targets), **kw))

    wide = dict(tm=256, tn=2048)
    conv_shapes = [w[n].shape for n in CONV_W]
    conv_all = all_gather(_pack([w[n] for n in CONV_W]), name="ag_conv_w")
    conv_full = [jnp.moveaxis(a, 0, -2).reshape(a.shape[1:-1] + (-1,))
                 for a in _unpack(conv_all, conv_shapes, lead=(N_DEV,))]
    conv_a_w, conv_b_w = conv_full
    vec = lambda a: a.reshape(1, -1)

    saved = []
    for i in range(depth):
        j = i // 2
        s = {"h0": h}
        in_name, out_name = mix_names(i)
        s["in_name"], s["out_name"] = in_name, out_name
        nxt = i + 1
        if nxt < depth:
            t_in, t_out = [(mix_names(nxt)[0], nxt // 2)], [(mix_names(nxt)[1], nxt // 2), ("w_ple_gate", nxt)]
            t_up, t_down, t_gate = [("w_mlp_down", nxt)], [("w_mlp_up", nxt)], [("w_ple_proj", nxt)]
        else:
            t_in = t_out = t_up = t_down = t_gate = []
        t_conf = t_rg = []
        if i == 0:
            t_in = t_in + [("w_out_rec", 0)]
            t_conf, t_rg = [("w_mlp_up", 0)], [("w_ple_gate", 0), ("w_ple_proj", 0)]
            t_out, t_up = [("w_mlp_down", 0)], t_up + t_out
        s["hn1"] = rms_fwd(h, w["norm_mix_g"][i], name="rms_fwd")
        if i % 2 == 0:
            (s["u"],) = mm_gathering(t_in, s["hn1"], wg[in_name], j, transpose_w=False, out_dtypes=[F32],
                                     name="mm_in_rec")
            ya, s["ya1"] = gathered(t_conf, conf_fwd(
                s["u"], conv_a_w[j], vec(w["conv_a_b"][j]), vec(w["ln_a_g"][j]), vec(w["ln_a_b"][j]),
                name="conf_fwd", ride=gather_ride(t_conf)))
            yb, s["hs"] = gathered(t_rg, rglru_fwd(
                s["u"], conv_b_w[j], vec(w["conv_b_b"][j]), w["w_rg_a"][j].astype(BF16), vec(w["b_rg_a"][j]),
                w["w_rg_x"][j].astype(BF16), vec(w["b_rg_x"][j]), vec(w["rg_lambda"][j]), name="rglru_fwd",
                ride=gather_ride(t_rg)))
            s["mix_in"] = jnp.concatenate([ya, yb], axis=1)
        else:
            (s["qkv"],) = mm_gathering(t_in, s["hn1"], wg[in_name], j, transpose_w=False, out_dtypes=[BF16],
                                       name="mm_qkv")
            s["mix_in"], s["s_tot"], s["start"] = attn_fwd(s["qkv"], nh, name="attn_fwd")
        (h,) = mm_gathering(t_out, s["mix_in"], wg[out_name], j, transpose_w=False, out_dtypes=[F32],
                            extras=[h], epilogue=lambda acc, res: (res + acc,), name="mm_mix_out", **wide)
        s["h1"] = h
        s["hn2"] = rms_fwd(h, w["norm_mlp_g"][i], name="rms_fwd")
        relu = lambda acc: jnp.maximum(acc, 0.0)
        s["up"], s["act"] = mm_gathering(t_up, s["hn2"], wg["w_mlp_up"], i, transpose_w=False,
                                         out_dtypes=[BF16, BF16], name="mm_mlp_up", tm=1024,
                                         epilogue=lambda acc: (acc, relu(acc) * relu(acc)))
        (h,) = mm_gathering(t_down, s["act"], wg["w_mlp_down"], i, transpose_w=False, out_dtypes=[F32],
                            tm=1024, extras=[h], epilogue=lambda acc, res: (res + acc,),
                            name="mm_mlp_down")
        s["h2"] = h
        s["hn3"] = rms_fwd(h, w["norm_ple_g"][i], name="rms_fwd")
        s["p"] = p[i, 0].astype(BF16)
        (s["pp"],) = mm_w(s["p"], wg["w_ple_proj"], i, transpose_w=False, out_dtypes=[F32], name="mm_ple_proj")
        h, s["gz"] = mm_gathering(t_gate, s["hn3"], wg["w_ple_gate"], i, transpose_w=False,
                                  out_dtypes=[F32, BF16], extras=[h, s["pp"]], name="mm_ple_gate", **wide,
                                  epilogue=lambda acc, res, pp: (res + pp * jax.nn.sigmoid(acc), acc))
        saved.append(s)

    dh, dg_f, loss_part = loss_head(h, w["norm_f_g"], loss_target[0], name="loss_head")
    loss = lax.psum(loss_part[0, 0], ("x", "y", "c"))

    acc = {n: None for n in BIG}
    part = {n: [None] * w[n].shape[0] for n in REPLICATED + CONV_W if n != "norm_f_g"}
    part["norm_f_g"] = dg_f[0]

    recv = {n: None for n in BIG}

    def exchanging(targets):
        ride = Ride()
        for n, layer in targets:
            ride.exchange(acc[n], BIG[n], layer, recv[n])
        return ride

    def landed(targets, arrays):
        for (n, _), a in zip(targets, arrays):
            recv[n] = a

    def dw(name, a, b, layer, targets=()):
        outs = mm_dw(a, b, acc[name], layer, wg[name].shape[0], name=f"dw_{name}",
                     ride=exchanging(targets) if targets else None)
        acc[name] = outs[0] if targets else outs
        landed(targets, outs[1:] if targets else [])

    def mm_exchanging(targets, *args, **kw):
        outs = mm_w(*args, ride=exchanging(targets), **kw)
        landed(targets, outs[len(outs) - len(targets):])
        return outs[:len(outs) - len(targets)]

    for i in reversed(range(depth)):
        j = i // 2
        s = saved[i]
        in_name, out_name = s["in_name"], s["out_name"]
        d_pp, d_gz = ple_bwd(dh, s["gz"], s["pp"], name="ple_bwd")
        dw("w_ple_proj", s["p"], d_pp, i)
        dw("w_ple_gate", s["hn3"], d_gz, i)
        (d_hn3,) = mm_exchanging([("w_ple_proj", i), ("w_ple_gate", i)], d_gz, wg["w_ple_gate"], i,
                                 transpose_w=True, out_dtypes=[BF16], name="mmT_ple_gate", tn=2048)
        dh, dh_b, dg = rms_bwd(s["h2"], w["norm_ple_g"][i], d_hn3, dh, name="rms_bwd")
        part["norm_ple_g"][i] = dg[0]
        (d_up,) = mm_w(dh_b, wg["w_mlp_down"], i, transpose_w=True, out_dtypes=[BF16], tm=1024,
                       extras=[s["up"]], name="mmT_mlp_down",
                       epilogue=lambda acc_, up: (acc_ * (2.0 * jnp.maximum(up.astype(F32), 0.0)),))
        dw("w_mlp_down", s["act"], dh_b, i)
        dw("w_mlp_up", s["hn2"], d_up, i, targets=[("w_mlp_down", i)])
        (d_hn2,) = mm_w(d_up, wg["w_mlp_up"], i, transpose_w=True, out_dtypes=[BF16], tm=1024, name="mmT_mlp_up")
        up_grad = [("w_mlp_up", i)]
        dh, dh_b, dg = rms_bwd(s["h1"], w["norm_mlp_g"][i], d_hn2, dh, name="rms_bwd")
        part["norm_mlp_g"][i] = dg[0]
        dw(out_name, s["mix_in"], dh_b, j)
        if i % 2 == 0:
            (d_cat,) = mm_exchanging([(out_name, j)], dh_b, wg[out_name], j, transpose_w=True,
                                     out_dtypes=[BF16], name="mmT_mix_out", tn=2048)
            du_a, pg_a, *sent = conf_bwd(s["u"], s["ya1"], d_cat, conv_a_w[j], vec(w["ln_a_g"][j]),
                                         vec(w["ln_a_b"][j]), name="conf_bwd", ride=exchanging(up_grad))
            landed(up_grad, sent)
            du_b, pg_b, d_wa, d_wx = rglru_bwd(
                s["u"], s["hs"], d_cat, conv_b_w[j], vec(w["conv_b_b"][j]), w["w_rg_a"][j].astype(BF16),
                vec(w["b_rg_a"][j]), w["w_rg_x"][j].astype(BF16), vec(w["b_rg_x"][j]), vec(w["rg_lambda"][j]),
                name="rglru_bwd")
            ka, kb = conv_a_w.shape[1], conv_b_w.shape[1]
            part["conv_a_w"][j], part["conv_a_b"][j] = pg_a[:ka], pg_a[ka]
            part["ln_a_g"][j], part["ln_a_b"][j] = pg_a[ka + 1], pg_a[ka + 2]
            part["conv_b_w"][j], part["conv_b_b"][j] = pg_b[:kb], pg_b[kb]
            part["b_rg_a"][j], part["b_rg_x"][j], part["rg_lambda"][j] = pg_b[kb + 1], pg_b[kb + 2], pg_b[kb + 3]
            part["w_rg_a"][j], part["w_rg_x"][j] = d_wa, d_wx
            d_mix = jnp.concatenate([du_a, du_b], axis=1)
        else:
            (d_o,) = mm_exchanging([(out_name, j)], dh_b, wg[out_name], j, transpose_w=True,
                                   out_dtypes=[BF16], name="mmT_mix_out", tn=2048)
            dq, dk, dv, *sent = attn_bwd(s["qkv"], d_o, s["s_tot"], s["start"], nh, name="attn_bwd",
                                         ride=exchanging(up_grad))
            landed(up_grad, sent)
            d_mix = jnp.concatenate([dq, dk.astype(BF16), dv.astype(BF16)], axis=1)
        dw(in_name, s["hn1"], d_mix, j)
        (d_hn1,) = mm_exchanging([(in_name, j)], d_mix, wg[in_name], j, transpose_w=True, out_dtypes=[BF16],
                                 name="mmT_mix_in")
        dh, _, dg = rms_bwd(s["h0"], w["norm_mix_g"][i], d_hn1, dh, name="rms_bwd")
        part["norm_mix_g"][i] = dg[0]

    grads, deltas, new_m, new_v = {}, {}, {}, {}

    def finish(name, outs, shape):
        for d, o in zip((grads, deltas, new_m, new_v), outs):
            d[name] = o.reshape(shape)

    for n in BIG:
        _, nl, r, c = recv[n].shape
        flat = lambda a: a.reshape(nl * r, c)
        finish(n, reduce_adamw(recv[n].reshape(N_DEV, nl * r, c), flat(w[n]), flat(mom[n]), flat(var[n]),
                               name=f"adamw_{n}"), w[n].shape)

    full = {n: (part[n] if n == "norm_f_g" else jnp.stack(part[n])) for n in REPLICATED + CONV_W}
    rep = _pack([full[n] for n in REPLICATED])
    conv = _pack([full[n] for n in CONV_W])
    small = all_gather(jnp.concatenate([rep, conv]), name="ag_small_grads")
    rep_shapes = [w[n].shape for n in REPLICATED]
    outs = reduce_adamw(small[:, :rep.shape[0]], _pack([w[n] for n in REPLICATED]),
                        _pack([mom[n] for n in REPLICATED]), _pack([var[n] for n in REPLICATED]), name="adamw_small")
    for d, o in zip((grads, deltas, new_m, new_v), outs):
        for n, a in zip(REPLICATED, _unpack(o, rep_shapes)):
            d[n] = a
    conv_parts = _unpack(small[:, rep.shape[0]:], [full[n].shape for n in CONV_W], lead=(N_DEV,))
    width = w["conv_a_w"].shape[-1]
    mine = [lax.dynamic_slice_in_dim(a, dev * width, width, axis=a.ndim - 1) for a in conv_parts]
    packed = jnp.stack([_pack([a[d] for a in mine]) for d in range(N_DEV)])
    outs = reduce_adamw(packed, _pack([w[n] for n in CONV_W]), _pack([mom[n] for n in CONV_W]),
                        _pack([var[n] for n in CONV_W]), name="adamw_conv_w")
    for d, o in zip((grads, deltas, new_m, new_v), outs):
        for n, a in zip(CONV_W, _unpack(o, conv_shapes)):
            d[n] = a
    return loss, dh[None], grads, deltas, new_m, new_v


def kernel(x, p, norm_mix_g, norm_mlp_g, norm_ple_g, norm_f_g, w_in_rec, conv_a_w, conv_a_b, ln_a_g, ln_a_b, conv_b_w, conv_b_b, w_rg_a, b_rg_a, w_rg_x, b_rg_x, rg_lambda, w_out_rec, w_qkv, w_o_attn, w_mlp_up, w_mlp_down, w_ple_proj, w_ple_gate, loss_target, m_norm_mix_g, m_norm_mlp_g, m_norm_ple_g, m_norm_f_g, m_w_in_rec, m_conv_a_w, m_conv_a_b, m_ln_a_g, m_ln_a_b, m_conv_b_w, m_conv_b_b, m_w_rg_a, m_b_rg_a, m_w_rg_x, m_b_rg_x, m_rg_lambda, m_w_out_rec, m_w_qkv, m_w_o_attn, m_w_mlp_up, m_w_mlp_down, m_w_ple_proj, m_w_ple_gate, v_norm_mix_g, v_norm_mlp_g, v_norm_ple_g, v_norm_f_g, v_w_in_rec, v_conv_a_w, v_conv_a_b, v_ln_a_g, v_ln_a_b, v_conv_b_w, v_conv_b_b, v_w_rg_a, v_b_rg_a, v_w_rg_x, v_b_rg_x, v_rg_lambda, v_w_out_rec, v_w_qkv, v_w_o_attn, v_w_mlp_up, v_w_mlp_down, v_w_ple_proj, v_w_ple_gate):
    given = dict(locals())
    w = {n: given[n] for n in WEIGHTS}
    mom = {n: given["m_" + n] for n in WEIGHTS}
    var = {n: given["v_" + n] for n in WEIGHTS}
    loss, grad_x, grads, deltas, new_m, new_v = _step(x, p, loss_target, w, mom, var)
    return (loss, grad_x, *[grads[n] for n in WEIGHTS], *[deltas[n] for n in WEIGHTS],
            *[new_m[n] for n in WEIGHTS], *[new_v[n] for n in WEIGHTS])
```

```python
import functools
import math

import jax
import jax.numpy as jnp
from jax import lax
from jax.experimental import pallas as pl
from jax.experimental.pallas import tpu as pltpu

F32, BF16 = jnp.float32, jnp.bfloat16
EPS = 1e-6
N_DEV = 8
SB_HEADS = 16
RG_C = 8.0
HALO_A = 32
HALO_B = 8
CONV_ROWS = 64
LANES = 128
VMEM_LIMIT = 48 * 1024 * 1024
ADAM_LR, ADAM_B1, ADAM_B2, ADAM_EPS, ADAM_WD, ADAM_STEP = 0.001, 0.9, 0.999, 1e-08, 0.01, 10
MESH = pl.DeviceIdType.MESH
SDS = jax.ShapeDtypeStruct
ANY = pl.BlockSpec(memory_space=pl.ANY)


def _cp(*sem):
    return pltpu.CompilerParams(dimension_semantics=sem, vmem_limit_bytes=VMEM_LIMIT)


def _tile(dim, pref, align=LANES):
    if dim <= pref:
        return dim
    t = (pref // align) * align
    while t >= align:
        if dim % t == 0:
            return t
        t -= align
    return dim


def _softplus(z):
    return jnp.maximum(z, 0.0) + jnp.log(1.0 + jnp.exp(-jnp.abs(z)))


def _expm1(x):
    t = x * (1.0 + x * (0.5 + x * (1.0 / 6.0 + x * (1.0 / 24.0 + x * (1.0 / 120.0)))))
    return jnp.where(jnp.abs(x) < 0.1, t, jnp.exp(x) - 1.0)


_GELU_C = math.sqrt(2.0 / math.pi)


def _gelu(x):
    return 0.5 * x * (1.0 + jnp.tanh(_GELU_C * (x + 0.044715 * x * x * x)))


def _gelu_grad(x):
    th = jnp.tanh(_GELU_C * (x + 0.044715 * x * x * x))
    return 0.5 * (1.0 + th) + 0.5 * x * (1.0 - th * th) * _GELU_C * (1.0 + 3.0 * 0.044715 * x * x)


def _dot(a, b):
    return jnp.dot(a, b, preferred_element_type=F32)


def _dot_nt(a, b):
    return lax.dot_general(a, b, (((1,), (1,)), ((), ())), preferred_element_type=F32)


def _dot_tn(a, b):
    return lax.dot_general(a, b, (((0,), (0,)), ((), ())), preferred_element_type=F32)


def _shard_of(ref, axis, dev, size):
    if axis is None:
        return ref.at[dev]
    return ref.at[(slice(None),) * axis + (pl.ds(pl.multiple_of(dev * size, size), size),)]


def _peer(k, mx, my, mc):
    return (1 - mx if k & 4 else mx, 1 - my if k & 2 else my, 1 - mc if k & 1 else mc)


class Ride:
    def __init__(self):
        self.items = []

    def gather(self, shard, axis, layer, whole_prev):
        shape = shard.shape[:axis] + (N_DEV * shard.shape[axis],) + shard.shape[axis + 1:]
        self.items.append(("gather", shard, whole_prev, SDS(shape, shard.dtype), axis, layer, shard.shape[axis]))
        return self

    def exchange(self, grad, axis, layer, recv_prev):
        size = grad.shape[axis] // N_DEV
        shape = (N_DEV,) + grad.shape[:axis] + (size,) + grad.shape[axis + 1:]
        self.items.append(("exchange", grad, recv_prev, SDS(shape, grad.dtype), axis, layer, size))
        return self

    def operands(self):
        return [a for it in self.items for a in ([it[1]] if it[2] is None else [it[1], it[2]])]

    def out_shapes(self):
        return [it[3] for it in self.items]

    def aliases(self, first_in, first_out):
        out, pos = {}, first_in
        for t, it in enumerate(self.items):
            pos += 1
            if it[2] is not None:
                out[pos] = first_out + t
                pos += 1
        return out

    def scratch(self):
        n = len(self.items)
        return [pltpu.SemaphoreType.DMA((n, N_DEV - 1)), pltpu.SemaphoreType.DMA((n, N_DEV - 1)),
                pltpu.SemaphoreType.DMA((n,))]

    def plan(self, in_refs, out_refs, send_sems, recv_sems, local_sems):
        mx, my, mc = lax.axis_index("x"), lax.axis_index("y"), lax.axis_index("c")
        me, sibling = (mx, my, mc), (mx, my, 1 - mc)
        lin = lambda d: 4 * d[0] + 2 * d[1] + d[2]
        first, middle, last, pos = [], [], [], 0
        for t, (kind, _, prev, _, axis, layer, size) in enumerate(self.items):
            src = in_refs[pos].at[layer]
            pos += 1 if prev is None else 2

            def remote(act, k, src_fn, dst_fn, to, t=t):
                return lambda: getattr(pltpu.make_async_remote_copy(
                    src_ref=src_fn(), dst_ref=dst_fn(), send_sem=send_sems.at[t, k], recv_sem=recv_sems.at[t, k],
                    device_id=to, device_id_type=MESH), act)()

            def local(act, src_fn, dst_fn, t=t):
                return lambda: getattr(pltpu.make_async_copy(src_fn(), dst_fn(), local_sems.at[t]), act)()

            src_fn = lambda src=src: src
            if kind == "exchange":
                dst_fn = lambda t=t, layer=layer: out_refs[t].at[lin(me), layer]
                shard = lambda d, src=src, axis=axis, size=size: (lambda: _shard_of(src, axis - 1, lin(d), size))
                for act, phase in (("start", first), ("wait", last)):
                    phase.append(local(act, shard(me), dst_fn))
                    for k in range(1, N_DEV):
                        peer = _peer(k, mx, my, mc)
                        phase.append(remote(act, k - 1, shard(peer), dst_fn, peer))
                continue
            whole = out_refs[t].at[layer]
            slot = lambda d, whole=whole, axis=axis, size=size: (lambda: _shard_of(whole, axis - 1, lin(d), size))
            chips = [(1 - mx, my), (mx, 1 - my), (1 - mx, 1 - my)]
            own = [(0, sibling)] + [(1 + j, (*chip, mc)) for j, chip in enumerate(chips)]
            first.append(local("start", src_fn, slot(me)))
            first += [remote("start", k, src_fn, slot(me), to) for k, to in own]
            for j, chip in enumerate(chips):
                landed = slot((*chip, mc))
                middle += [remote("wait_recv", 1 + j, src_fn, landed, me), remote("start", 4 + j, landed, landed, sibling)]
                last.append(remote("wait_send", 4 + j, landed, landed, sibling))
            last.append(remote("wait_recv", 0, src_fn, slot(sibling), me))
            last += [remote("wait_recv", 4 + j, src_fn, slot((*chip, 1 - mc)), me) for j, chip in enumerate(chips)]
            last += [remote("wait_send", k, src_fn, slot(me), to) for k, to in own]
            last.append(local("wait", src_fn, slot(me)))
        return first, middle, last


def _ride_hooks(ride, grid, in_refs, out_refs, sems):
    step, total = 0, 1
    for d, g in enumerate(grid):
        step, total = step * g + pl.program_id(d), total * g

    def at(which, when):
        @pl.when(step == when)
        def _():
            for act in ride.plan(in_refs, out_refs, *sems)[which]:
                act()

    def start():
        at(0, 0)
        at(1, total // 2)

    def finish():
        at(2, total - 1)

    return start, finish


def hosted_call(body, ride, *, name, grid, in_specs, out_specs, out_shape, scratch_shapes, operands):
    if not ride:
        return pl.pallas_call(
            body, name=name, grid=grid, in_specs=in_specs, out_specs=out_specs, out_shape=out_shape,
            scratch_shapes=scratch_shapes, compiler_params=_cp(*["arbitrary"] * len(grid)))(*operands)
    n_in, n_out, n_scratch = len(in_specs), len(out_specs), len(scratch_shapes)
    ride_in = ride.operands()

    def carrying(*refs):
        ins, refs = refs[:n_in], refs[n_in:]
        ride_in_refs, refs = refs[:len(ride_in)], refs[len(ride_in):]
        outs, refs = refs[:n_out], refs[n_out:]
        ride_out_refs, refs = refs[:len(ride.items)], refs[len(ride.items):]
        start, finish = _ride_hooks(ride, grid, ride_in_refs, ride_out_refs, refs[n_scratch:])
        start()
        body(*ins, *outs, *refs[:n_scratch])
        finish()

    return pl.pallas_call(
        carrying, name=name, grid=grid, in_specs=list(in_specs) + [ANY] * len(ride_in),
        out_specs=list(out_specs) + [ANY] * len(ride.items), out_shape=list(out_shape) + ride.out_shapes(),
        scratch_shapes=list(scratch_shapes) + ride.scratch(), input_output_aliases=ride.aliases(n_in, n_out),
        compiler_params=_cp(*["arbitrary"] * len(grid)))(*operands, *ride_in)


def mm_w(a, w, layer, *, transpose_w, out_dtypes, name, extras=(), epilogue=None, ride=None,
         tm=512, tn=1024, tk=2048):
    m, k_dim = a.shape
    _, rows, cols = w.shape
    n = rows if transpose_w else cols
    assert k_dim == (cols if transpose_w else rows), (a.shape, w.shape)
    tm, tn, tk = _tile(m, tm, 8), _tile(n, tn), _tile(k_dim, tk)
    nk = k_dim // tk
    grid = (m // tm, n // tn, nk)
    n_extra, n_out = len(extras), len(out_dtypes)
    ride_in = ride.operands() if ride else []
    n_ride_out = len(ride.items) if ride else 0
    if epilogue is None:
        epilogue = lambda acc: (acc,)
    dot = _dot_nt if transpose_w else _dot

    def body(a_ref, w_ref, *rest):
        extra_refs, rest = rest[:n_extra], rest[n_extra:]
        ride_in_refs, rest = rest[:len(ride_in)], rest[len(ride_in):]
        out_refs, rest = rest[:n_out], rest[n_out:]
        ride_out_refs, scratch = rest[:n_ride_out], rest[n_ride_out:]
        if ride:
            start, finish = _ride_hooks(ride, grid, ride_in_refs, ride_out_refs, scratch[-3:])
            start()

        def write(acc):
            outs = epilogue(acc, *[r[...] for r in extra_refs])
            for o_ref, o in zip(out_refs, outs):
                o_ref[...] = o.astype(o_ref.dtype)

        if nk == 1:
            write(dot(a_ref[...], w_ref[...]))
        else:
            acc = scratch[0]
            k = pl.program_id(2)

            @pl.when(k == 0)
            def _():
                acc[...] = jnp.zeros_like(acc)

            acc[...] += dot(a_ref[...], w_ref[...])

            @pl.when(k == nk - 1)
            def _():
                write(acc[...])

        if ride:
            finish()

    if transpose_w:
        w_spec = pl.BlockSpec((None, tn, tk), lambda i, j, k: (layer, j, k))
    else:
        w_spec = pl.BlockSpec((None, tk, tn), lambda i, j, k: (layer, k, j))
    tile_spec = pl.BlockSpec((tm, tn), lambda i, j, k: (i, j))
    return pl.pallas_call(
        body, name=name, grid=grid,
        in_specs=[pl.BlockSpec((tm, tk), lambda i, j, k: (i, k)), w_spec] + [tile_spec] * n_extra
        + [ANY] * len(ride_in),
        out_specs=[tile_spec] * n_out + [ANY] * n_ride_out,
        out_shape=[SDS((m, n), dt) for dt in out_dtypes] + (ride.out_shapes() if ride else []),
        scratch_shapes=([pltpu.VMEM((tm, tn), F32)] if nk > 1 else []) + (ride.scratch() if ride else []),
        input_output_aliases=ride.aliases(2 + n_extra, n_out) if ride else {},
        compiler_params=_cp("arbitrary", "arbitrary", "arbitrary") if ride
        else _cp("parallel", "parallel", "arbitrary"),
    )(a, w, *extras, *ride_in)


def mm_dw(a, b, buf, layer, n_layers, name, ride=None, tm=512, tn=1024, tk=2048):
    tokens, m = a.shape
    _, n = b.shape
    tm, tn, tk = _tile(m, tm), _tile(n, tn), _tile(tokens, tk)
    nk = tokens // tk
    grid = (m // tm, n // tn, nk)
    n_buf = 0 if buf is None else 1
    ride_in = ride.operands() if ride else []
    n_ride_out = len(ride.items) if ride else 0

    def body(a_ref, b_ref, *rest):
        rest = rest[n_buf:]
        ride_in_refs, rest = rest[:len(ride_in)], rest[len(ride_in):]
        o_ref, rest = rest[0], rest[1:]
        ride_out_refs, scratch = rest[:n_ride_out], rest[n_ride_out:]
        acc = scratch[0]
        if ride:
            start, finish = _ride_hooks(ride, grid, ride_in_refs, ride_out_refs, scratch[-3:])
            start()
        k = pl.program_id(2)

        @pl.when(k == 0)
        def _():
            acc[...] = jnp.zeros_like(acc)

        acc[...] += _dot_tn(a_ref[...], b_ref[...])

        @pl.when(k == nk - 1)
        def _():
            o_ref[...] = acc[...].astype(BF16)

        if ride:
            finish()

    aliases = {} if buf is None else {2: 0}
    if ride:
        aliases.update(ride.aliases(2 + n_buf, 1))
    outs = pl.pallas_call(
        body, name=name, grid=grid,
        in_specs=[pl.BlockSpec((tk, tm), lambda i, j, k: (k, i)),
                  pl.BlockSpec((tk, tn), lambda i, j, k: (k, j))] + [ANY] * (n_buf + len(ride_in)),
        out_specs=[pl.BlockSpec((None, tm, tn), lambda i, j, k: (layer, i, j))] + [ANY] * n_ride_out,
        out_shape=[SDS((n_layers, m, n), BF16)] + (ride.out_shapes() if ride else []),
        scratch_shapes=[pltpu.VMEM((tm, tn), F32)] + (ride.scratch() if ride else []),
        input_output_aliases=aliases,
        compiler_params=_cp("arbitrary", "arbitrary", "arbitrary") if ride
        else _cp("parallel", "parallel", "arbitrary"),
    )(a, b, *([] if buf is None else [buf]), *ride_in)
    return outs if ride else outs[0]


def rms_fwd(h, g, name):
    t, d = h.shape
    tt = _tile(t, 512, 8)

    def body(h_ref, g_ref, o_ref):
        x = h_ref[...]
        r = lax.rsqrt(jnp.mean(x * x, axis=-1, keepdims=True) + EPS)
        o_ref[...] = (x * r * g_ref[...]).astype(BF16)

    row = pl.BlockSpec((tt, d), lambda i: (i, 0))
    return pl.pallas_call(
        body, name=name, grid=(t // tt,),
        in_specs=[row, pl.BlockSpec((1, d), lambda i: (0, 0))], out_specs=row,
        out_shape=SDS((t, d), BF16), compiler_params=_cp("parallel"),
    )(h, g.reshape(1, d))


def _rms_bwd_math(x, g, dhn):
    r = lax.rsqrt(jnp.mean(x * x, axis=-1, keepdims=True) + EPS)
    xn = x * r
    dxn = dhn * g
    dx = r * (dxn - xn * jnp.mean(dxn * xn, axis=-1, keepdims=True))
    return dx, jnp.sum(dhn * xn, axis=0, keepdims=True)


def rms_bwd(h, g, dhn, dres, name):
    t, d = h.shape
    tt = _tile(t, 256, 8)

    def body(h_ref, g_ref, dhn_ref, dres_ref, dh_ref, dhb_ref, dg_ref):
        @pl.when(pl.program_id(0) == 0)
        def _():
            dg_ref[...] = jnp.zeros_like(dg_ref)

        dx, dg = _rms_bwd_math(h_ref[...], g_ref[...], dhn_ref[...].astype(F32))
        dh = dres_ref[...] + dx
        dh_ref[...] = dh
        dhb_ref[...] = dh.astype(BF16)
        dg_ref[...] += dg

    row = pl.BlockSpec((tt, d), lambda i: (i, 0))
    vec = pl.BlockSpec((1, d), lambda i: (0, 0))
    return pl.pallas_call(
        body, name=name, grid=(t // tt,),
        in_specs=[row, vec, row, row], out_specs=[row, row, vec],
        out_shape=[SDS((t, d), F32), SDS((t, d), BF16), SDS((1, d), F32)],
        compiler_params=_cp("arbitrary"),
    )(h, g.reshape(1, d), dhn, dres)


def loss_head(h, g, target, name):
    t, d = h.shape
    tt = _tile(t, 256, 8)

    def body(h_ref, g_ref, t_ref, dh_ref, dg_ref, loss_ref):
        @pl.when(pl.program_id(0) == 0)
        def _():
            dg_ref[...] = jnp.zeros_like(dg_ref)
            loss_ref[...] = jnp.zeros_like(loss_ref)

        x, gain = h_ref[...], g_ref[...]
        r = lax.rsqrt(jnp.mean(x * x, axis=-1, keepdims=True) + EPS)
        err = x * r * gain - t_ref[...]
        loss_ref[...] += 0.5 * jnp.sum(jnp.mean(err * err, axis=-1, keepdims=True))
        dx, dg = _rms_bwd_math(x, gain, err * (1.0 / d))
        dh_ref[...] = dx
        dg_ref[...] += dg

    row = pl.BlockSpec((tt, d), lambda i: (i, 0))
    vec = pl.BlockSpec((1, d), lambda i: (0, 0))
    return pl.pallas_call(
        body, name=name, grid=(t // tt,),
        in_specs=[row, vec, row],
        out_specs=[row, vec, pl.BlockSpec((8, LANES), lambda i: (0, 0))],
        out_shape=[SDS((t, d), F32), SDS((1, d), F32), SDS((8, LANES), F32)],
        compiler_params=_cp("arbitrary"),
    )(h, g.reshape(1, d), target)


def ple_bwd(dh, gz, pp, name):
    t, d = dh.shape
    tt = _tile(t, 256, 8)

    def body(dh_ref, gz_ref, pp_ref, dpp_ref, dgz_ref):
        g = dh_ref[...]
        gate = jax.nn.sigmoid(gz_ref[...].astype(F32))
        dpp_ref[...] = (g * gate).astype(BF16)
        dgz_ref[...] = (g * pp_ref[...].astype(F32) * gate * (1.0 - gate)).astype(BF16)

    row = pl.BlockSpec((tt, d), lambda i: (i, 0))
    return pl.pallas_call(
        body, name=name, grid=(t // tt,), in_specs=[row, row, row], out_specs=[row, row],
        out_shape=[SDS((t, d), BF16), SDS((t, d), BF16)], compiler_params=_cp("parallel"),
    )(dh, gz, pp)


def _layer_norm_parts(y1):
    mu = jnp.mean(y1, axis=-1, keepdims=True)
    dlt = y1 - mu
    rstd = lax.rsqrt(jnp.mean(dlt * dlt, axis=-1, keepdims=True) + EPS)
    return dlt * rstd, rstd


def _fill_shifted(buf, shifted, tt):
    for r in range(1, 8):
        shifted[r - 1] = buf[pl.ds(r, tt + HALO_A - 8), :]


def _rows_from(buf, shifted, start, tt, cs):
    q, r = divmod(start, 8)
    if r == 0:
        return buf[pl.ds(8 * q, tt), cs]
    return shifted[r - 1, pl.ds(8 * q, tt), cs]


def conf_fwd(u, cw, cb, lg, lb, name, ride=None):
    t = u.shape[0]
    ka, c = cw.shape
    tt = _tile(t, 256, HALO_A)
    hb = tt // HALO_A
    off = HALO_A - (ka - 1)
    grp = _tile(tt, CONV_ROWS, 8)

    def body(av_ref, ag_ref, pv_ref, pg_ref, cw_ref, cb_ref, lg_ref, lb_ref, ya_ref, ya1_ref, buf, shifted):
        live = (pl.program_id(0) > 0).astype(F32)
        buf[0:HALO_A, :] = pv_ref[...] * jax.nn.sigmoid(pg_ref[...]) * live
        buf[HALO_A:, :] = av_ref[...] * jax.nn.sigmoid(ag_ref[...])
        _fill_shifted(buf, shifted, tt)
        for c0 in range(0, c, LANES):
            cs = pl.ds(c0, LANES)
            for g0 in range(0, tt, grp):
                acc = jnp.broadcast_to(cb_ref[:, cs], (grp, LANES))
                for k in range(ka):
                    acc = acc + cw_ref[k:k + 1, cs] * _rows_from(buf, shifted, off + k + g0, grp, cs)
                ya1_ref[pl.ds(g0, grp), cs] = acc
        yn, _ = _layer_norm_parts(ya1_ref[...])
        y2 = yn * lg_ref[...] + lb_ref[...]
        ya_ref[...] = (y2 * jax.nn.sigmoid(y2)).astype(BF16)

    cur = lambda col: pl.BlockSpec((tt, c), lambda i: (i, col))
    prev = lambda col: pl.BlockSpec((HALO_A, c), lambda i: (jnp.maximum(i * hb - 1, 0), col))
    vec = pl.BlockSpec((1, c), lambda i: (0, 0))
    return hosted_call(
        body, ride, name=name, grid=(t // tt,),
        in_specs=[cur(0), cur(1), prev(0), prev(1), pl.BlockSpec((ka, c), lambda i: (0, 0)), vec, vec, vec],
        out_specs=[pl.BlockSpec((tt, c), lambda i: (i, 0))] * 2,
        out_shape=[SDS((t, c), BF16), SDS((t, c), F32)],
        scratch_shapes=[pltpu.VMEM((tt + HALO_A, c), F32), pltpu.VMEM((7, tt + HALO_A - 8, c), F32)],
        operands=(u, u, u, u, cw, cb, lg, lb))


def conf_bwd(u, ya1, dcat, cw, lg, lb, name, ride=None):
    t = u.shape[0]
    ka, c = cw.shape
    tt = _tile(t, 256, HALO_A)
    nt, hb = t // tt, tt // HALO_A
    off = HALO_A - (ka - 1)
    grp = _tile(tt, CONV_ROWS, 8)

    def body(av_ref, ag_ref, pv_ref, pg_ref, y1_ref, dya_ref, cw_ref, lg_ref, lb_ref,
             du_ref, pgrad_ref, ybuf, dbuf, carry, yshift, dshift):
        i = pl.program_id(0)

        @pl.when(i == 0)
        def _():
            carry[...] = jnp.zeros_like(carry)
            pgrad_ref[...] = jnp.zeros_like(pgrad_ref)

        live = (i < nt - 1).astype(F32)
        av = av_ref[...]
        sg = jax.nn.sigmoid(ag_ref[...])
        ybuf[0:HALO_A, :] = pv_ref[...] * jax.nn.sigmoid(pg_ref[...]) * live
        ybuf[HALO_A:, :] = av * sg
        yn, rstd = _layer_norm_parts(y1_ref[...])
        gain = lg_ref[...]
        y2 = yn * gain + lb_ref[...]
        s2 = jax.nn.sigmoid(y2)
        dy2 = dya_ref[...].astype(F32) * (s2 * (1.0 + y2 * (1.0 - s2)))
        pgrad_ref[ka + 1:ka + 2, :] += jnp.sum(dy2 * yn, axis=0, keepdims=True)
        pgrad_ref[ka + 2:ka + 3, :] += jnp.sum(dy2, axis=0, keepdims=True)
        dyn = dy2 * gain
        dy1 = rstd * (dyn - jnp.mean(dyn, axis=-1, keepdims=True)
                      - yn * jnp.mean(dyn * yn, axis=-1, keepdims=True))
        pgrad_ref[ka:ka + 1, :] += jnp.sum(dy1, axis=0, keepdims=True)
        dbuf[0:tt, :] = dy1
        dbuf[tt:, :] = carry[...]
        carry[...] = dbuf[0:HALO_A, :]
        _fill_shifted(ybuf, yshift, tt)
        _fill_shifted(dbuf, dshift, tt)
        groups = range(0, tt, grp)
        for c0 in range(0, c, LANES):
            cs = pl.ds(c0, LANES)
            accs = [jnp.zeros((grp, LANES), F32) for _ in groups]
            for k in range(ka):
                tap = cw_ref[k:k + 1, cs]
                dw_k = jnp.zeros((8, LANES), F32)
                for gi, g0 in enumerate(groups):
                    prod = dbuf[pl.ds(g0, grp), cs] * _rows_from(ybuf, yshift, off + k + g0, grp, cs)
                    for r0 in range(0, grp, 8):
                        dw_k = dw_k + prod[r0:r0 + 8]
                    accs[gi] = accs[gi] + tap * _rows_from(dbuf, dshift, ka - 1 - k + g0, grp, cs)
                pgrad_ref[k:k + 1, cs] += jnp.sum(dw_k, axis=0, keepdims=True)
            for acc, g0 in zip(accs, groups):
                sgc, avc = sg[g0:g0 + grp, c0:c0 + LANES], av[g0:g0 + grp, c0:c0 + LANES]
                du_ref[pl.ds(g0, grp), cs] = (acc * sgc).astype(BF16)
                du_ref[pl.ds(g0, grp), pl.ds(c + c0, LANES)] = (acc * avc * sgc * (1.0 - sgc)).astype(BF16)

    cur = lambda col: pl.BlockSpec((tt, c), lambda i: (nt - 1 - i, col))
    prev = lambda col: pl.BlockSpec((HALO_A, c), lambda i: (jnp.maximum((nt - 1 - i) * hb - 1, 0), col))
    vec = pl.BlockSpec((1, c), lambda i: (0, 0))
    return hosted_call(
        body, ride, name=name, grid=(nt,),
        in_specs=[cur(0), cur(1), prev(0), prev(1), cur(0), cur(0),
                  pl.BlockSpec((ka, c), lambda i: (0, 0)), vec, vec],
        out_specs=[pl.BlockSpec((tt, 2 * c), lambda i: (nt - 1 - i, 0)),
                   pl.BlockSpec((ka + 3, c), lambda i: (0, 0))],
        out_shape=[SDS((t, 2 * c), BF16), SDS((ka + 3, c), F32)],
        scratch_shapes=[pltpu.VMEM((tt + HALO_A, c), F32), pltpu.VMEM((tt + HALO_A, c), F32),
                        pltpu.VMEM((HALO_A, c), F32), pltpu.VMEM((7, tt + HALO_A - 8, c), F32),
                        pltpu.VMEM((7, tt + HALO_A - 8, c), F32)],
        operands=(u, u, u, u, ya1, dcat, cw, lg, lb))


def _rg_gates(xc, wa_ref, ba, wx_ref, bx, sp, nh, hd):
    parts = []
    for h in range(nh):
        hs = slice(h * hd, (h + 1) * hd)
        xh = xc[:, hs].astype(BF16)
        r = jax.nn.sigmoid(_dot(xh, wa_ref[h]) + ba[:, hs])
        ig = jax.nn.sigmoid(_dot(xh, wx_ref[h]) + bx[:, hs])
        log_a = -RG_C * r * sp[:, hs]
        parts.append((r, ig, jnp.exp(log_a), jnp.sqrt(-_expm1(2.0 * log_a))))
    return parts


def _conv_b(xbuf, bw_ref, bb, tt, kb):
    off = HALO_B - (kb - 1)
    xc = bb
    for k in range(kb):
        xc = xc + bw_ref[k:k + 1, :] * xbuf[pl.ds(off + k, tt), :]
    return xc


def rglru_fwd(u, bw, bb, wa, ba, wx, bx, lam, name, ride=None):
    t = u.shape[0]
    kb, c = bw.shape
    nh, hd, _ = wa.shape
    tt = _tile(t, 256, HALO_B)
    hb = tt // HALO_B

    def body(xr_ref, gr_ref, px_ref, bw_ref, bb_ref, wa_ref, ba_ref, wx_ref, bx_ref, lam_ref,
             yb_ref, hs_ref, xbuf, a_s, u_s, hc):
        i = pl.program_id(0)

        @pl.when(i == 0)
        def _():
            hc[...] = jnp.zeros_like(hc)

        xbuf[0:HALO_B, :] = px_ref[...] * (i > 0).astype(F32)
        xbuf[HALO_B:, :] = xr_ref[...]
        xc = _conv_b(xbuf, bw_ref, bb_ref[...], tt, kb)
        sp = _softplus(-lam_ref[...])
        gates = _rg_gates(xc, wa_ref, ba_ref[...], wx_ref, bx_ref[...], sp, nh, hd)
        for h, (_, ig, a, mult) in enumerate(gates):
            hs = slice(h * hd, (h + 1) * hd)
            a_s[:, hs] = a
            u_s[:, hs] = mult * ig * xc[:, hs]

        def step(g, hcur):
            base = pl.multiple_of(g * 8, 8)
            for j in range(8):
                hcur = a_s[pl.ds(base + j, 1), :] * hcur + u_s[pl.ds(base + j, 1), :]
                hs_ref[pl.ds(base + j, 1), :] = hcur
            return hcur

        hc[...] = lax.fori_loop(0, tt // 8, step, hc[...])
        yb_ref[...] = (hs_ref[...] * _gelu(gr_ref[...])).astype(BF16)

    cur = lambda col: pl.BlockSpec((tt, c), lambda i: (i, col))
    vec = pl.BlockSpec((1, c), lambda i: (0, 0))
    wsp = pl.BlockSpec((nh, hd, hd), lambda i: (0, 0, 0))
    return hosted_call(
        body, ride, name=name, grid=(t // tt,),
        in_specs=[cur(2), cur(3), pl.BlockSpec((HALO_B, c), lambda i: (jnp.maximum(i * hb - 1, 0), 2)),
                  pl.BlockSpec((kb, c), lambda i: (0, 0)), vec, wsp, vec, wsp, vec, vec],
        out_specs=[pl.BlockSpec((tt, c), lambda i: (i, 0))] * 2,
        out_shape=[SDS((t, c), BF16), SDS((t, c), F32)],
        scratch_shapes=[pltpu.VMEM((tt + HALO_B, c), F32), pltpu.VMEM((tt, c), F32),
                        pltpu.VMEM((tt, c), F32), pltpu.VMEM((1, c), F32)],
        operands=(u, u, u, bw, bb, wa, ba, wx, bx, lam))


def rglru_bwd(u, hs_all, dcat, bw, bb, wa, ba, wx, bx, lam, name):
    t = u.shape[0]
    kb, c = bw.shape
    nh, hd, _ = wa.shape
    tt = _tile(t, 256, HALO_B)
    nt, hb = t // tt, tt // HALO_B
    off = HALO_B - (kb - 1)

    def body(xr_ref, gr_ref, px_ref, hs_ref, ph_ref, dyb_ref, bw_ref, bb_ref, wa_ref, ba_ref, wx_ref, bx_ref,
             lam_ref, du_ref, pgrad_ref, dwa_ref, dwx_ref, xbuf, hbuf, a_s, g_s, dxbuf, cg, cdx):
        i = pl.program_id(0)

        @pl.when(i == 0)
        def _():
            cg[...] = jnp.zeros_like(cg)
            cdx[...] = jnp.zeros_like(cdx)
            pgrad_ref[...] = jnp.zeros_like(pgrad_ref)
            dwa_ref[...] = jnp.zeros_like(dwa_ref)
            dwx_ref[...] = jnp.zeros_like(dwx_ref)

        live = (i < nt - 1).astype(F32)
        xbuf[0:HALO_B, :] = px_ref[...] * live
        xbuf[HALO_B:, :] = xr_ref[...]
        hbuf[0:HALO_B, :] = ph_ref[...] * live
        hbuf[HALO_B:, :] = hs_ref[...]
        xc = _conv_b(xbuf, bw_ref, bb_ref[...], tt, kb)
        lam_v = lam_ref[...]
        sp = _softplus(-lam_v)
        gates = _rg_gates(xc, wa_ref, ba_ref[...], wx_ref, bx_ref[...], sp, nh, hd)
        gr = gr_ref[...]
        dyb = dyb_ref[...].astype(F32)
        g_s[...] = dyb * _gelu(gr)
        for h, (_, _, a, _) in enumerate(gates):
            a_s[:, h * hd:(h + 1) * hd] = a

        def step(g, carry):
            base = pl.multiple_of((tt // 8 - 1 - g) * 8, 8)
            for j in range(7, -1, -1):
                gt = g_s[pl.ds(base + j, 1), :] + carry
                g_s[pl.ds(base + j, 1), :] = gt
                carry = a_s[pl.ds(base + j, 1), :] * gt
            return carry

        cg[...] = lax.fori_loop(0, tt // 8, step, cg[...])
        hprev = hbuf[pl.ds(HALO_B - 1, tt), :]
        gfull = g_s[...]
        for h, (r, ig, a, mult) in enumerate(gates):
            hs = slice(h * hd, (h + 1) * hd)
            g, xch = gfull[:, hs], xc[:, hs]
            d_la = g * hprev[:, hs] * a - g * ig * xch * (a * a) / mult
            d_ig = g * mult * xch
            d_ra = d_la * (-RG_C * sp[:, hs]) * r * (1.0 - r)
            d_ia = d_ig * ig * (1.0 - ig)
            dsp = jnp.sum(d_la * (-RG_C) * r, axis=0, keepdims=True)
            pgrad_ref[kb + 1:kb + 2, hs] += jnp.sum(d_ra, axis=0, keepdims=True)
            pgrad_ref[kb + 2:kb + 3, hs] += jnp.sum(d_ia, axis=0, keepdims=True)
            pgrad_ref[kb + 3:kb + 4, hs] += dsp * (-jax.nn.sigmoid(-lam_v[:, hs]))
            xh, d_ra_b, d_ia_b = xch.astype(BF16), d_ra.astype(BF16), d_ia.astype(BF16)
            dwa_ref[h] += _dot_tn(xh, d_ra_b)
            dwx_ref[h] += _dot_tn(xh, d_ia_b)
            dxbuf[0:tt, hs] = g * mult * ig + _dot_nt(d_ra_b, wa_ref[h]) + _dot_nt(d_ia_b, wx_ref[h])
        dxbuf[tt:, :] = cdx[...]
        cdx[...] = dxbuf[0:HALO_B, :]
        d_xc = dxbuf[0:tt, :]
        pgrad_ref[kb:kb + 1, :] += jnp.sum(d_xc, axis=0, keepdims=True)
        d_xr = jnp.zeros((tt, c), F32)
        for k in range(kb):
            d_xr = d_xr + bw_ref[k:k + 1, :] * dxbuf[pl.ds(kb - 1 - k, tt), :]
            pgrad_ref[k:k + 1, :] += jnp.sum(d_xc * xbuf[pl.ds(off + k, tt), :], axis=0, keepdims=True)
        du_ref[:, 0:c] = d_xr.astype(BF16)
        du_ref[:, c:2 * c] = (dyb * hs_ref[...] * _gelu_grad(gr)).astype(BF16)

    cur = lambda col: pl.BlockSpec((tt, c), lambda i: (nt - 1 - i, col))
    prev = lambda col: pl.BlockSpec((HALO_B, c), lambda i: (jnp.maximum((nt - 1 - i) * hb - 1, 0), col))
    vec = pl.BlockSpec((1, c), lambda i: (0, 0))
    wsp = pl.BlockSpec((nh, hd, hd), lambda i: (0, 0, 0))
    return pl.pallas_call(
        body, name=name, grid=(nt,),
        in_specs=[cur(2), cur(3), prev(2), cur(0), prev(0), cur(1),
                  pl.BlockSpec((kb, c), lambda i: (0, 0)), vec, wsp, vec, wsp, vec, vec],
        out_specs=[pl.BlockSpec((tt, 2 * c), lambda i: (nt - 1 - i, 0)),
                   pl.BlockSpec((kb + 4, c), lambda i: (0, 0)), wsp, wsp],
        out_shape=[SDS((t, 2 * c), BF16), SDS((kb + 4, c), F32), SDS((nh, hd, hd), F32), SDS((nh, hd, hd), F32)],
        scratch_shapes=[pltpu.VMEM((tt + HALO_B, c), F32), pltpu.VMEM((tt + HALO_B, c), F32),
                        pltpu.VMEM((tt, c), F32), pltpu.VMEM((tt, c), F32), pltpu.VMEM((tt + HALO_B, c), F32),
                        pltpu.VMEM((1, c), F32), pltpu.VMEM((HALO_B, c), F32)],
        compiler_params=_cp("arbitrary"),
    )(u, u, u, hs_all, hs_all, dcat, bw, bb, wa, ba, wx, bx, lam)


def attn_fwd(qkv, nh, name):
    t = qkv.shape[0]
    dh = qkv.shape[1] // (3 * nh)
    bq = _tile(t, 256)
    scale = 1.0 / math.sqrt(dh)
    hp = 2 if nh % 2 == 0 else 1

    def body(q_ref, k_ref, v_ref, o_ref, s_ref, start_ref):
        pair, qi = pl.program_id(0), pl.program_id(1)
        row = lax.broadcasted_iota(jnp.int32, (bq, bq), 0)
        col = lax.broadcasted_iota(jnp.int32, (bq, bq), 1)
        tri = (row >= col).astype(BF16)

        def block(hh, kb, c, acc, mask):
            ks = pl.ds(pl.multiple_of(kb * bq, bq), bq)
            hs = pl.ds(hh * dh, dh)
            z = _dot_nt(q_ref[:, hs], k_ref[ks, hs]) * scale
            sp = _softplus(z)
            if mask is not None:
                sp = jnp.where(mask, sp, 0.0)
            lw = z - c - _dot(sp.astype(BF16), tri)
            if mask is not None:
                lw = jnp.where(mask, lw, -1e30)
            acc = acc + _dot(jnp.exp(lw).astype(BF16), v_ref[ks, hs])
            return c + jnp.sum(sp, axis=1, keepdims=True), acc

        state = []
        for hh in range(hp):
            c, acc = block(hh, qi, jnp.zeros((bq, 1), F32), jnp.zeros((bq, dh), F32), col < row)
            state.append(block(hh, jnp.maximum(qi - 1, 0), c, acc, qi > 0))

        for hh, (c, acc) in enumerate(state):
            c, acc = lax.fori_loop(0, jnp.maximum(qi - 1, 0),
                                   lambda it, st, hh=hh: block(hh, qi - 2 - it, st[0], st[1], None), (c, acc))
            o_ref[:, pl.ds(hh * dh, dh)] = acc.astype(BF16)
            s_ref[hh] = jnp.broadcast_to(c, (bq, LANES))
            start_ref[hp * pair + hh, qi] = 0.0

    whole = lambda base: pl.BlockSpec((t, hp * dh), lambda h, qi: (0, base + h))
    return pl.pallas_call(
        body, name=name, grid=(nh // hp, t // bq),
        in_specs=[pl.BlockSpec((bq, hp * dh), lambda h, qi: (qi, h)), whole(nh // hp), whole(2 * nh // hp)],
        out_specs=[pl.BlockSpec((bq, hp * dh), lambda h, qi: (qi, h)),
                   pl.BlockSpec((hp, bq, LANES), lambda h, qi: (h, qi, 0)),
                   pl.BlockSpec(memory_space=pltpu.SMEM)],
        out_shape=[SDS((t, nh * dh), BF16), SDS((nh, t, LANES), F32), SDS((nh, t // bq), F32)],
        compiler_params=_cp("arbitrary", "arbitrary"),
    )(qkv, qkv, qkv)


def attn_bwd(qkv, do, s_tot, start, nh, name, ride=None):
    t = qkv.shape[0]
    dh = qkv.shape[1] // (3 * nh)
    bq = _tile(t, 256)
    scale = 1.0 / math.sqrt(dh)
    hp = 2 if nh % 2 == 0 else 1

    def body(start_ref, q_ref, k_ref, v_ref, do_ref, s_ref, dq_ref, dk_ref, dv_ref):
        pair, qi = pl.program_id(0), pl.program_id(1)

        @pl.when(qi == 0)
        def _():
            dk_ref[...] = jnp.zeros_like(dk_ref)
            dv_ref[...] = jnp.zeros_like(dv_ref)

        row = lax.broadcasted_iota(jnp.int32, (bq, bq), 0)
        col = lax.broadcasted_iota(jnp.int32, (bq, bq), 1)
        tri_suffix = (row >= col).astype(BF16)
        tri_prefix = (row <= col).astype(BF16)

        def block(hh, kb, psp, pg, dq, mask):
            ks = pl.ds(pl.multiple_of(kb * bq, bq), bq)
            hs = pl.ds(hh * dh, dh)
            q, do_b, k_b, v_b = q_ref[:, hs], do_ref[:, hs], k_ref[ks, hs], v_ref[ks, hs]
            z = _dot_nt(q, k_b) * scale
            sp_all = _softplus(z)
            sp = sp_all if mask is None else jnp.where(mask, sp_all, 0.0)
            psp = psp + jnp.sum(sp, axis=1, keepdims=True)
            lw = z - (s_ref[hh, :, 0:1] - psp) - _dot(sp.astype(BF16), tri_suffix)
            if mask is not None:
                lw = jnp.where(mask, lw, -1e30)
            a = jnp.exp(lw)
            g = _dot_nt(do_b, v_b) * a
            dz = g - (pg + _dot(g.astype(BF16), tri_prefix)) * jnp.exp(z - sp_all)
            if mask is not None:
                dz = jnp.where(mask, dz, 0.0)
            dzs = (dz * scale).astype(BF16)
            dk_ref[ks, hs] += _dot_tn(dzs, q)
            dv_ref[ks, hs] += _dot_tn(a.astype(BF16), do_b)
            return psp, pg + jnp.sum(g, axis=1, keepdims=True), dq + _dot(dzs, k_b)

        state = []
        for hh in range(hp):
            first = 0
            init = (jnp.zeros((bq, 1), F32), jnp.zeros((bq, 1), F32), jnp.zeros((bq, dh), F32))
            state.append((first,) + lax.fori_loop(
                first, jnp.maximum(qi - 1, first),
                lambda kb, cr, hh=hh: block(hh, kb, cr[0], cr[1], cr[2], None), init))
        for hh, (first, psp, pg, dq) in enumerate(state):
            psp, pg, dq = block(hh, jnp.maximum(qi - 1, 0), psp, pg, dq, jnp.logical_and(qi > 0, first < qi))
            _, _, dq = block(hh, qi, psp, pg, dq, col < row)
            dq_ref[:, pl.ds(hh * dh, dh)] = dq.astype(BF16)

    once = dict(pipeline_mode=pl.Buffered(1))
    whole = lambda base: pl.BlockSpec((t, hp * dh), lambda h, qi: (0, base + h), **once)
    qblk = pl.BlockSpec((bq, hp * dh), lambda h, qi: (qi, h))
    acc = pl.BlockSpec((t, hp * dh), lambda h, qi: (0, h))
    return hosted_call(
        body, ride, name=name, grid=(nh // hp, t // bq),
        in_specs=[pl.BlockSpec(memory_space=pltpu.SMEM), qblk, whole(nh // hp), whole(2 * nh // hp), qblk,
                  pl.BlockSpec((hp, bq, LANES), lambda h, qi: (h, qi, 0))],
        out_specs=[qblk, acc, acc],
        out_shape=[SDS((t, nh * dh), BF16), SDS((t, nh * dh), F32), SDS((t, nh * dh), F32)],
        scratch_shapes=[], operands=(start, qkv, qkv, qkv, do, s_tot))


def all_gather(x, name, axis=None, first_layer_only=False):
    if axis is None:
        out_shape = (N_DEV,) + x.shape
    else:
        out_shape = x.shape[:axis] + (N_DEV * x.shape[axis],) + x.shape[axis + 1:]

    def body(x_full_ref, out_full_ref, send_sems, recv_sems, local_sem):
        mx, my, mc = lax.axis_index("x"), lax.axis_index("y"), lax.axis_index("c")
        me, sibling = (mx, my, mc), (mx, my, 1 - mc)
        chips = [(1 - mx, my), (mx, 1 - my), (1 - mx, 1 - my)]
        x_ref = x_full_ref.at[0] if first_layer_only else x_full_ref
        out_ref = out_full_ref.at[0] if first_layer_only else out_full_ref
        ax = axis - 1 if first_layer_only else axis

        def slot(px, py, pc):
            return _shard_of(out_ref, ax, 4 * px + 2 * py + pc, None if axis is None else x.shape[axis])

        def copy(k, block, to, src=None):
            return pltpu.make_async_remote_copy(
                src_ref=slot(*block) if src is None else src, dst_ref=slot(*block),
                send_sem=send_sems.at[k], recv_sem=recv_sems.at[k], device_id=to, device_id_type=MESH)

        mine = pltpu.make_async_copy(x_ref, slot(*me), local_sem)
        mine.start()
        first = [copy(0, me, sibling, src=x_ref)]
        first += [copy(1 + j, me, (*chip, mc), src=x_ref) for j, chip in enumerate(chips)]
        for cp in first:
            cp.start()
        passed = [copy(4 + j, (*chip, mc), sibling) for j, chip in enumerate(chips)]
        for j, chip in enumerate(chips):
            copy(1 + j, (*chip, mc), me).wait_recv()
            passed[j].start()
        copy(0, sibling, me).wait_recv()
        for j, chip in enumerate(chips):
            copy(4 + j, (*chip, 1 - mc), me).wait_recv()
        for cp in first + passed:
            cp.wait_send()
        mine.wait()

    return pl.pallas_call(
        body, name=name, out_shape=SDS(out_shape, x.dtype), in_specs=[ANY], out_specs=ANY,
        scratch_shapes=[pltpu.SemaphoreType.DMA((7,)), pltpu.SemaphoreType.DMA((7,)), pltpu.SemaphoreType.DMA],
    )(x)


def reduce_adamw(gs, w, m, v, name):
    s, r, c = gs.shape
    tr = _tile(r, max(16, (128 * 1024) // c), 16)

    def body(gs_ref, w_ref, m_ref, v_ref, g_out, d_out, m_out, v_out):
        g = gs_ref[0].astype(F32)
        for j in range(1, s):
            g = g + gs_ref[j].astype(F32)
        m_new = ADAM_B1 * m_ref[...] + (1.0 - ADAM_B1) * g
        v_new = ADAM_B2 * v_ref[...] + (1.0 - ADAM_B2) * (g * g)
        m_hat = m_new / (1.0 - ADAM_B1 ** ADAM_STEP)
        v_hat = v_new / (1.0 - ADAM_B2 ** ADAM_STEP)
        g_out[...] = g
        d_out[...] = -ADAM_LR * (m_hat / (jnp.sqrt(v_hat) + ADAM_EPS) + ADAM_WD * w_ref[...])
        m_out[...] = m_new
        v_out[...] = v_new

    row = pl.BlockSpec((tr, c), lambda i: (i, 0))
    return pl.pallas_call(
        body, name=name, grid=(r // tr,),
        in_specs=[pl.BlockSpec((s, tr, c), lambda i: (0, i, 0)), row, row, row], out_specs=[row] * 4,
        out_shape=[SDS((r, c), F32)] * 4, compiler_params=_cp("parallel"),
    )(gs, w, m, v)


BIG = {
    "w_in_rec": 2, "w_out_rec": 1, "w_qkv": 2, "w_o_attn": 1,
    "w_mlp_up": 2, "w_mlp_down": 1, "w_ple_proj": 2, "w_ple_gate": 1,
}
REPLICATED = ["norm_mix_g", "norm_mlp_g", "norm_ple_g", "norm_f_g", "conv_a_b", "ln_a_g", "ln_a_b", "conv_b_b",
              "w_rg_a", "b_rg_a", "w_rg_x", "b_rg_x", "rg_lambda"]
CONV_W = ["conv_a_w", "conv_b_w"]
WEIGHTS = ["norm_mix_g", "norm_mlp_g", "norm_ple_g", "norm_f_g", "w_in_rec", "conv_a_w", "conv_a_b", "ln_a_g",
           "ln_a_b", "conv_b_w", "conv_b_b", "w_rg_a", "b_rg_a", "w_rg_x", "b_rg_x", "rg_lambda", "w_out_rec",
           "w_qkv", "w_o_attn", "w_mlp_up", "w_mlp_down", "w_ple_proj", "w_ple_gate"]


def _pack(arrays):
    flat = jnp.concatenate([a.reshape(-1) for a in arrays])
    pad = (-flat.shape[0]) % (8 * LANES)
    return jnp.pad(flat, (0, pad)).reshape(-1, LANES)


def _unpack(packed, shapes, lead=()):
    flat = packed.reshape(lead + (-1,))
    out, pos = [], 0
    for shp in shapes:
        size = math.prod(shp)
        out.append(flat[..., pos:pos + size].reshape(lead + tuple(shp)))
        pos += size
    return out


def _step(x, p, loss_target, w, mom, var):
    dev = 4 * lax.axis_index("x") + 2 * lax.axis_index("y") + lax.axis_index("c")
    depth = w["norm_mix_g"].shape[0]
    h = x[0]
    nh = SB_HEADS

    wb = {n: w[n].astype(BF16) for n in BIG}
    wg = {n: None for n in BIG}
    wg["w_in_rec"] = all_gather(wb["w_in_rec"], name="ag0_w_in_rec", axis=BIG["w_in_rec"], first_layer_only=True)
    mix_names = lambda layer: ("w_in_rec", "w_out_rec") if layer % 2 == 0 else ("w_qkv", "w_o_attn")

    def gather_ride(targets):
        ride = Ride()
        for n, layer in targets:
            ride.gather(wb[n], BIG[n], layer, wg[n])
        return ride if targets else None

    def gathered(targets, outs):
        for (n, _), a in zip(targets, outs[len(outs) - len(targets):]):
            wg[n] = a
        return outs[:len(outs) - len(targets)]

    def mm_gathering(targets, *args, **kw):
        return gathered(targets, mm_w(*args, ride=gather_ride(targets), **kw))

    wide = dict(tm=256, tn=2048)
    conv_shapes = [w[n].shape for n in CONV_W]
    conv_all = all_gather(_pack([w[n] for n in CONV_W]), name="ag_conv_w")
    conv_full = [jnp.moveaxis(a, 0, -2).reshape(a.shape[1:-1] + (-1,))
                 for a in _unpack(conv_all, conv_shapes, lead=(N_DEV,))]
    conv_a_w, conv_b_w = conv_full
    vec = lambda a: a.reshape(1, -1)

    saved = []
    for i in range(depth):
        j = i // 2
        s = {"h0": h}
        in_name, out_name = mix_names(i)
        s["in_name"], s["out_name"] = in_name, out_name
        nxt = i + 1
        if nxt < depth:
            t_in, t_out = [(mix_names(nxt)[0], nxt // 2)], [(mix_names(nxt)[1], nxt // 2), ("w_ple_gate", nxt)]
            t_up, t_down, t_gate = [("w_mlp_down", nxt)], [("w_mlp_up", nxt)], [("w_ple_proj", nxt)]
        else:
            t_in = t_out = t_up = t_down = t_gate = []
        t_conf = t_rg = []
        if i == 0:
            t_in = t_in + [("w_out_rec", 0)]
            t_conf, t_rg = [("w_mlp_up", 0)], [("w_ple_gate", 0), ("w_ple_proj", 0)]
            t_out, t_up = [("w_mlp_down", 0)], t_up + t_out
        s["hn1"] = rms_fwd(h, w["norm_mix_g"][i], name="rms_fwd")
        if i % 2 == 0:
            (s["u"],) = mm_gathering(t_in, s["hn1"], wg[in_name], j, transpose_w=False, out_dtypes=[F32],
                                     name="mm_in_rec")
            ya, s["ya1"] = gathered(t_conf, conf_fwd(
                s["u"], conv_a_w[j], vec(w["conv_a_b"][j]), vec(w["ln_a_g"][j]), vec(w["ln_a_b"][j]),
                name="conf_fwd", ride=gather_ride(t_conf)))
            yb, s["hs"] = gathered(t_rg, rglru_fwd(
                s["u"], conv_b_w[j], vec(w["conv_b_b"][j]), w["w_rg_a"][j].astype(BF16), vec(w["b_rg_a"][j]),
                w["w_rg_x"][j].astype(BF16), vec(w["b_rg_x"][j]), vec(w["rg_lambda"][j]), name="rglru_fwd",
                ride=gather_ride(t_rg)))
            s["mix_in"] = jnp.concatenate([ya, yb], axis=1)
        else:
            (s["qkv"],) = mm_gathering(t_in, s["hn1"], wg[in_name], j, transpose_w=False, out_dtypes=[BF16],
                                       name="mm_qkv")
            s["mix_in"], s["s_tot"], s["start"] = attn_fwd(s["qkv"], nh, name="attn_fwd")
        (h,) = mm_gathering(t_out, s["mix_in"], wg[out_name], j, transpose_w=False, out_dtypes=[F32],
                            extras=[h], epilogue=lambda acc, res: (res + acc,), name="mm_mix_out", **wide)
        s["h1"] = h
        s["hn2"] = rms_fwd(h, w["norm_mlp_g"][i], name="rms_fwd")
        relu = lambda acc: jnp.maximum(acc, 0.0)
        s["up"], s["act"] = mm_gathering(t_up, s["hn2"], wg["w_mlp_up"], i, transpose_w=False,
                                         out_dtypes=[BF16, BF16], name="mm_mlp_up", tm=1024,
                                         epilogue=lambda acc: (acc, relu(acc) * relu(acc)))
        (h,) = mm_gathering(t_down, s["act"], wg["w_mlp_down"], i, transpose_w=False, out_dtypes=[F32],
                            tm=1024, extras=[h], epilogue=lambda acc, res: (res + acc,),
                            name="mm_mlp_down")
        s["h2"] = h
        s["hn3"] = rms_fwd(h, w["norm_ple_g"][i], name="rms_fwd")
        s["p"] = p[i, 0].astype(BF16)
        (s["pp"],) = mm_w(s["p"], wg["w_ple_proj"], i, transpose_w=False, out_dtypes=[F32], name="mm_ple_proj")
        h, s["gz"] = mm_gathering(t_gate, s["hn3"], wg["w_ple_gate"], i, transpose_w=False,
                                  out_dtypes=[F32, BF16], extras=[h, s["pp"]], name="mm_ple_gate", **wide,
                                  epilogue=lambda acc, res, pp: (res + pp * jax.nn.sigmoid(acc), acc))
        saved.append(s)

    dh, dg_f, loss_part = loss_head(h, w["norm_f_g"], loss_target[0], name="loss_head")
    loss = lax.psum(loss_part[0, 0], ("x", "y", "c"))

    acc = {n: None for n in BIG}
    part = {n: [None] * w[n].shape[0] for n in REPLICATED + CONV_W if n != "norm_f_g"}
    part["norm_f_g"] = dg_f[0]

    recv = {n: None for n in BIG}

    def exchanging(targets):
        ride = Ride()
        for n, layer in targets:
            ride.exchange(acc[n], BIG[n], layer, recv[n])
        return ride

    def landed(targets, arrays):
        for (n, _), a in zip(targets, arrays):
            recv[n] = a

    def dw(name, a, b, layer, targets=()):
        outs = mm_dw(a, b, acc[name], layer, wg[name].shape[0], name=f"dw_{name}",
                     ride=exchanging(targets) if targets else None)
        acc[name] = outs[0] if targets else outs
        landed(targets, outs[1:] if targets else [])

    def mm_exchanging(targets, *args, **kw):
        outs = mm_w(*args, ride=exchanging(targets), **kw)
        landed(targets, outs[len(outs) - len(targets):])
        return outs[:len(outs) - len(targets)]

    for i in reversed(range(depth)):
        j = i // 2
        s = saved[i]
        in_name, out_name = s["in_name"], s["out_name"]
        d_pp, d_gz = ple_bwd(dh, s["gz"], s["pp"], name="ple_bwd")
        dw("w_ple_proj", s["p"], d_pp, i)
        dw("w_ple_gate", s["hn3"], d_gz, i)
        (d_hn3,) = mm_exchanging([("w_ple_proj", i), ("w_ple_gate", i)], d_gz, wg["w_ple_gate"], i,
                                 transpose_w=True, out_dtypes=[BF16], name="mmT_ple_gate", tn=2048)
        dh, dh_b, dg = rms_bwd(s["h2"], w["norm_ple_g"][i], d_hn3, dh, name="rms_bwd")
        part["norm_ple_g"][i] = dg[0]
        (d_up,) = mm_w(dh_b, wg["w_mlp_down"], i, transpose_w=True, out_dtypes=[BF16], tm=1024,
                       extras=[s["up"]], name="mmT_mlp_down",
                       epilogue=lambda acc_, up: (acc_ * (2.0 * jnp.maximum(up.astype(F32), 0.0)),))
        dw("w_mlp_down", s["act"], dh_b, i)
        dw("w_mlp_up", s["hn2"], d_up, i, targets=[("w_mlp_down", i)])
        (d_hn2,) = mm_w(d_up, wg["w_mlp_up"], i, transpose_w=True, out_dtypes=[BF16], tm=1024, name="mmT_mlp_up")
        up_grad = [("w_mlp_up", i)]
        dh, dh_b, dg = rms_bwd(s["h1"], w["norm_mlp_g"][i], d_hn2, dh, name="rms_bwd")
        part["norm_mlp_g"][i] = dg[0]
        dw(out_name, s["mix_in"], dh_b, j)
        if i % 2 == 0:
            (d_cat,) = mm_exchanging([(out_name, j)], dh_b, wg[out_name], j, transpose_w=True,
                                     out_dtypes=[BF16], name="mmT_mix_out", tn=2048)
            du_a, pg_a, *sent = conf_bwd(s["u"], s["ya1"], d_cat, conv_a_w[j], vec(w["ln_a_g"][j]),
                                         vec(w["ln_a_b"][j]), name="conf_bwd", ride=exchanging(up_grad))
            landed(up_grad, sent)
            du_b, pg_b, d_wa, d_wx = rglru_bwd(
                s["u"], s["hs"], d_cat, conv_b_w[j], vec(w["conv_b_b"][j]), w["w_rg_a"][j].astype(BF16),
                vec(w["b_rg_a"][j]), w["w_rg_x"][j].astype(BF16), vec(w["b_rg_x"][j]), vec(w["rg_lambda"][j]),
                name="rglru_bwd")
            ka, kb = conv_a_w.shape[1], conv_b_w.shape[1]
            part["conv_a_w"][j], part["conv_a_b"][j] = pg_a[:ka], pg_a[ka]
            part["ln_a_g"][j], part["ln_a_b"][j] = pg_a[ka + 1], pg_a[ka + 2]
            part["conv_b_w"][j], part["conv_b_b"][j] = pg_b[:kb], pg_b[kb]
            part["b_rg_a"][j], part["b_rg_x"][j], part["rg_lambda"][j] = pg_b[kb + 1], pg_b[kb + 2], pg_b[kb + 3]
            part["w_rg_a"][j], part["w_rg_x"][j] = d_wa, d_wx
            d_mix = jnp.concatenate([du_a, du_b], axis=1)
        else:
            (d_o,) = mm_exchanging([(out_name, j)], dh_b, wg[out_name], j, transpose_w=True,
                                   out_dtypes=[BF16], name="mmT_mix_out", tn=2048)
            dq, dk, dv, *sent = attn_bwd(s["qkv"], d_o, s["s_tot"], s["start"], nh, name="attn_bwd",
                                         ride=exchanging(up_grad))
            landed(up_grad, sent)
            d_mix = jnp.concatenate([dq, dk.astype(BF16), dv.astype(BF16)], axis=1)
        dw(in_name, s["hn1"], d_mix, j)
        (d_hn1,) = mm_exchanging([(in_name, j)], d_mix, wg[in_name], j, transpose_w=True, out_dtypes=[BF16],
                                 name="mmT_mix_in")
        dh, _, dg = rms_bwd(s["h0"], w["norm_mix_g"][i], d_hn1, dh, name="rms_bwd")
        part["norm_mix_g"][i] = dg[0]

    grads, deltas, new_m, new_v = {}, {}, {}, {}

    def finish(name, outs, shape):
        for d, o in zip((grads, deltas, new_m, new_v), outs):
            d[name] = o.reshape(shape)

    for n in BIG:
        _, nl, r, c = recv[n].shape
        flat = lambda a: a.reshape(nl * r, c)
        finish(n, reduce_adamw(recv[n].reshape(N_DEV, nl * r, c), flat(w[n]), flat(mom[n]), flat(var[n]),
                               name=f"adamw_{n}"), w[n].shape)

    full = {n: (part[n] if n == "norm_f_g" else jnp.stack(part[n])) for n in REPLICATED + CONV_W}
    rep = _pack([full[n] for n in REPLICATED])
    conv = _pack([full[n] for n in CONV_W])
    small = all_gather(jnp.concatenate([rep, conv]), name="ag_small_grads")
    rep_shapes = [w[n].shape for n in REPLICATED]
    outs = reduce_adamw(small[:, :rep.shape[0]], _pack([w[n] for n in REPLICATED]),
                        _pack([mom[n] for n in REPLICATED]), _pack([var[n] for n in REPLICATED]), name="adamw_small")
    for d, o in zip((grads, deltas, new_m, new_v), outs):
        for n, a in zip(REPLICATED, _unpack(o, rep_shapes)):
            d[n] = a
    conv_parts = _unpack(small[:, rep.shape[0]:], [full[n].shape for n in CONV_W], lead=(N_DEV,))
    width = w["conv_a_w"].shape[-1]
    mine = [lax.dynamic_slice_in_dim(a, dev * width, width, axis=a.ndim - 1) for a in conv_parts]
    packed = jnp.stack([_pack([a[d] for a in mine]) for d in range(N_DEV)])
    outs = reduce_adamw(packed, _pack([w[n] for n in CONV_W]), _pack([mom[n] for n in CONV_W]),
                        _pack([var[n] for n in CONV_W]), name="adamw_conv_w")
    for d, o in zip((grads, deltas, new_m, new_v), outs):
        for n, a in zip(CONV_W, _unpack(o, conv_shapes)):
            d[n] = a
    return loss, dh[None], grads, deltas, new_m, new_v


def kernel(x, p, norm_mix_g, norm_mlp_g, norm_ple_g, norm_f_g, w_in_rec, conv_a_w, conv_a_b, ln_a_g, ln_a_b, conv_b_w, conv_b_b, w_rg_a, b_rg_a, w_rg_x, b_rg_x, rg_lambda, w_out_rec, w_qkv, w_o_attn, w_mlp_up, w_mlp_down, w_ple_proj, w_ple_gate, loss_target, m_norm_mix_g, m_norm_mlp_g, m_norm_ple_g, m_norm_f_g, m_w_in_rec, m_conv_a_w, m_conv_a_b, m_ln_a_g, m_ln_a_b, m_conv_b_w, m_conv_b_b, m_w_rg_a, m_b_rg_a, m_w_rg_x, m_b_rg_x, m_rg_lambda, m_w_out_rec, m_w_qkv, m_w_o_attn, m_w_mlp_up, m_w_mlp_down, m_w_ple_proj, m_w_ple_gate, v_norm_mix_g, v_norm_mlp_g, v_norm_ple_g, v_norm_f_g, v_w_in_rec, v_conv_a_w, v_conv_a_b, v_ln_a_g, v_ln_a_b, v_conv_b_w, v_conv_b_b, v_w_rg_a, v_b_rg_a, v_w_rg_x, v_b_rg_x, v_rg_lambda, v_w_out_rec, v_w_qkv, v_w_o_attn, v_w_mlp_up, v_w_mlp_down, v_w_ple_proj, v_w_ple_gate):
    given = dict(locals())
    w = {n: given[n] for n in WEIGHTS}
    mom = {n: given["m_" + n] for n in WEIGHTS}
    var = {n: given["v_" + n] for n in WEIGHTS}
    loss, grad_x, grads, deltas, new_m, new_v = _step(x, p, loss_target, w, mom, var)
    return (loss, grad_x, *[grads[n] for n in WEIGHTS], *[deltas[n] for n in WEIGHTS],
            *[new_m[n] for n in WEIGHTS], *[new_v[n] for n in WEIGHTS])
```
